```python
import math, functools
import jax, jax.numpy as jnp
from jax import lax
import numpy as np

D_MODEL = 1024
BATCH = 2
SEQ = 8192
DEPTH = 1
DEC_BATCH = 128
DEC_SEQ = 1
PAST_LEN = 8192
PAGE_SIZE = 128

MLA_HEADS = 8
Q_LORA = 384
KV_LORA = 256
NOPE_DIM = 64
ROPE_DIM = 32
V_DIM = 64
ROPE_THETA = 10000.0
Q_BLOCK = 128
M_HEADS = 4
M_DIM = 128
CHUNK = 128
MLA_WIDTH = MLA_HEADS * V_DIM
MLSTM_WIDTH = M_HEADS * M_DIM
MIX_WIDTH = MLA_WIDTH + MLSTM_WIDTH
D_FF = 2816
PLE_DIM = 256
EPS = 1e-6
OFF_KV = Q_LORA
OFF_KR = OFF_KV + KV_LORA
OFF_MQ = OFF_KR + ROPE_DIM
OFF_MK = OFF_MQ + MLSTM_WIDTH
OFF_MV = OFF_MK + MLSTM_WIDTH
OFF_MO = OFF_MV + MLSTM_WIDTH
OFF_MI = OFF_MO + MLSTM_WIDTH
OFF_MF = OFF_MI + M_HEADS
IN_COLS = OFF_MF + M_HEADS

kernel_name = "hymba_mla_mlstm_macaron_step"


def rmsnorm(x, g):
    xf = x.astype(jnp.float32)
    y = xf * lax.rsqrt(jnp.mean(xf * xf, axis=-1, keepdims=True) + EPS)
    return (y * g.astype(jnp.float32)).astype(x.dtype)


def rope(x, pos):
    half = ROPE_DIM // 2
    inv = ROPE_THETA ** (-jnp.arange(half, dtype=jnp.float32) / half)
    ang = pos.astype(jnp.float32)[:, None] * inv[None, :]
    shape = (1, pos.shape[0]) + (1,) * (x.ndim - 3) + (half,)
    cos = jnp.cos(ang).reshape(shape)
    sin = jnp.sin(ang).reshape(shape)
    xf = x.astype(jnp.float32)
    x1, x2 = xf[..., :half], xf[..., half:]
    return jnp.concatenate([x1 * cos - x2 * sin, x1 * sin + x2 * cos], axis=-1).astype(x.dtype)


def swiglu(x, wg, wu, wd):
    return (jax.nn.silu(x @ wg) * (x @ wu)) @ wd


def prompt_attention(q_nope, q_rope, ckv, krope, w_uk, w_uv):
    B, S, H, _ = q_nope.shape
    qb = math.gcd(S, Q_BLOCK)
    nb = S // qb
    scale = (NOPE_DIM + ROPE_DIM) ** -0.5
    k_nope = (ckv @ w_uk).reshape(B, S, H, NOPE_DIM)
    v = (ckv @ w_uv).reshape(B, S, H, V_DIM)
    qn = q_nope.reshape(B, nb, qb, H, NOPE_DIM).swapaxes(0, 1)
    qr = q_rope.reshape(B, nb, qb, H, ROPE_DIM).swapaxes(0, 1)
    kpos = jnp.arange(S)

    def block(args):
        qn_b, qr_b, bi = args
        s = (jnp.einsum('bqhd,bkhd->bhqk', qn_b, k_nope, preferred_element_type=jnp.float32)
             + jnp.einsum('bqhr,bkr->bhqk', qr_b, krope, preferred_element_type=jnp.float32)) * scale
        qpos = bi * qb + jnp.arange(qb)
        s = jnp.where(kpos[None, :] <= qpos[:, None], s, -jnp.inf)
        p = jax.nn.softmax(s, axis=-1)
        return jnp.einsum('bhqk,bkhd->bqhd', p.astype(v.dtype), v)

    out = lax.map(block, (qn, qr, jnp.arange(nb)))
    return out.swapaxes(0, 1).reshape(B, S, H * V_DIM)


def sample_attention(q_nope, q_rope, ckv, krope, w_uk, w_uv, cache_c, cache_r, page_table):
    Bd, Sq, H, _ = q_nope.shape
    scale = (NOPE_DIM + ROPE_DIM) ** -0.5
    past_c = cache_c[page_table].reshape(Bd, -1, KV_LORA)
    past_r = cache_r[page_table].reshape(Bd, -1, ROPE_DIM)
    n_past = past_c.shape[1]
    q_abs = jnp.einsum('bqhn,chn->bqhc', q_nope, w_uk.reshape(KV_LORA, H, NOPE_DIM))
    s_past = (jnp.einsum('bqhc,bkc->bhqk', q_abs, past_c, preferred_element_type=jnp.float32)
              + jnp.einsum('bqhr,bkr->bhqk', q_rope, past_r, preferred_element_type=jnp.float32))
    s_new = (jnp.einsum('bqhc,bkc->bhqk', q_abs, ckv, preferred_element_type=jnp.float32)
             + jnp.einsum('bqhr,bkr->bhqk', q_rope, krope, preferred_element_type=jnp.float32))
    causal = jnp.tril(jnp.ones((Sq, Sq), dtype=bool))
    s_new = jnp.where(causal, s_new, -jnp.inf)
    p = jax.nn.softmax(jnp.concatenate([s_past, s_new], axis=-1) * scale, axis=-1)
    p = p.astype(ckv.dtype)
    o_lat = (jnp.einsum('bhqk,bkc->bqhc', p[..., :n_past], past_c)
             + jnp.einsum('bhqk,bkc->bqhc', p[..., n_past:], ckv))
    out = jnp.einsum('bqhc,chv->bqhv', o_lat, w_uv.reshape(KV_LORA, H, V_DIM))
    return out.reshape(Bd, Sq, H * V_DIM)


def mlstm_chunkwise(q, k, v, ig, lf, C0, n0, m0):
    B, S, H, D = q.shape
    L = math.gcd(S, CHUNK)
    nc = S // L
    causal = jnp.tril(jnp.ones((L, L), dtype=bool))

    def to_chunks(a):
        return a.reshape((B, nc, L) + a.shape[2:]).swapaxes(0, 1)

    def step(carry, xs):
        C, n, m = carry
        qc, kc, vc, ic, fc = xs
        bT = jnp.cumsum(fc, axis=1).transpose(0, 2, 1)
        iT = ic.transpose(0, 2, 1)
        Dm = bT[..., :, None] - bT[..., None, :] + iT[..., None, :]
        Dm = jnp.where(causal, Dm, -jnp.inf)
        inter = bT + m[..., None]
        m_row = jnp.maximum(inter, jnp.max(Dm, axis=-1))
        w_inter = jnp.exp(inter - m_row)
        Sqk = jnp.einsum('bthd,bshd->bhts', qc, kc) * jnp.exp(Dm - m_row[..., None])
        w_t = w_inter.transpose(0, 2, 1)[..., None]
        num = jnp.einsum('bhts,bshd->bthd', Sqk, vc) + jnp.einsum('bthd,bhde->bthe', qc, C) * w_t
        den = jnp.sum(Sqk, axis=-1) + jnp.einsum('bthd,bhd->bht', qc, n) * w_inter
        den = jnp.maximum(jnp.abs(den), jnp.exp(-m_row))
        h = num / den.transpose(0, 2, 1)[..., None]
        bL = bT[..., -1]
        g = bL[..., None] - bT + iT
        m_new = jnp.maximum(bL + m, jnp.max(g, axis=-1))
        a = jnp.exp(bL + m - m_new)
        ws = jnp.exp(g - m_new[..., None])
        C_new = a[..., None, None] * C + jnp.einsum('bhs,bshd,bshe->bhde', ws, kc, vc)
        n_new = a[..., None] * n + jnp.einsum('bhs,bshd->bhd', ws, kc)
        return (C_new, n_new, m_new), h

    (C, n, m), hs = lax.scan(step, (C0, n0, m0), (to_chunks(q), to_chunks(k), to_chunks(v), to_chunks(ig), to_chunks(lf)))
    return hs.swapaxes(0, 1).reshape(B, S, H, D), C, n, m


def trunk_layer(x, pe, pos, attn_fn, C0, n0, m0, lw):
    B, S, _ = x.shape
    h = x + 0.5 * swiglu(rmsnorm(x, lw['g_ff1']), lw['w_ff1_gate'], lw['w_ff1_up'], lw['w_ff1_down'])
    z = rmsnorm(h, lw['g_mix']) @ lw['w_in']
    zq, zkv, zkr, mq, mk, mv, mo, mi, mf = jnp.split(
        z, [OFF_KV, OFF_KR, OFF_MQ, OFF_MK, OFF_MV, OFF_MO, OFF_MI, OFF_MF], axis=-1)
    ckv = rmsnorm(zkv, lw['g_kv'])
    krope = rope(zkr, pos)
    q = (rmsnorm(zq, lw['g_q']) @ lw['w_uq']).reshape(B, S, MLA_HEADS, NOPE_DIM + ROPE_DIM)
    q_nope = q[..., :NOPE_DIM]
    q_rope = rope(q[..., NOPE_DIM:], pos)
    a = attn_fn(q_nope, q_rope, ckv, krope, lw['w_uk'], lw['w_uv'])
    f32 = jnp.float32
    qm = mq.astype(f32).reshape(B, S, M_HEADS, M_DIM)
    km = mk.astype(f32).reshape(B, S, M_HEADS, M_DIM) * (M_DIM ** -0.5)
    vm = mv.astype(f32).reshape(B, S, M_HEADS, M_DIM)
    ig = mi.astype(f32) + lw['b_gate_i'].astype(f32)
    lf = jax.nn.log_sigmoid(mf.astype(f32) + lw['b_gate_f'].astype(f32))
    hm, C, n, m = mlstm_chunkwise(qm, km, vm, ig, lf, C0, n0, m0)
    hm = jax.nn.sigmoid(mo.astype(f32)).reshape(B, S, M_HEADS, M_DIM) * hm
    hm = rmsnorm(hm, lw['g_mlstm_out']).reshape(B, S, MLSTM_WIDTH).astype(x.dtype)
    mix = jnp.concatenate([rmsnorm(a, lw['g_attn_out']), hm], axis=-1) @ lw['w_out']
    h = h + mix
    h = h + 0.5 * swiglu(rmsnorm(h, lw['g_ff2']), lw['w_ff2_gate'], lw['w_ff2_up'], lw['w_ff2_down'])
    h = h + jax.nn.sigmoid(rmsnorm(h, lw['g_ple']) @ lw['w_ple_gate']) * (pe @ lw['w_ple_proj'])
    return h, ckv, krope, C, n, m


def setup_inputs(seed: int = 0) -> dict:
    key = jax.random.key(seed)
    ks = iter(jax.random.split(key, 64))
    f32 = jnp.float32

    def normal(shape, scale=1.0):
        return jax.random.normal(next(ks), shape, f32) * scale

    def gain(shape):
        return 1.0 + normal(shape, 0.02)

    def dense(fan_in, fan_out):
        return normal((DEPTH, fan_in, fan_out), fan_in ** -0.5)

    n_pages = PAST_LEN // PAGE_SIZE
    n_used = DEC_BATCH * n_pages
    n_phys = (n_used * 5) // 4
    page_table = jax.random.permutation(next(ks), n_phys)[:n_used].reshape(DEC_BATCH, n_pages).astype(jnp.int32)
    b_f = jnp.broadcast_to(jnp.linspace(3.0, 6.0, M_HEADS, dtype=f32), (DEPTH, M_HEADS)) + normal((DEPTH, M_HEADS), 0.01)
    return {
        'x_prompt': normal((BATCH, SEQ, D_MODEL)),
        'x_sample': normal((DEC_BATCH, DEC_SEQ, D_MODEL)),
        'p_prompt': normal((DEPTH, BATCH, SEQ, PLE_DIM)),
        'p_sample': normal((DEPTH, DEC_BATCH, DEC_SEQ, PLE_DIM)),
        'cache_ckv': normal((DEPTH, n_phys, PAGE_SIZE, KV_LORA)),
        'cache_krope': normal((DEPTH, n_phys, PAGE_SIZE, ROPE_DIM)),
        'state_C': normal((DEPTH, DEC_BATCH, M_HEADS, M_DIM, M_DIM), 0.5),
        'state_n': normal((DEPTH, DEC_BATCH, M_HEADS, M_DIM), 0.5),
        'state_m': normal((DEPTH, DEC_BATCH, M_HEADS), 0.5),
        'page_table': page_table,
        'g_ff1': gain((DEPTH, D_MODEL)),
        'w_ff1_gate': dense(D_MODEL, D_FF),
        'w_ff1_up': dense(D_MODEL, D_FF),
        'w_ff1_down': dense(D_FF, D_MODEL),
        'g_mix': gain((DEPTH, D_MODEL)),
        'w_in': dense(D_MODEL, IN_COLS),
        'g_q': gain((DEPTH, Q_LORA)),
        'w_uq': dense(Q_LORA, MLA_HEADS * (NOPE_DIM + ROPE_DIM)),
        'g_kv': gain((DEPTH, KV_LORA)),
        'w_uk': dense(KV_LORA, MLA_HEADS * NOPE_DIM),
        'w_uv': dense(KV_LORA, MLA_HEADS * V_DIM),
        'b_gate_i': normal((DEPTH, M_HEADS), 0.1),
        'b_gate_f': b_f,
        'g_attn_out': gain((DEPTH, MLA_WIDTH)),
        'g_mlstm_out': gain((DEPTH, M_HEADS, M_DIM)),
        'w_out': dense(MIX_WIDTH, D_MODEL),
        'g_ff2': gain((DEPTH, D_MODEL)),
        'w_ff2_gate': dense(D_MODEL, D_FF),
        'w_ff2_up': dense(D_MODEL, D_FF),
        'w_ff2_down': dense(D_FF, D_MODEL),
        'g_ple': gain((DEPTH, D_MODEL)),
        'w_ple_gate': dense(D_MODEL, D_MODEL),
        'w_ple_proj': dense(PLE_DIM, D_MODEL),
        'g_final': gain((D_MODEL,)),
    }


def reference(x_prompt, x_sample, p_prompt, p_sample, cache_ckv, cache_krope, state_C, state_n, state_m,
              page_table, g_ff1, w_ff1_gate, w_ff1_up, w_ff1_down, g_mix, w_in, g_q, w_uq, g_kv, w_uk, w_uv,
              b_gate_i, b_gate_f, g_attn_out, g_mlstm_out, w_out, g_ff2, w_ff2_gate, w_ff2_up, w_ff2_down,
              g_ple, w_ple_gate, w_ple_proj, g_final):
    f32 = jnp.float32
    pos_p = jnp.arange(SEQ)
    pos_s = PAST_LEN + jnp.arange(DEC_SEQ)
    hp, hs = x_prompt, x_sample
    ckv_p, kr_p, C_p, n_p, m_p = [], [], [], [], []
    ckv_s, kr_s, C_s, n_s, m_s = [], [], [], [], []
    for i in range(DEPTH):
        lw = dict(g_ff1=g_ff1[i], w_ff1_gate=w_ff1_gate[i], w_ff1_up=w_ff1_up[i], w_ff1_down=w_ff1_down[i],
                  g_mix=g_mix[i], w_in=w_in[i], g_q=g_q[i], w_uq=w_uq[i], g_kv=g_kv[i], w_uk=w_uk[i], w_uv=w_uv[i],
                  b_gate_i=b_gate_i[i], b_gate_f=b_gate_f[i], g_attn_out=g_attn_out[i], g_mlstm_out=g_mlstm_out[i],
                  w_out=w_out[i], g_ff2=g_ff2[i], w_ff2_gate=w_ff2_gate[i], w_ff2_up=w_ff2_up[i],
                  w_ff2_down=w_ff2_down[i], g_ple=g_ple[i], w_ple_gate=w_ple_gate[i], w_ple_proj=w_ple_proj[i])
        C0 = jnp.zeros((BATCH, M_HEADS, M_DIM, M_DIM), f32)
        n0 = jnp.zeros((BATCH, M_HEADS, M_DIM), f32)
        m0 = jnp.zeros((BATCH, M_HEADS), f32)
        hp, c1, r1, C1, n1, m1 = trunk_layer(hp, p_prompt[i], pos_p, prompt_attention, C0, n0, m0, lw)
        ckv_p.append(c1); kr_p.append(r1); C_p.append(C1); n_p.append(n1); m_p.append(m1)
        attn_s = functools.partial(sample_attention, cache_c=cache_ckv[i], cache_r=cache_krope[i], page_table=page_table)
        hs, c2, r2, C2, n2, m2 = trunk_layer(hs, p_sample[i], pos_s, attn_s, state_C[i].astype(f32),
                                             state_n[i].astype(f32), state_m[i].astype(f32), lw)
        ckv_s.append(c2); kr_s.append(r2); C_s.append(C2); n_s.append(n2); m_s.append(m2)
    y_prompt = rmsnorm(hp, g_final)
    y_sample = rmsnorm(hs, g_final)
    return (y_prompt, y_sample,
            jnp.stack(ckv_p), jnp.stack(kr_p), jnp.stack(C_p), jnp.stack(n_p), jnp.stack(m_p),
            jnp.stack(ckv_s), jnp.stack(kr_s), jnp.stack(C_s), jnp.stack(n_s), jnp.stack(m_s))
```

```python
import functools
import math

import jax
import jax.numpy as jnp
from jax import lax
from jax.experimental import pallas as pl
from jax.experimental.pallas import tpu as pltpu

F32 = jnp.float32
BF16 = jnp.bfloat16

D_MODEL = 1024
SEQ = 8192
DEC_BATCH = 128
PAST_LEN = 8192
PAGE_SIZE = 128
N_PAGES = PAST_LEN // PAGE_SIZE
MLA_HEADS = 8
Q_LORA = 384
KV_LORA = 256
NOPE_DIM = 64
ROPE_DIM = 32
ROPE_HALF = ROPE_DIM // 2
V_DIM = 64
ROPE_THETA = 10000.0
M_HEADS = 4
M_DIM = 128
CHUNK = 128
MLSTM_WIDTH = M_HEADS * M_DIM
D_FF = 2816
PLE_DIM = 256
EPS = 1e-6

LANES = 128
SUBLANES = 8

HEAD_PAD = LANES
MLA_PAD_WIDTH = MLA_HEADS * HEAD_PAD
GATE_I_LANE = ROPE_DIM
GATE_F_LANE = ROPE_DIM + M_HEADS
QK_SCALE = (NOPE_DIM + ROPE_DIM) ** -0.5 * math.log2(math.e)

VMEM_LIMIT = 56 * 1024 * 1024


def _dot(a, b):
    return jnp.dot(a, b, preferred_element_type=F32)


def _dot_nt(a, b):
    return lax.dot_general(a, b, (((1,), (1,)), ((), ())), preferred_element_type=F32)


def _dot_tn(a, b):
    return lax.dot_general(a, b, (((0,), (0,)), ((), ())), preferred_element_type=F32)


def _rms(x, g, width=None):
    width = x.shape[-1] if width is None else width
    ms = jnp.sum(x * x, axis=-1, keepdims=True) * (1.0 / width)
    return x * lax.rsqrt(ms + EPS) * g


def _resident(shape):
    return pl.BlockSpec(shape, lambda *_: (0,) * len(shape), pipeline_mode=pl.Buffered(1))


def _params(n_axes):
    return pltpu.CompilerParams(dimension_semantics=("arbitrary",) * n_axes,
                                vmem_limit_bytes=VMEM_LIMIT)


FF_SPLIT = 2
FF_CHUNK = D_FF // FF_SPLIT


def _swiglu_half(xn, wg_ref, wu_ref, wd_ref):
    out = None
    for c in range(FF_SPLIT):
        cols = slice(c * FF_CHUNK, (c + 1) * FF_CHUNK)
        gate = _dot(xn, wg_ref[:, cols])
        up = _dot(xn, wu_ref[:, cols])
        act = (jax.nn.silu(gate) * up).astype(BF16)
        part = _dot(act, wd_ref[cols, :])
        out = part if out is None else out + part
    return 0.5 * out


def _ffn_body(x_ref, g_ref, wg_ref, wu_ref, wd_ref, o_ref):
    x = x_ref[...]
    xn = _rms(x, g_ref[...]).astype(BF16)
    o_ref[...] = x + _swiglu_half(xn, wg_ref, wu_ref, wd_ref)


def _ffn_call(x, g, wg, wu, wd, tm):
    t = x.shape[0]
    row = lambda i: (i, 0)
    return pl.pallas_call(
        _ffn_body,
        grid=(t // tm,),
        in_specs=[pl.BlockSpec((tm, D_MODEL), row), _resident((1, D_MODEL)),
                  _resident((D_MODEL, D_FF)), _resident((D_MODEL, D_FF)), _resident((D_FF, D_MODEL))],
        out_specs=pl.BlockSpec((tm, D_MODEL), row),
        out_shape=jax.ShapeDtypeStruct((t, D_MODEL), F32),
        compiler_params=_params(1),
        name="ffn1",
    )(x, g, wg, wu, wd)


def _rope128(x, tab):
    return (x * tab[:, 0:LANES]
            + pltpu.roll(x, LANES - ROPE_HALF, 1) * tab[:, LANES:2 * LANES]
            + pltpu.roll(x, ROPE_HALF, 1) * tab[:, 2 * LANES:3 * LANES])


def _proj_body(h_ref, gmix_ref, wq_ref, wkv_ref, wsm_ref, wm_ref, gq_ref, wuq_ref, gkv_ref,
               wuk_ref, wuv_ref, place_ref, bias_ref, tq_ref, tk_ref,
               q_out, k_out, v_out, ckv_out, kr_out, mq_out, mk_out, mv_out, mo_out, gate_out):
    u = _rms(h_ref[...], gmix_ref[...]).astype(BF16)
    ckv = _rms(_dot(u, wkv_ref[...]), gkv_ref[...])
    ckv_out[...] = ckv
    ckv_b = ckv.astype(BF16)
    zs = _dot(u, wsm_ref[...])
    kr = _rope128(zs, tk_ref[...])
    kr_out[...] = kr[:, :ROPE_DIM]
    k_out[...] = (_dot(ckv_b, wuk_ref[...]) + _dot(kr.astype(BF16), place_ref[...])).astype(BF16)
    v_out[...] = _dot(ckv_b, wuv_ref[...]).astype(BF16)
    qn = _rms(_dot(u, wq_ref[...]), gq_ref[...]).astype(BF16)
    q = _dot(qn, wuq_ref[...])
    tq = tq_ref[...]
    for hd in range(MLA_HEADS):
        lanes = slice(hd * HEAD_PAD, (hd + 1) * HEAD_PAD)
        q_out[:, lanes] = _rope128(q[:, lanes], tq).astype(BF16)
    zm = _dot(u, wm_ref[...])
    mq_out[...] = zm[:, 0:MLSTM_WIDTH].astype(BF16)
    mk_out[...] = (zm[:, MLSTM_WIDTH:2 * MLSTM_WIDTH] * (M_DIM ** -0.5)).astype(BF16)
    mv_out[...] = zm[:, 2 * MLSTM_WIDTH:3 * MLSTM_WIDTH].astype(BF16)
    mo_out[...] = zm[:, 3 * MLSTM_WIDTH:4 * MLSTM_WIDTH]
    zb = zs + bias_ref[...]
    lane = lax.broadcasted_iota(jnp.int32, zb.shape, 1)
    log_sig = jnp.minimum(zb, 0.0) - jnp.log1p(jnp.exp(-jnp.abs(zb)))
    is_f = (lane >= GATE_F_LANE) & (lane < GATE_F_LANE + M_HEADS)
    gate_out[...] = jnp.where(is_f, log_sig, zb)


def _proj_call(h, w, tq_tab, tk_tab, tm):
    t = h.shape[0]
    n_tab = tq_tab.shape[0] // tm
    row = lambda i: (i, 0)
    tab = lambda i: (i % n_tab, 0)
    widths_bf16 = (MLA_PAD_WIDTH, MLA_PAD_WIDTH, MLA_PAD_WIDTH)
    out_shape = (
        [jax.ShapeDtypeStruct((t, wd), BF16) for wd in widths_bf16]
        + [jax.ShapeDtypeStruct((t, KV_LORA), F32), jax.ShapeDtypeStruct((t, ROPE_DIM), F32)]
        + [jax.ShapeDtypeStruct((t, MLSTM_WIDTH), BF16)] * 3
        + [jax.ShapeDtypeStruct((t, MLSTM_WIDTH), F32), jax.ShapeDtypeStruct((t, LANES), F32)])
    out_specs = [pl.BlockSpec((tm, s.shape[1]), row) for s in out_shape]
    return pl.pallas_call(
        _proj_body,
        grid=(t // tm,),
        in_specs=[pl.BlockSpec((tm, D_MODEL), row), _resident((1, D_MODEL)),
                  _resident((D_MODEL, Q_LORA)), _resident((D_MODEL, KV_LORA)),
                  _resident((D_MODEL, LANES)), _resident((D_MODEL, 4 * MLSTM_WIDTH)),
                  _resident((1, Q_LORA)), _resident((Q_LORA, MLA_PAD_WIDTH)),
                  _resident((1, KV_LORA)), _resident((KV_LORA, MLA_PAD_WIDTH)),
                  _resident((KV_LORA, MLA_PAD_WIDTH)), _resident((LANES, MLA_PAD_WIDTH)),
                  _resident((1, LANES)),
                  pl.BlockSpec((tm, 3 * LANES), tab), pl.BlockSpec((tm, 3 * LANES), tab)],
        out_specs=out_specs,
        out_shape=out_shape,
        compiler_params=_params(1),
        name="proj",
    )(h, w["g_mix"], w["w_q"], w["w_kv"], w["w_sm"], w["w_m"], w["g_q"], w["w_uq"], w["g_kv"],
      w["w_uk"], w["w_uv"], w["place"], w["gate_bias"], tq_tab, tk_tab)


ATT_BLOCK = 512


def _flash_body(q_ref, k_ref, v_ref, o_ref):
    qi = pl.program_id(2)
    q = q_ref[0]
    blk = ATT_BLOCK

    def step(j, carry, diagonal):
        m, l, acc = carry
        start = pl.multiple_of(j * blk, blk)
        k = k_ref[0, pl.ds(start, blk), :]
        v = v_ref[0, pl.ds(start, blk), :]
        s = _dot_nt(q, k)
        if diagonal:
            r = lax.broadcasted_iota(jnp.int32, s.shape, 0)
            c = lax.broadcasted_iota(jnp.int32, s.shape, 1)
            s = jnp.where(c <= r, s, -jnp.inf)
        m_new = jnp.maximum(m, jnp.max(s, axis=1, keepdims=True))
        alpha = jnp.exp2(m - m_new)
        p = jnp.exp2(s - m_new)
        l = alpha * l + jnp.sum(p, axis=1, keepdims=True)
        acc = alpha * acc + _dot(p.astype(BF16), v)
        return m_new, l, acc

    init = (jnp.full((blk, 1), -jnp.inf, F32), jnp.zeros((blk, 1), F32),
            jnp.zeros((blk, HEAD_PAD), F32))
    carry = lax.fori_loop(0, qi, lambda j, c: step(j, c, False), init)
    _, l, acc = step(qi, carry, True)
    o_ref[0] = (acc / l).astype(BF16)


def _flash_call(q, k, v):
    b, s, _ = q.shape
    qmap = lambda bi, hi, qi: (bi, qi, hi)
    kvmap = lambda bi, hi, qi: (bi, 0, hi)
    return pl.pallas_call(
        _flash_body,
        grid=(b, MLA_HEADS, s // ATT_BLOCK),
        in_specs=[pl.BlockSpec((1, ATT_BLOCK, HEAD_PAD), qmap),
                  pl.BlockSpec((1, s, HEAD_PAD), kvmap),
                  pl.BlockSpec((1, s, HEAD_PAD), kvmap)],
        out_specs=pl.BlockSpec((1, ATT_BLOCK, HEAD_PAD), qmap),
        out_shape=jax.ShapeDtypeStruct((b, s, MLA_PAD_WIDTH), BF16),
        compiler_params=_params(3),
        name="prompt_attn",
    )(q, k, v)


def _mlstm_body(q_ref, k_ref, v_ref, o_ref, g_ref, gm_ref,
                hm_out, c_out, n_out, m_out, c_s, n_s, m_s):
    ci = pl.program_id(1)

    @pl.when(ci == 0)
    def _():
        c_s[...] = jnp.zeros_like(c_s)
        n_s[...] = jnp.zeros_like(n_s)
        m_s[...] = jnp.zeros_like(m_s)

    g = g_ref[0]
    g_t = g.T
    L = CHUNK
    t_idx = lax.broadcasted_iota(jnp.int32, (L, L), 0)
    s_idx = lax.broadcasted_iota(jnp.int32, (L, L), 1)
    causal = s_idx <= t_idx
    for hd in range(M_HEADS):
        lanes = slice(hd * M_DIM, (hd + 1) * M_DIM)
        i_col = g[:, GATE_I_LANE + hd:GATE_I_LANE + hd + 1]
        f_col = g[:, GATE_F_LANE + hd:GATE_F_LANE + hd + 1]
        i_row = g_t[GATE_I_LANE + hd:GATE_I_LANE + hd + 1, :]
        f_row = g_t[GATE_F_LANE + hd:GATE_F_LANE + hd + 1, :]
        b_col = jnp.sum(jnp.where(causal, f_row, 0.0), axis=1, keepdims=True)
        b_row = jnp.sum(jnp.where(t_idx <= s_idx, f_col, 0.0), axis=0, keepdims=True)
        a_row = i_row - b_row
        a_col = i_col - b_col
        m_prev = m_s[hd]
        m_prev_col = jnp.broadcast_to(m_prev, (L, L))[:, 0:1]
        run_max = jnp.max(jnp.where(causal, a_row, -jnp.inf), axis=1, keepdims=True)
        big_m = jnp.maximum(m_prev_col, run_max)
        decay = jnp.where(causal, jnp.exp(a_row - big_m), 0.0)
        w_inter = jnp.exp(m_prev_col - big_m)
        q = q_ref[0, :, lanes]
        k = k_ref[0, :, lanes]
        v = v_ref[0, :, lanes]
        kf = k.astype(F32)
        vf = v.astype(F32)
        c_prev = c_s[hd]
        n_prev = n_s[hd]
        sqk = _dot_nt(q, k) * decay
        num = _dot(sqk.astype(BF16), v) + _dot(q, c_prev.astype(BF16)) * w_inter
        qn = jnp.sum(q.astype(F32) * n_prev, axis=1, keepdims=True)
        den = jnp.sum(sqk, axis=1, keepdims=True) + qn * w_inter
        den = jnp.maximum(jnp.abs(den), jnp.exp(-(b_col + big_m)))
        hh = num / den
        hh = jax.nn.sigmoid(o_ref[0, :, lanes]) * hh
        hm_out[0, :, lanes] = _rms(hh, gm_ref[:, lanes]).astype(BF16)
        a_all_row = jnp.max(jnp.broadcast_to(a_col, (L, L)), axis=0, keepdims=True)
        a_all_col = jnp.max(jnp.broadcast_to(a_row, (L, L)), axis=1, keepdims=True)
        m_last_row = jnp.maximum(m_prev, a_all_row)
        m_last_col = jnp.maximum(m_prev_col, a_all_col)
        b_last_row = jnp.sum(jnp.broadcast_to(f_col, (L, L)), axis=0, keepdims=True)
        keep = jnp.exp(m_prev - m_last_row)
        w_src = jnp.exp(a_col - m_last_col)
        c_new = keep * c_prev + _dot(kf.T.astype(BF16), (w_src * vf).astype(BF16))
        n_new = keep * n_prev + jnp.sum(w_src * kf, axis=0, keepdims=True)
        m_new = b_last_row + m_last_row
        c_s[hd] = c_new
        n_s[hd] = n_new
        m_s[hd] = m_new
        c_out[0, hd] = c_new
        n_out[0, hd] = n_new
        m_out[0, hd] = m_new


def _mlstm_call(mq, mk, mv, mo, gates, gm):
    b, s, _ = mq.shape
    nc = s // CHUNK
    tok = lambda bi, ci: (bi, ci, 0)
    st4 = lambda bi, ci: (bi, 0, 0, 0)
    return pl.pallas_call(
        _mlstm_body,
        grid=(b, nc),
        in_specs=[pl.BlockSpec((1, CHUNK, MLSTM_WIDTH), tok)] * 4
                 + [pl.BlockSpec((1, CHUNK, LANES), tok), _resident((1, MLSTM_WIDTH))],
        out_specs=[pl.BlockSpec((1, CHUNK, MLSTM_WIDTH), tok),
                   pl.BlockSpec((1, M_HEADS, M_DIM, M_DIM), st4),
                   pl.BlockSpec((1, M_HEADS, 1, M_DIM), st4),
                   pl.BlockSpec((1, M_HEADS, 1, LANES), st4)],
        out_shape=[jax.ShapeDtypeStruct((b, s, MLSTM_WIDTH), BF16),
                   jax.ShapeDtypeStruct((b, M_HEADS, M_DIM, M_DIM), F32),
                   jax.ShapeDtypeStruct((b, M_HEADS, 1, M_DIM), F32),
                   jax.ShapeDtypeStruct((b, M_HEADS, 1, LANES), F32)],
        scratch_shapes=[pltpu.VMEM((M_HEADS, M_DIM, M_DIM), F32),
                        pltpu.VMEM((M_HEADS, 1, M_DIM), F32),
                        pltpu.VMEM((M_HEADS, 1, LANES), F32)],
        compiler_params=_params(2),
        name="prompt_mlstm",
    )(mq, mk, mv, mo, gates, gm)


def _merge_body(h_ref, a_ref, hm_ref, p_ref, ga_ref, woa_ref, wom_ref, gff_ref, wg_ref, wu_ref,
                wd_ref, gple_ref, wpg_ref, wpp_ref, gfin_ref, y_ref):
    a = a_ref[...].astype(F32)
    an = _rms(a, ga_ref[...], width=MLA_HEADS * V_DIM).astype(BF16)
    h = h_ref[...] + _dot(an, woa_ref[...]) + _dot(hm_ref[...], wom_ref[...])
    h = h + _swiglu_half(_rms(h, gff_ref[...]).astype(BF16), wg_ref, wu_ref, wd_ref)
    gate = jax.nn.sigmoid(_dot(_rms(h, gple_ref[...]).astype(BF16), wpg_ref[...]))
    h = h + gate * _dot(p_ref[...].astype(BF16), wpp_ref[...])
    y_ref[...] = _rms(h, gfin_ref[...])


def _merge_call(h, a, hm, p, w, tm):
    t = h.shape[0]
    row = lambda i: (i, 0)
    return pl.pallas_call(
        _merge_body,
        grid=(t // tm,),
        in_specs=[pl.BlockSpec((tm, D_MODEL), row), pl.BlockSpec((tm, MLA_PAD_WIDTH), row),
                  pl.BlockSpec((tm, MLSTM_WIDTH), row), pl.BlockSpec((tm, PLE_DIM), row),
                  _resident((1, MLA_PAD_WIDTH)), _resident((MLA_PAD_WIDTH, D_MODEL)),
                  _resident((MLSTM_WIDTH, D_MODEL)), _resident((1, D_MODEL)),
                  _resident((D_MODEL, D_FF)), _resident((D_MODEL, D_FF)), _resident((D_FF, D_MODEL)),
                  _resident((1, D_MODEL)), _resident((D_MODEL, D_MODEL)),
                  _resident((PLE_DIM, D_MODEL)), _resident((1, D_MODEL))],
        out_specs=pl.BlockSpec((tm, D_MODEL), row),
        out_shape=jax.ShapeDtypeStruct((t, D_MODEL), F32),
        compiler_params=_params(1),
        name="merge",
    )(h, a, hm, p, w["g_attn"], w["w_out_a"], w["w_out_m"], w["g_ff2"], w["w_ff2_gate"],
      w["w_ff2_up"], w["w_ff2_down"], w["g_ple"], w["w_ple_gate"], w["w_ple_proj"], w["g_final"])


PAGES_PER_CHUNK = 16
N_CHUNKS = N_PAGES // PAGES_PER_CHUNK
CHUNK_KEYS = PAGES_PER_CHUNK * PAGE_SIZE
N_SLOTS = 2


def _sample_attn_body(pt_ref, q_ref, ckv_ref, kr_ref, cache_c, cache_r, wabs_ref, wuv_ref,
                      o_ref, cbuf, rbuf, sem_c, sem_r):
    b = pl.program_id(0)
    nb = pl.num_programs(0)

    def copies(bi, chunk, slot):
        out = []
        for j in range(PAGES_PER_CHUNK):
            page = pt_ref[bi, chunk * PAGES_PER_CHUNK + j]
            out.append(pltpu.make_async_copy(cache_c.at[page], cbuf.at[slot, j], sem_c.at[slot]))
            out.append(pltpu.make_async_copy(cache_r.at[page], rbuf.at[slot, j], sem_r.at[slot]))
        return out

    def start(bi, chunk, slot):
        for cp in copies(bi, chunk, slot):
            cp.start()

    @pl.when(b == 0)
    def _():
        start(0, 0, 0)

    q_row = q_ref[0]
    sub = lax.broadcasted_iota(jnp.int32, (MLA_HEADS, MLA_PAD_WIDTH), 0)
    lane = lax.broadcasted_iota(jnp.int32, (MLA_HEADS, MLA_PAD_WIDTH), 1)
    own = (lane // HEAD_PAD) == sub
    q_bd = jnp.where(own, jnp.broadcast_to(q_row.astype(F32), own.shape), 0.0).astype(BF16)
    q_ext = _dot(q_bd, wabs_ref[...])
    q_abs = q_ext[:, :KV_LORA].astype(BF16)
    q_rope = q_ext[:, KV_LORA:KV_LORA + ROPE_DIM].astype(BF16)

    c_new = ckv_ref[0]
    r_new = kr_ref[0]
    m = (jnp.sum(q_abs.astype(F32) * c_new.astype(BF16).astype(F32), axis=1, keepdims=True)
         + jnp.sum(q_rope.astype(F32) * r_new.astype(BF16).astype(F32), axis=1, keepdims=True))
    l = jnp.ones((MLA_HEADS, 1), F32)
    acc = jnp.broadcast_to(c_new.astype(BF16).astype(F32), (MLA_HEADS, KV_LORA))

    for chunk in range(N_CHUNKS):
        slot = chunk % N_SLOTS
        nxt = (chunk + 1) % N_SLOTS
        if chunk + 1 < N_CHUNKS:
            start(b, chunk + 1, nxt)
        else:
            @pl.when(b + 1 < nb)
            def _():
                start(b + 1, 0, nxt)
        for cp in copies(b, chunk, slot):
            cp.wait()
        kc = cbuf[slot].reshape(CHUNK_KEYS, KV_LORA).astype(BF16)
        kr = rbuf[slot].reshape(CHUNK_KEYS, ROPE_DIM).astype(BF16)
        s = _dot_nt(q_abs, kc) + _dot_nt(q_rope, kr)
        m_new = jnp.maximum(m, jnp.max(s, axis=1, keepdims=True))
        alpha = jnp.exp2(m - m_new)
        p = jnp.exp2(s - m_new)
        l = alpha * l + jnp.sum(p, axis=1, keepdims=True)
        acc = alpha * acc + _dot(p.astype(BF16), kc)
        m = m_new

    o_lat = (acc / l).astype(BF16)
    res = _dot(o_lat, wuv_ref[...])
    o_ref[0] = jnp.sum(jnp.where(own, res, 0.0), axis=0, keepdims=True).astype(BF16)


def _sample_attn_call(page_table, q, ckv, kr, cache_c, cache_r, w_abs, w_uv):
    nb = q.shape[0]
    assert N_CHUNKS % N_SLOTS == 0
    tok = lambda bi, pt: (bi, 0, 0)
    whole = lambda shape: pl.BlockSpec(shape, lambda bi, pt: (0,) * len(shape),
                                       pipeline_mode=pl.Buffered(1))
    grid_spec = pltpu.PrefetchScalarGridSpec(
        num_scalar_prefetch=1,
        grid=(nb,),
        in_specs=[pl.BlockSpec((1, 1, MLA_PAD_WIDTH), tok), pl.BlockSpec((1, 1, KV_LORA), tok),
                  pl.BlockSpec((1, 1, ROPE_DIM), tok),
                  pl.BlockSpec(memory_space=pl.ANY), pl.BlockSpec(memory_space=pl.ANY),
                  whole((MLA_PAD_WIDTH, KV_LORA + LANES)), whole((KV_LORA, MLA_PAD_WIDTH))],
        out_specs=pl.BlockSpec((1, 1, MLA_PAD_WIDTH), tok),
        scratch_shapes=[pltpu.VMEM((N_SLOTS, PAGES_PER_CHUNK, PAGE_SIZE, KV_LORA), F32),
                        pltpu.VMEM((N_SLOTS, PAGES_PER_CHUNK, PAGE_SIZE, ROPE_DIM), F32),
                        pltpu.SemaphoreType.DMA((N_SLOTS,)), pltpu.SemaphoreType.DMA((N_SLOTS,))])
    return pl.pallas_call(
        _sample_attn_body,
        grid_spec=grid_spec,
        out_shape=jax.ShapeDtypeStruct((nb, 1, MLA_PAD_WIDTH), BF16),
        compiler_params=_params(1),
        name="sample_attn",
    )(page_table, q, ckv, kr, cache_c, cache_r, w_abs, w_uv)


SAMPLE_ROWS = 32


def _sample_mlstm_body(q_ref, k_ref, v_ref, o_ref, g_ref, gm_ref, c_ref, n_ref, m_ref,
                       hm_out, c_out, n_out, m_out):
    hd = pl.program_id(1)
    g = g_ref[...]
    lane = lax.broadcasted_iota(jnp.int32, g.shape, 1)
    ig = jnp.sum(jnp.where(lane == GATE_I_LANE + hd, g, 0.0), axis=1, keepdims=True)
    lf = jnp.sum(jnp.where(lane == GATE_F_LANE + hd, g, 0.0), axis=1, keepdims=True)
    m_all = m_ref[...]
    head_lane = lax.broadcasted_iota(jnp.int32, m_all.shape, 1)
    m0 = jnp.sum(jnp.where(head_lane == hd, m_all, 0.0), axis=1, keepdims=True)
    q = q_ref[...].astype(F32)
    k = k_ref[...].astype(F32)
    v = v_ref[...].astype(F32)
    n0 = n_ref[...]
    m_new = jnp.maximum(lf + m0, ig)
    keep = jnp.exp(lf + m0 - m_new)
    w_src = jnp.exp(ig - m_new)
    sqk = jnp.sum(q * k, axis=1, keepdims=True) * w_src
    wk = w_src * k
    qc = jnp.zeros_like(v)
    for d in range(M_DIM):
        c_d = c_ref[:, 0, d, :]
        qc = qc + q[:, d:d + 1] * c_d
        c_out[:, 0, d, :] = keep * c_d + wk[:, d:d + 1] * v
    num = sqk * v + qc * keep
    den = sqk + jnp.sum(q * n0, axis=1, keepdims=True) * keep
    den = jnp.maximum(jnp.abs(den), jnp.exp(-m_new))
    hh = jax.nn.sigmoid(o_ref[...]) * (num / den)
    hm_out[...] = _rms(hh, gm_ref[...]).astype(BF16)
    n_out[...] = keep * n0 + wk

    @pl.when(hd == 0)
    def _():
        m_out[...] = jnp.broadcast_to(m_new, m_all.shape)

    @pl.when(hd > 0)
    def _():
        m_out[...] = jnp.where(head_lane == hd, m_new, m_out[...])


def _sample_mlstm_call(mq, mk, mv, mo, gates, gm, c0, n0, m0):
    nb = mq.shape[0]
    rows = SAMPLE_ROWS
    tok = lambda bi, hi: (bi, hi)
    return pl.pallas_call(
        _sample_mlstm_body,
        grid=(nb // rows, M_HEADS),
        in_specs=[pl.BlockSpec((rows, M_DIM), tok)] * 4
                 + [pl.BlockSpec((rows, LANES), lambda bi, hi: (bi, 0)),
                    pl.BlockSpec((1, M_DIM), lambda bi, hi: (0, hi)),
                    pl.BlockSpec((rows, 1, M_DIM, M_DIM), lambda bi, hi: (bi, hi, 0, 0)),
                    pl.BlockSpec((rows, M_DIM), tok),
                    pl.BlockSpec((rows, M_HEADS), lambda bi, hi: (bi, 0))],
        out_specs=[pl.BlockSpec((rows, M_DIM), tok),
                   pl.BlockSpec((rows, 1, M_DIM, M_DIM), lambda bi, hi: (bi, hi, 0, 0)),
                   pl.BlockSpec((rows, M_DIM), tok),
                   pl.BlockSpec((rows, M_HEADS), lambda bi, hi: (bi, 0))],
        out_shape=[jax.ShapeDtypeStruct((nb, MLSTM_WIDTH), BF16),
                   jax.ShapeDtypeStruct((nb, M_HEADS, M_DIM, M_DIM), F32),
                   jax.ShapeDtypeStruct((nb, MLSTM_WIDTH), F32),
                   jax.ShapeDtypeStruct((nb, M_HEADS), F32)],
        compiler_params=_params(2),
        name="sample_mlstm",
    )(mq, mk, mv, mo, gates, gm, c0, n0, m0)


def _pad_heads(w, head_dim):
    rows = w.shape[0]
    w = w.reshape(rows, MLA_HEADS, head_dim)
    w = jnp.pad(w, ((0, 0), (0, 0), (0, HEAD_PAD - head_dim)))
    return w.reshape(rows, MLA_PAD_WIDTH)


def _rope_tables(pos):
    inv = ROPE_THETA ** (-jnp.arange(ROPE_HALF, dtype=F32) / ROPE_HALF)
    ang = pos.astype(F32)[:, None] * inv[None, :]
    cos, sin = jnp.cos(ang), jnp.sin(ang)
    n = pos.shape[0]
    z = lambda wd: jnp.zeros((n, wd), F32)
    tab_k = jnp.concatenate([cos, cos, z(LANES - ROPE_DIM),
                             -sin, z(LANES - ROPE_HALF),
                             z(ROPE_HALF), sin, z(LANES - ROPE_DIM)], axis=1)
    pad = LANES - NOPE_DIM - ROPE_DIM
    tab_q = jnp.concatenate([jnp.ones((n, NOPE_DIM), F32), cos, cos, z(pad),
                             z(NOPE_DIM), -sin, z(ROPE_HALF + pad),
                             z(NOPE_DIM + ROPE_HALF), sin, z(pad)], axis=1) * QK_SCALE
    return tab_q, tab_k


def _prep_weights(g_ff1, w_ff1_gate, w_ff1_up, w_ff1_down, g_mix, w_in, g_q, w_uq, g_kv, w_uk,
                  w_uv, b_gate_i, b_gate_f, g_attn_out, g_mlstm_out, w_out, g_ff2, w_ff2_gate,
                  w_ff2_up, w_ff2_down, g_ple, w_ple_gate, w_ple_proj, g_final):
    bf = lambda a: a.astype(BF16)
    row = lambda a: a.reshape(1, -1).astype(F32)
    off_kv, off_kr = Q_LORA, Q_LORA + KV_LORA
    off_m = off_kr + ROPE_DIM
    off_i = off_m + 4 * MLSTM_WIDTH
    small_pad = LANES - ROPE_DIM - 2 * M_HEADS
    w_sm = jnp.concatenate([w_in[:, off_kr:off_m], w_in[:, off_i:off_i + 2 * M_HEADS],
                            jnp.zeros((D_MODEL, small_pad), F32)], axis=1)
    gate_bias = jnp.concatenate([jnp.zeros((ROPE_DIM,), F32), b_gate_i, b_gate_f,
                                 jnp.zeros((small_pad,), F32)])
    src = jnp.arange(LANES)[:, None]
    dst = jnp.arange(MLA_PAD_WIDTH)[None, :]
    place = ((dst % HEAD_PAD) == (src + NOPE_DIM)) & (src < ROPE_DIM)
    w_abs = jnp.pad(w_uk.reshape(KV_LORA, MLA_HEADS, NOPE_DIM).transpose(1, 2, 0),
                    ((0, 0), (0, HEAD_PAD - NOPE_DIM), (0, 0))).reshape(MLA_PAD_WIDTH, KV_LORA)
    rope_sel = (dst.T % HEAD_PAD) == (src.T + NOPE_DIM)
    rope_sel = rope_sel & (src.T < ROPE_DIM)
    w_abs = jnp.concatenate([w_abs, rope_sel.astype(F32)], axis=1)
    w_out_a = jnp.pad(w_out[:MLA_HEADS * V_DIM].reshape(MLA_HEADS, V_DIM, D_MODEL),
                      ((0, 0), (0, HEAD_PAD - V_DIM), (0, 0))).reshape(MLA_PAD_WIDTH, D_MODEL)
    return dict(
        g_ff1=row(g_ff1), w_ff1_gate=bf(w_ff1_gate), w_ff1_up=bf(w_ff1_up), w_ff1_down=bf(w_ff1_down),
        g_mix=row(g_mix), w_q=bf(w_in[:, :off_kv]), w_kv=bf(w_in[:, off_kv:off_kr]), w_sm=bf(w_sm),
        w_m=bf(w_in[:, off_m:off_i]), g_q=row(g_q), w_uq=bf(_pad_heads(w_uq, NOPE_DIM + ROPE_DIM)),
        g_kv=row(g_kv), w_uk=bf(_pad_heads(w_uk, NOPE_DIM)), w_uv=bf(_pad_heads(w_uv, V_DIM)),
        place=place.astype(BF16), gate_bias=row(gate_bias), w_abs=bf(w_abs),
        g_attn=row(_pad_heads(g_attn_out.reshape(1, -1), V_DIM)), w_out_a=bf(w_out_a),
        w_out_m=bf(w_out[MLA_HEADS * V_DIM:]), g_mlstm=row(g_mlstm_out),
        g_ff2=row(g_ff2), w_ff2_gate=bf(w_ff2_gate), w_ff2_up=bf(w_ff2_up), w_ff2_down=bf(w_ff2_down),
        g_ple=row(g_ple), w_ple_gate=bf(w_ple_gate), w_ple_proj=bf(w_ple_proj), g_final=row(g_final))


PROMPT_TILE = 512


def kernel(x_prompt, x_sample, p_prompt, p_sample, cache_ckv, cache_krope, state_C, state_n, state_m, page_table, g_ff1, w_ff1_gate, w_ff1_up, w_ff1_down, g_mix, w_in, g_q, w_uq, g_kv, w_uk, w_uv, b_gate_i, b_gate_f, g_attn_out, g_mlstm_out, w_out, g_ff2, w_ff2_gate, w_ff2_up, w_ff2_down, g_ple, w_ple_gate, w_ple_proj, g_final):
    assert w_in.shape[0] == 1, "single-layer trunk"
    w = _prep_weights(g_ff1[0], w_ff1_gate[0], w_ff1_up[0], w_ff1_down[0], g_mix[0], w_in[0],
                      g_q[0], w_uq[0], g_kv[0], w_uk[0], w_uv[0], b_gate_i[0], b_gate_f[0],
                      g_attn_out[0], g_mlstm_out[0], w_out[0], g_ff2[0], w_ff2_gate[0],
                      w_ff2_up[0], w_ff2_down[0], g_ple[0], w_ple_gate[0], w_ple_proj[0], g_final)
    nb_p, seq, _ = x_prompt.shape
    nb_s = x_sample.shape[0]
    t_p = nb_p * seq

    tq_p, tk_p = _rope_tables(jnp.arange(seq))
    h_p = _ffn_call(x_prompt.reshape(t_p, D_MODEL), w["g_ff1"], w["w_ff1_gate"], w["w_ff1_up"],
                    w["w_ff1_down"], PROMPT_TILE)
    q_p, k_p, v_p, ckv_p, kr_p, mq_p, mk_p, mv_p, mo_p, gates_p = _proj_call(
        h_p, w, tq_p, tk_p, PROMPT_TILE)
    seq3 = lambda a: a.reshape(nb_p, seq, a.shape[-1])
    a_p = _flash_call(seq3(q_p), seq3(k_p), seq3(v_p))
    hm_p, c_p, n_p, m_p = _mlstm_call(seq3(mq_p), seq3(mk_p), seq3(mv_p), seq3(mo_p),
                                      seq3(gates_p), w["g_mlstm"])
    y_p = _merge_call(h_p, a_p.reshape(t_p, MLA_PAD_WIDTH), hm_p.reshape(t_p, MLSTM_WIDTH),
                      p_prompt.reshape(t_p, PLE_DIM), w, PROMPT_TILE)

    tq_s, tk_s = _rope_tables(jnp.full((nb_s,), PAST_LEN, jnp.int32))
    h_s = _ffn_call(x_sample.reshape(nb_s, D_MODEL), w["g_ff1"], w["w_ff1_gate"], w["w_ff1_up"],
                    w["w_ff1_down"], nb_s)
    q_s, _, _, ckv_s, kr_s, mq_s, mk_s, mv_s, mo_s, gates_s = _proj_call(h_s, w, tq_s, tk_s, nb_s)
    n_phys = cache_ckv.shape[1]
    a_s = _sample_attn_call(
        page_table, q_s.reshape(nb_s, 1, MLA_PAD_WIDTH), ckv_s.reshape(nb_s, 1, KV_LORA),
        kr_s.reshape(nb_s, 1, ROPE_DIM), cache_ckv.reshape(n_phys, PAGE_SIZE, KV_LORA),
        cache_krope.reshape(n_phys, PAGE_SIZE, ROPE_DIM), w["w_abs"], w["w_uv"])
    hm_s, c_s, n_s, m_s = _sample_mlstm_call(
        mq_s, mk_s, mv_s, mo_s, gates_s, w["g_mlstm"], state_C[0].astype(F32),
        state_n[0].astype(F32).reshape(nb_s, MLSTM_WIDTH), state_m[0].astype(F32))
    y_s = _merge_call(h_s, a_s.reshape(nb_s, MLA_PAD_WIDTH), hm_s, p_sample.reshape(nb_s, PLE_DIM),
                      w, nb_s)

    return (y_p.reshape(nb_p, seq, D_MODEL), y_s.reshape(nb_s, 1, D_MODEL),
            ckv_p.reshape(1, nb_p, seq, KV_LORA), kr_p.reshape(1, nb_p, seq, ROPE_DIM),
            c_p[None], n_p.reshape(1, nb_p, M_HEADS, M_DIM), m_p[:, :, 0, 0][None],
            ckv_s.reshape(1, nb_s, 1, KV_LORA), kr_s.reshape(1, nb_s, 1, ROPE_DIM),
            c_s[None], n_s.reshape(1, nb_s, M_HEADS, M_DIM), m_s[None])
```

```python
import functools
import math

import jax
import jax.numpy as jnp
from jax import lax
from jax.experimental import pallas as pl
from jax.experimental.pallas import tpu as pltpu

F32 = jnp.float32
BF16 = jnp.bfloat16

D_MODEL = 1024
SEQ = 8192
DEC_BATCH = 128
PAST_LEN = 8192
PAGE_SIZE = 128
N_PAGES = PAST_LEN // PAGE_SIZE
MLA_HEADS = 8
Q_LORA = 384
KV_LORA = 256
NOPE_DIM = 64
ROPE_DIM = 32
ROPE_HALF = ROPE_DIM // 2
V_DIM = 64
ROPE_THETA = 10000.0
M_HEADS = 4
M_DIM = 128
CHUNK = 128
MLSTM_WIDTH = M_HEADS * M_DIM
D_FF = 2816
PLE_DIM = 256
EPS = 1e-6

LANES = 128
SUBLANES = 8

HEAD_PAD = LANES
MLA_PAD_WIDTH = MLA_HEADS * HEAD_PAD
GATE_I_LANE = ROPE_DIM
GATE_F_LANE = ROPE_DIM + M_HEADS
QK_SCALE = (NOPE_DIM + ROPE_DIM) ** -0.5 * math.log2(math.e)

VMEM_LIMIT = 56 * 1024 * 1024


def _dot(a, b):
    return jnp.dot(a, b, preferred_element_type=F32)


def _dot_nt(a, b):
    return lax.dot_general(a, b, (((1,), (1,)), ((), ())), preferred_element_type=F32)


def _dot_tn(a, b):
    return lax.dot_general(a, b, (((0,), (0,)), ((), ())), preferred_element_type=F32)


def _rms(x, g, width=None):
    width = x.shape[-1] if width is None else width
    ms = jnp.sum(x * x, axis=-1, keepdims=True) * (1.0 / width)
    return x * lax.rsqrt(ms + EPS) * g


def _resident(shape):
    return pl.BlockSpec(shape, lambda *_: (0,) * len(shape), pipeline_mode=pl.Buffered(1))


def _params(n_axes):
    return pltpu.CompilerParams(dimension_semantics=("arbitrary",) * n_axes,
                                vmem_limit_bytes=VMEM_LIMIT)


FF_SPLIT = 2
FF_CHUNK = D_FF // FF_SPLIT


def _swiglu_half(xn, wg_ref, wu_ref, wd_ref):
    out = None
    for c in range(FF_SPLIT):
        cols = slice(c * FF_CHUNK, (c + 1) * FF_CHUNK)
        gate = _dot(xn, wg_ref[:, cols])
        up = _dot(xn, wu_ref[:, cols])
        act = (jax.nn.silu(gate) * up).astype(BF16)
        part = _dot(act, wd_ref[cols, :])
        out = part if out is None else out + part
    return 0.5 * out


def _ffn_body(x_ref, g_ref, wg_ref, wu_ref, wd_ref, o_ref):
    x = x_ref[...]
    xn = _rms(x, g_ref[...]).astype(BF16)
    o_ref[...] = x + _swiglu_half(xn, wg_ref, wu_ref, wd_ref)


def _ffn_call(x, g, wg, wu, wd, tm):
    t = x.shape[0]
    row = lambda i: (i, 0)
    return pl.pallas_call(
        _ffn_body,
        grid=(t // tm,),
        in_specs=[pl.BlockSpec((tm, D_MODEL), row), _resident((1, D_MODEL)),
                  _resident((D_MODEL, D_FF)), _resident((D_MODEL, D_FF)), _resident((D_FF, D_MODEL))],
        out_specs=pl.BlockSpec((tm, D_MODEL), row),
        out_shape=jax.ShapeDtypeStruct((t, D_MODEL), F32),
        compiler_params=_params(1),
        name="ffn1",
    )(x, g, wg, wu, wd)


def _rope128(x, tab):
    return (x * tab[:, 0:LANES]
            + pltpu.roll(x, LANES - ROPE_HALF, 1) * tab[:, LANES:2 * LANES]
            + pltpu.roll(x, ROPE_HALF, 1) * tab[:, 2 * LANES:3 * LANES])


def _proj_body(h_ref, gmix_ref, wq_ref, wkv_ref, wsm_ref, wm_ref, gq_ref, wuq_ref, gkv_ref,
               wuk_ref, wuvt_ref, place_ref, bias_ref, tq_ref, tk_ref,
               q_out, k_out, vt_out, ckv_out, kr_out, mq_out, mk_out, mv_out, mo_out, gate_out):
    u = _rms(h_ref[...], gmix_ref[...]).astype(BF16)
    ckv = _rms(_dot(u, wkv_ref[...]), gkv_ref[...])
    ckv_out[...] = ckv
    ckv_b = ckv.astype(BF16)
    zs = _dot(u, wsm_ref[...])
    kr = _rope128(zs, tk_ref[...])
    kr_out[...] = kr[:, :ROPE_DIM]
    k_out[...] = (_dot(ckv_b, wuk_ref[...]) + _dot(kr.astype(BF16), place_ref[...])).astype(BF16)
    vt_out[0, 0] = _dot_nt(wuvt_ref[...], ckv_b).astype(BF16)
    qn = _rms(_dot(u, wq_ref[...]), gq_ref[...]).astype(BF16)
    q = _dot(qn, wuq_ref[...])
    tq = tq_ref[...]
    for hd in range(MLA_HEADS):
        lanes = slice(hd * HEAD_PAD, (hd + 1) * HEAD_PAD)
        q_out[:, lanes] = _rope128(q[:, lanes], tq).astype(BF16)
    zm = _dot(u, wm_ref[...])
    mq_out[...] = zm[:, 0:MLSTM_WIDTH].astype(BF16)
    mk_out[...] = (zm[:, MLSTM_WIDTH:2 * MLSTM_WIDTH] * (M_DIM ** -0.5)).astype(BF16)
    mv_out[...] = zm[:, 2 * MLSTM_WIDTH:3 * MLSTM_WIDTH].astype(BF16)
    mo_out[...] = zm[:, 3 * MLSTM_WIDTH:4 * MLSTM_WIDTH]
    zb = zs + bias_ref[...]
    lane = lax.broadcasted_iota(jnp.int32, zb.shape, 1)
    log_sig = jnp.minimum(zb, 0.0) - jnp.log1p(jnp.exp(-jnp.abs(zb)))
    is_f = (lane >= GATE_F_LANE) & (lane < GATE_F_LANE + M_HEADS)
    gate_out[...] = jnp.where(is_f, log_sig, zb)


def _proj_call(h, w, tq_tab, tk_tab, tm):
    t = h.shape[0]
    n_tab = tq_tab.shape[0] // tm
    row = lambda i: (i, 0)
    tab = lambda i: (i % n_tab, 0)
    v_width = MLA_HEADS * V_DIM
    tok_outs = ([jax.ShapeDtypeStruct((t, MLA_PAD_WIDTH), BF16)] * 2
                + [jax.ShapeDtypeStruct((t, KV_LORA), F32), jax.ShapeDtypeStruct((t, ROPE_DIM), F32)]
                + [jax.ShapeDtypeStruct((t, MLSTM_WIDTH), BF16)] * 3
                + [jax.ShapeDtypeStruct((t, MLSTM_WIDTH), F32), jax.ShapeDtypeStruct((t, LANES), F32)])
    tok_specs = [pl.BlockSpec((tm, s.shape[1]), row) for s in tok_outs]
    vt_shape = jax.ShapeDtypeStruct((t // tm // n_tab, n_tab, v_width, tm), BF16)
    vt_spec = pl.BlockSpec((1, 1, v_width, tm), lambda i: (i // n_tab, i % n_tab, 0, 0))
    out_specs = tok_specs[:2] + [vt_spec] + tok_specs[2:]
    out_shape = tok_outs[:2] + [vt_shape] + tok_outs[2:]
    return pl.pallas_call(
        _proj_body,
        grid=(t // tm,),
        in_specs=[pl.BlockSpec((tm, D_MODEL), row), _resident((1, D_MODEL)),
                  _resident((D_MODEL, Q_LORA)), _resident((D_MODEL, KV_LORA)),
                  _resident((D_MODEL, LANES)), _resident((D_MODEL, 4 * MLSTM_WIDTH)),
                  _resident((1, Q_LORA)), _resident((Q_LORA, MLA_PAD_WIDTH)),
                  _resident((1, KV_LORA)), _resident((KV_LORA, MLA_PAD_WIDTH)),
                  _resident((v_width, KV_LORA)), _resident((LANES, MLA_PAD_WIDTH)),
                  _resident((1, LANES)),
                  pl.BlockSpec((tm, 3 * LANES), tab), pl.BlockSpec((tm, 3 * LANES), tab)],
        out_specs=out_specs,
        out_shape=out_shape,
        compiler_params=_params(1),
        name="proj",
    )(h, w["g_mix"], w["w_q"], w["w_kv"], w["w_sm"], w["w_m"], w["g_q"], w["w_uq"], w["g_kv"],
      w["w_uk"], w["w_uv_t"], w["place"], w["gate_bias"], tq_tab, tk_tab)


ATT_BLOCK = 512
ATT_HEADS = 8
ATT_LANES = ATT_HEADS * HEAD_PAD
ATT_V_ROWS = ATT_HEADS * V_DIM


def _flash_body(q_ref, k_ref, vt_ref, o_ref):
    qi = pl.program_id(2)
    blk = ATT_BLOCK
    head_lanes = [slice(hd * HEAD_PAD, (hd + 1) * HEAD_PAD) for hd in range(ATT_HEADS)]
    qs = [q_ref[0, :, lanes] for lanes in head_lanes]

    def step(j, carry, diagonal):
        start = pl.multiple_of(j * blk, blk)
        scores = [_dot_nt(k_ref[0, pl.ds(start, blk), head_lanes[hd]], qs[hd])
                  for hd in range(ATT_HEADS)]
        if diagonal:
            key = lax.broadcasted_iota(jnp.int32, (blk, blk), 0)
            qry = lax.broadcasted_iota(jnp.int32, (blk, blk), 1)
            visible = key <= qry
        out = []
        for hd in range(ATT_HEADS):
            m, l, acc = carry[hd]
            s = jnp.where(visible, scores[hd], -jnp.inf) if diagonal else scores[hd]
            m_new = jnp.maximum(m, jnp.max(s, axis=0, keepdims=True))
            alpha = jnp.exp2(m - m_new)
            p = jnp.exp2(s - m_new)
            l = alpha * l + jnp.sum(p, axis=0, keepdims=True)
            vt = vt_ref[0, j, hd * V_DIM:(hd + 1) * V_DIM, :]
            acc = alpha * acc + _dot(vt, p.astype(BF16))
            out.append((m_new, l, acc))
        return tuple(out)

    init = tuple((jnp.full((1, blk), -jnp.inf, F32), jnp.zeros((1, blk), F32),
                  jnp.zeros((V_DIM, blk), F32)) for _ in range(ATT_HEADS))
    carry = lax.fori_loop(0, qi, lambda j, c: step(j, c, False), init)
    final = step(qi, carry, True)
    for hd in range(ATT_HEADS):
        _, l, acc = final[hd]
        o_t = jnp.concatenate([acc / l, jnp.zeros((HEAD_PAD - V_DIM, blk), F32)], axis=0)
        o_ref[0, :, head_lanes[hd]] = o_t.T.astype(BF16)


def _flash_call(q, k, vt):
    b, s, _ = q.shape
    assert vt.shape == (b, s // ATT_BLOCK, MLA_HEADS * V_DIM, ATT_BLOCK)
    qmap = lambda bi, gi, qi: (bi, qi, gi)
    return pl.pallas_call(
        _flash_body,
        grid=(b, MLA_HEADS // ATT_HEADS, s // ATT_BLOCK),
        in_specs=[pl.BlockSpec((1, ATT_BLOCK, ATT_LANES), qmap),
                  pl.BlockSpec((1, s, ATT_LANES), lambda bi, gi, qi: (bi, 0, gi),
                               pipeline_mode=pl.Buffered(1)),
                  pl.BlockSpec((1, s // ATT_BLOCK, ATT_V_ROWS, ATT_BLOCK),
                               lambda bi, gi, qi: (bi, 0, gi, 0), pipeline_mode=pl.Buffered(1))],
        out_specs=pl.BlockSpec((1, ATT_BLOCK, ATT_LANES), qmap),
        out_shape=jax.ShapeDtypeStruct((b, s, MLA_PAD_WIDTH), BF16),
        compiler_params=_params(3),
        name="prompt_attn",
    )(q, k, vt)


def _mlstm_body(q_ref, k_ref, v_ref, o_ref, g_ref, gm_ref,
                hm_out, c_out, n_out, m_out, c_s, n_s, m_s):
    ci = pl.program_id(1)

    @pl.when(ci == 0)
    def _():
        c_s[...] = jnp.zeros_like(c_s)
        n_s[...] = jnp.zeros_like(n_s)
        m_s[...] = jnp.zeros_like(m_s)

    g = g_ref[0]
    g_t = g.T
    L = CHUNK
    t_idx = lax.broadcasted_iota(jnp.int32, (L, L), 0)
    s_idx = lax.broadcasted_iota(jnp.int32, (L, L), 1)
    causal = s_idx <= t_idx
    heads = range(M_HEADS)
    head_lanes = [slice(hd * M_DIM, (hd + 1) * M_DIM) for hd in heads]

    loc = []
    for hd in heads:
        i_col = g[:, GATE_I_LANE + hd:GATE_I_LANE + hd + 1]
        f_col = g[:, GATE_F_LANE + hd:GATE_F_LANE + hd + 1]
        i_row = g_t[GATE_I_LANE + hd:GATE_I_LANE + hd + 1, :]
        f_row = g_t[GATE_F_LANE + hd:GATE_F_LANE + hd + 1, :]
        b_col = jnp.sum(jnp.where(causal, f_row, 0.0), axis=1, keepdims=True)
        b_row = jnp.sum(jnp.where(t_idx <= s_idx, f_col, 0.0), axis=0, keepdims=True)
        a_row = i_row - b_row
        a_col = i_col - b_col
        run_max = jnp.max(jnp.where(causal, a_row, -jnp.inf), axis=1, keepdims=True)
        top_row = jnp.max(jnp.broadcast_to(a_col, (L, L)), axis=0, keepdims=True)
        top_col = jnp.broadcast_to(top_row, (L, L))[:, 0:1]
        b_last = jnp.sum(jnp.broadcast_to(f_col, (L, L)), axis=0, keepdims=True)
        q = q_ref[0, :, head_lanes[hd]]
        k = k_ref[0, :, head_lanes[hd]]
        s0 = _dot_nt(q, k) * jnp.where(causal, jnp.exp(a_row - run_max), 0.0)
        loc.append(dict(b_col=b_col, run_max=run_max, top_row=top_row, b_last=b_last, q=q, k=k,
                        s0=s0, src=jnp.exp(a_col - top_col)))
    for hd in heads:
        d = loc[hd]
        v = v_ref[0, :, head_lanes[hd]]
        kf = d["k"].astype(F32)
        d["u"] = _dot(d["s0"].astype(BF16), v)
        d["row_sum"] = jnp.sum(d["s0"], axis=1, keepdims=True)
        d["kv"] = _dot(kf.T.astype(BF16), (d["src"] * v.astype(F32)).astype(BF16))
        d["k_sum"] = jnp.sum(d["src"] * kf, axis=0, keepdims=True)

    for hd in heads:
        d = loc[hd]
        m_prev = m_s[hd]
        m_prev_col = jnp.broadcast_to(m_prev, (L, L))[:, 0:1]
        c_prev = c_s[hd]
        n_prev = n_s[hd]
        big_m = jnp.maximum(m_prev_col, d["run_max"])
        w_intra = jnp.exp(d["run_max"] - big_m)
        w_inter = jnp.exp(m_prev_col - big_m)
        num = d["u"] * w_intra + _dot(d["q"], c_prev.astype(BF16)) * w_inter
        qn = jnp.sum(d["q"].astype(F32) * n_prev, axis=1, keepdims=True)
        den = d["row_sum"] * w_intra + qn * w_inter
        den = jnp.maximum(jnp.abs(den), jnp.exp(-(d["b_col"] + big_m)))
        hh = jax.nn.sigmoid(o_ref[0, :, head_lanes[hd]]) * (num / den)
        hm_out[0, :, head_lanes[hd]] = _rms(hh, gm_ref[:, head_lanes[hd]]).astype(BF16)
        m_last = jnp.maximum(m_prev, d["top_row"])
        keep = jnp.exp(m_prev - m_last)
        gain = jnp.exp(d["top_row"] - m_last)
        c_new = keep * c_prev + gain * d["kv"]
        n_new = keep * n_prev + gain * d["k_sum"]
        m_new = d["b_last"] + m_last
        c_s[hd] = c_new
        n_s[hd] = n_new
        m_s[hd] = m_new
        c_out[0, hd] = c_new
        n_out[0, hd] = n_new
        m_out[0, hd] = m_new


def _mlstm_call(mq, mk, mv, mo, gates, gm):
    b, s, _ = mq.shape
    nc = s // CHUNK
    tok = lambda bi, ci: (bi, ci, 0)
    st4 = lambda bi, ci: (bi, 0, 0, 0)
    return pl.pallas_call(
        _mlstm_body,
        grid=(b, nc),
        in_specs=[pl.BlockSpec((1, CHUNK, MLSTM_WIDTH), tok)] * 4
                 + [pl.BlockSpec((1, CHUNK, LANES), tok), _resident((1, MLSTM_WIDTH))],
        out_specs=[pl.BlockSpec((1, CHUNK, MLSTM_WIDTH), tok),
                   pl.BlockSpec((1, M_HEADS, M_DIM, M_DIM), st4),
                   pl.BlockSpec((1, M_HEADS, 1, M_DIM), st4),
                   pl.BlockSpec((1, M_HEADS, 1, LANES), st4)],
        out_shape=[jax.ShapeDtypeStruct((b, s, MLSTM_WIDTH), BF16),
                   jax.ShapeDtypeStruct((b, M_HEADS, M_DIM, M_DIM), F32),
                   jax.ShapeDtypeStruct((b, M_HEADS, 1, M_DIM), F32),
                   jax.ShapeDtypeStruct((b, M_HEADS, 1, LANES), F32)],
        scratch_shapes=[pltpu.VMEM((M_HEADS, M_DIM, M_DIM), F32),
                        pltpu.VMEM((M_HEADS, 1, M_DIM), F32),
                        pltpu.VMEM((M_HEADS, 1, LANES), F32)],
        compiler_params=_params(2),
        name="prompt_mlstm",
    )(mq, mk, mv, mo, gates, gm)


def _merge_body(h_ref, a_ref, hm_ref, p_ref, ga_ref, woa_ref, wom_ref, gff_ref, wg_ref, wu_ref,
                wd_ref, gple_ref, wpg_ref, wpp_ref, gfin_ref, y_ref):
    a = a_ref[...].astype(F32)
    an = _rms(a, ga_ref[...], width=MLA_HEADS * V_DIM).astype(BF16)
    h = h_ref[...] + _dot(an, woa_ref[...]) + _dot(hm_ref[...], wom_ref[...])
    h = h + _swiglu_half(_rms(h, gff_ref[...]).astype(BF16), wg_ref, wu_ref, wd_ref)
    gate = jax.nn.sigmoid(_dot(_rms(h, gple_ref[...]).astype(BF16), wpg_ref[...]))
    h = h + gate * _dot(p_ref[...].astype(BF16), wpp_ref[...])
    y_ref[...] = _rms(h, gfin_ref[...])


def _merge_call(h, a, hm, p, w, tm):
    t = h.shape[0]
    row = lambda i: (i, 0)
    return pl.pallas_call(
        _merge_body,
        grid=(t // tm,),
        in_specs=[pl.BlockSpec((tm, D_MODEL), row), pl.BlockSpec((tm, MLA_PAD_WIDTH), row),
                  pl.BlockSpec((tm, MLSTM_WIDTH), row), pl.BlockSpec((tm, PLE_DIM), row),
                  _resident((1, MLA_PAD_WIDTH)), _resident((MLA_PAD_WIDTH, D_MODEL)),
                  _resident((MLSTM_WIDTH, D_MODEL)), _resident((1, D_MODEL)),
                  _resident((D_MODEL, D_FF)), _resident((D_MODEL, D_FF)), _resident((D_FF, D_MODEL)),
                  _resident((1, D_MODEL)), _resident((D_MODEL, D_MODEL)),
                  _resident((PLE_DIM, D_MODEL)), _resident((1, D_MODEL))],
        out_specs=pl.BlockSpec((tm, D_MODEL), row),
        out_shape=jax.ShapeDtypeStruct((t, D_MODEL), F32),
        compiler_params=_params(1),
        name="merge",
    )(h, a, hm, p, w["g_attn"], w["w_out_a"], w["w_out_m"], w["g_ff2"], w["w_ff2_gate"],
      w["w_ff2_up"], w["w_ff2_down"], w["g_ple"], w["w_ple_gate"], w["w_ple_proj"], w["g_final"])


N_SLOTS = 2
SAMPLE_CHUNKS = 4
SAMPLE_CHUNK_PAGES = N_PAGES // SAMPLE_CHUNKS
SAMPLE_CHUNK_KEYS = SAMPLE_CHUNK_PAGES * PAGE_SIZE


def _sample_attn_body(pt_ref, q_ref, ckv_ref, kr_ref, cache_c, cache_r, wabs_ref, wuv_ref,
                      o_ref, cbuf, rbuf, sem_c, sem_r):
    b = pl.program_id(0)
    nb = pl.num_programs(0)
    slot = b % N_SLOTS

    def copies(bi, sl):
        out = []
        for j in range(N_PAGES):
            page = pt_ref[bi, j]
            out.append(pltpu.make_async_copy(cache_c.at[page], cbuf.at[sl, j], sem_c.at[sl]))
            out.append(pltpu.make_async_copy(
                cache_r.at[page], rbuf.at[sl, :, pl.ds(j * PAGE_SIZE, PAGE_SIZE)], sem_r.at[sl]))
        return out

    @pl.when(b == 0)
    def _():
        for bi in range(min(N_SLOTS, DEC_BATCH)):
            for cp in copies(bi, bi):
                cp.start()

    q_row = q_ref[0]
    sub = lax.broadcasted_iota(jnp.int32, (MLA_HEADS, MLA_PAD_WIDTH), 0)
    lane = lax.broadcasted_iota(jnp.int32, (MLA_HEADS, MLA_PAD_WIDTH), 1)
    own = (lane // HEAD_PAD) == sub
    q_bd = jnp.where(own, jnp.broadcast_to(q_row.astype(F32), own.shape), 0.0).astype(BF16)
    q_ext = _dot(q_bd, wabs_ref[...])
    q_abs = q_ext[:, :KV_LORA].astype(BF16)
    q_rope = q_ext[:, KV_LORA:KV_LORA + ROPE_DIM].astype(BF16)

    c_new = ckv_ref[0].astype(BF16).astype(F32)
    r_new = kr_ref[0].astype(BF16).astype(F32)
    s_new = (jnp.sum(q_abs.astype(F32) * c_new, axis=1, keepdims=True)
             + jnp.sum(q_rope.astype(F32) * r_new, axis=1, keepdims=True))

    for cp in copies(b, slot):
        cp.wait()

    def chunk_keys(i):
        pages = slice(i * SAMPLE_CHUNK_PAGES, (i + 1) * SAMPLE_CHUNK_PAGES)
        keys = slice(i * SAMPLE_CHUNK_KEYS, (i + 1) * SAMPLE_CHUNK_KEYS)
        kc = cbuf[slot, pages].reshape(SAMPLE_CHUNK_KEYS, KV_LORA).astype(BF16)
        return kc, _dot_nt(q_abs, kc) + _dot(q_rope, rbuf[slot, :, keys].astype(BF16))

    m = s_new
    l = jnp.ones_like(s_new)
    acc = jnp.broadcast_to(c_new, (MLA_HEADS, KV_LORA))
    kc, s = chunk_keys(0)
    for i in range(SAMPLE_CHUNKS):
        nxt = chunk_keys(i + 1) if i + 1 < SAMPLE_CHUNKS else None
        m_new = jnp.maximum(m, jnp.max(s, axis=1, keepdims=True))
        alpha = jnp.exp2(m - m_new)
        p = jnp.exp2(s - m_new)
        l = alpha * l + jnp.sum(p, axis=1, keepdims=True)
        acc = alpha * acc + _dot(p.astype(BF16), kc)
        m = m_new
        if nxt is not None:
            kc, s = nxt
    o_lat = (acc / l).astype(BF16)
    res = _dot(o_lat, wuv_ref[...])
    o_ref[0] = jnp.sum(jnp.where(own, res, 0.0), axis=0, keepdims=True).astype(BF16)

    @pl.when(b + N_SLOTS < nb)
    def _():
        for cp in copies(b + N_SLOTS, slot):
            cp.start()


def _sample_attn_call(page_table, q, ckv, kr, cache_c, cache_r, w_abs, w_uv):
    nb = q.shape[0]
    tok = lambda bi, pt: (bi, 0, 0)
    whole = lambda shape: pl.BlockSpec(shape, lambda bi, pt: (0,) * len(shape),
                                       pipeline_mode=pl.Buffered(1))
    grid_spec = pltpu.PrefetchScalarGridSpec(
        num_scalar_prefetch=1,
        grid=(nb,),
        in_specs=[pl.BlockSpec((1, 1, MLA_PAD_WIDTH), tok), pl.BlockSpec((1, 1, KV_LORA), tok),
                  pl.BlockSpec((1, 1, ROPE_DIM), tok),
                  pl.BlockSpec(memory_space=pl.ANY), pl.BlockSpec(memory_space=pl.ANY),
                  whole((MLA_PAD_WIDTH, KV_LORA + LANES)), whole((KV_LORA, MLA_PAD_WIDTH))],
        out_specs=pl.BlockSpec((1, 1, MLA_PAD_WIDTH), tok),
        scratch_shapes=[pltpu.VMEM((N_SLOTS, N_PAGES, PAGE_SIZE, KV_LORA), F32),
                        pltpu.VMEM((N_SLOTS, ROPE_DIM, PAST_LEN), F32),
                        pltpu.SemaphoreType.DMA((N_SLOTS,)), pltpu.SemaphoreType.DMA((N_SLOTS,))])
    return pl.pallas_call(
        _sample_attn_body,
        grid_spec=grid_spec,
        out_shape=jax.ShapeDtypeStruct((nb, 1, MLA_PAD_WIDTH), BF16),
        compiler_params=_params(1),
        name="sample_attn",
    )(page_table, q, ckv, kr, cache_c, cache_r, w_abs, w_uv)


SAMPLE_ROWS = 32


def _sample_mlstm_body(q_ref, k_ref, v_ref, o_ref, g_ref, gm_ref, c_ref, n_ref, m_ref,
                       hm_out, c_out, n_out, m_out):
    hd = pl.program_id(1)
    g = g_ref[...]
    lane = lax.broadcasted_iota(jnp.int32, g.shape, 1)
    ig = jnp.sum(jnp.where(lane == GATE_I_LANE + hd, g, 0.0), axis=1, keepdims=True)
    lf = jnp.sum(jnp.where(lane == GATE_F_LANE + hd, g, 0.0), axis=1, keepdims=True)
    m_all = m_ref[...]
    head_lane = lax.broadcasted_iota(jnp.int32, m_all.shape, 1)
    m0 = jnp.sum(jnp.where(head_lane == hd, m_all, 0.0), axis=1, keepdims=True)
    q = q_ref[...].astype(F32)
    k = k_ref[...].astype(F32)
    v = v_ref[...].astype(F32)
    n0 = n_ref[...]
    m_new = jnp.maximum(lf + m0, ig)
    keep = jnp.exp(lf + m0 - m_new)
    w_src = jnp.exp(ig - m_new)
    sqk = jnp.sum(q * k, axis=1, keepdims=True) * w_src
    wk = w_src * k
    qc = jnp.zeros_like(v)
    for d in range(M_DIM):
        c_d = c_ref[:, 0, d, :]
        qc = qc + q[:, d:d + 1] * c_d
        c_out[:, 0, d, :] = keep * c_d + wk[:, d:d + 1] * v
    num = sqk * v + qc * keep
    den = sqk + jnp.sum(q * n0, axis=1, keepdims=True) * keep
    den = jnp.maximum(jnp.abs(den), jnp.exp(-m_new))
    hh = jax.nn.sigmoid(o_ref[...]) * (num / den)
    hm_out[...] = _rms(hh, gm_ref[...]).astype(BF16)
    n_out[...] = keep * n0 + wk

    @pl.when(hd == 0)
    def _():
        m_out[...] = jnp.broadcast_to(m_new, m_all.shape)

    @pl.when(hd > 0)
    def _():
        m_out[...] = jnp.where(head_lane == hd, m_new, m_out[...])


def _sample_mlstm_call(mq, mk, mv, mo, gates, gm, c0, n0, m0):
    nb = mq.shape[0]
    rows = SAMPLE_ROWS
    tok = lambda bi, hi: (bi, hi)
    return pl.pallas_call(
        _sample_mlstm_body,
        grid=(nb // rows, M_HEADS),
        in_specs=[pl.BlockSpec((rows, M_DIM), tok)] * 4
                 + [pl.BlockSpec((rows, LANES), lambda bi, hi: (bi, 0)),
                    pl.BlockSpec((1, M_DIM), lambda bi, hi: (0, hi)),
                    pl.BlockSpec((rows, 1, M_DIM, M_DIM), lambda bi, hi: (bi, hi, 0, 0)),
                    pl.BlockSpec((rows, M_DIM), tok),
                    pl.BlockSpec((rows, M_HEADS), lambda bi, hi: (bi, 0))],
        out_specs=[pl.BlockSpec((rows, M_DIM), tok),
                   pl.BlockSpec((rows, 1, M_DIM, M_DIM), lambda bi, hi: (bi, hi, 0, 0)),
                   pl.BlockSpec((rows, M_DIM), tok),
                   pl.BlockSpec((rows, M_HEADS), lambda bi, hi: (bi, 0))],
        out_shape=[jax.ShapeDtypeStruct((nb, MLSTM_WIDTH), BF16),
                   jax.ShapeDtypeStruct((nb, M_HEADS, M_DIM, M_DIM), F32),
                   jax.ShapeDtypeStruct((nb, MLSTM_WIDTH), F32),
                   jax.ShapeDtypeStruct((nb, M_HEADS), F32)],
        compiler_params=_params(2),
        name="sample_mlstm",
    )(mq, mk, mv, mo, gates, gm, c0, n0, m0)


def _pad_heads(w, head_dim):
    rows = w.shape[0]
    w = w.reshape(rows, MLA_HEADS, head_dim)
    w = jnp.pad(w, ((0, 0), (0, 0), (0, HEAD_PAD - head_dim)))
    return w.reshape(rows, MLA_PAD_WIDTH)


def _rope_tables(pos):
    inv = ROPE_THETA ** (-jnp.arange(ROPE_HALF, dtype=F32) / ROPE_HALF)
    ang = pos.astype(F32)[:, None] * inv[None, :]
    cos, sin = jnp.cos(ang), jnp.sin(ang)
    n = pos.shape[0]
    z = lambda wd: jnp.zeros((n, wd), F32)
    tab_k = jnp.concatenate([cos, cos, z(LANES - ROPE_DIM),
                             -sin, z(LANES - ROPE_HALF),
                             z(ROPE_HALF), sin, z(LANES - ROPE_DIM)], axis=1)
    pad = LANES - NOPE_DIM - ROPE_DIM
    tab_q = jnp.concatenate([jnp.ones((n, NOPE_DIM), F32), cos, cos, z(pad),
                             z(NOPE_DIM), -sin, z(ROPE_HALF + pad),
                             z(NOPE_DIM + ROPE_HALF), sin, z(pad)], axis=1) * QK_SCALE
    return tab_q, tab_k


def _prep_weights(g_ff1, w_ff1_gate, w_ff1_up, w_ff1_down, g_mix, w_in, g_q, w_uq, g_kv, w_uk,
                  w_uv, b_gate_i, b_gate_f, g_attn_out, g_mlstm_out, w_out, g_ff2, w_ff2_gate,
                  w_ff2_up, w_ff2_down, g_ple, w_ple_gate, w_ple_proj, g_final):
    bf = lambda a: a.astype(BF16)
    row = lambda a: a.reshape(1, -1).astype(F32)
    off_kv, off_kr = Q_LORA, Q_LORA + KV_LORA
    off_m = off_kr + ROPE_DIM
    off_i = off_m + 4 * MLSTM_WIDTH
    small_pad = LANES - ROPE_DIM - 2 * M_HEADS
    w_sm = jnp.concatenate([w_in[:, off_kr:off_m], w_in[:, off_i:off_i + 2 * M_HEADS],
                            jnp.zeros((D_MODEL, small_pad), F32)], axis=1)
    gate_bias = jnp.concatenate([jnp.zeros((ROPE_DIM,), F32), b_gate_i, b_gate_f,
                                 jnp.zeros((small_pad,), F32)])
    src = jnp.arange(LANES)[:, None]
    dst = jnp.arange(MLA_PAD_WIDTH)[None, :]
    place = ((dst % HEAD_PAD) == (src + NOPE_DIM)) & (src < ROPE_DIM)
    w_abs = jnp.pad(w_uk.reshape(KV_LORA, MLA_HEADS, NOPE_DIM).transpose(1, 2, 0),
                    ((0, 0), (0, HEAD_PAD - NOPE_DIM), (0, 0))).reshape(MLA_PAD_WIDTH, KV_LORA)
    rope_sel = (dst.T % HEAD_PAD) == (src.T + NOPE_DIM)
    rope_sel = rope_sel & (src.T < ROPE_DIM)
    w_abs = jnp.concatenate([w_abs, rope_sel.astype(F32)], axis=1)
    w_out_a = jnp.pad(w_out[:MLA_HEADS * V_DIM].reshape(MLA_HEADS, V_DIM, D_MODEL),
                      ((0, 0), (0, HEAD_PAD - V_DIM), (0, 0))).reshape(MLA_PAD_WIDTH, D_MODEL)
    return dict(
        g_ff1=row(g_ff1), w_ff1_gate=bf(w_ff1_gate), w_ff1_up=bf(w_ff1_up), w_ff1_down=bf(w_ff1_down),
        g_mix=row(g_mix), w_q=bf(w_in[:, :off_kv]), w_kv=bf(w_in[:, off_kv:off_kr]), w_sm=bf(w_sm),
        w_m=bf(w_in[:, off_m:off_i]), g_q=row(g_q), w_uq=bf(_pad_heads(w_uq, NOPE_DIM + ROPE_DIM)),
        g_kv=row(g_kv), w_uk=bf(_pad_heads(w_uk, NOPE_DIM)), w_uv=bf(_pad_heads(w_uv, V_DIM)),
        w_uv_t=bf(w_uv.T),
        place=place.astype(BF16), gate_bias=row(gate_bias), w_abs=bf(w_abs),
        g_attn=row(_pad_heads(g_attn_out.reshape(1, -1), V_DIM)), w_out_a=bf(w_out_a),
        w_out_m=bf(w_out[MLA_HEADS * V_DIM:]), g_mlstm=row(g_mlstm_out),
        g_ff2=row(g_ff2), w_ff2_gate=bf(w_ff2_gate), w_ff2_up=bf(w_ff2_up), w_ff2_down=bf(w_ff2_down),
        g_ple=row(g_ple), w_ple_gate=bf(w_ple_gate), w_ple_proj=bf(w_ple_proj), g_final=row(g_final))


PROMPT_TILE = 512


def kernel(x_prompt, x_sample, p_prompt, p_sample, cache_ckv, cache_krope, state_C, state_n, state_m, page_table, g_ff1, w_ff1_gate, w_ff1_up, w_ff1_down, g_mix, w_in, g_q, w_uq, g_kv, w_uk, w_uv, b_gate_i, b_gate_f, g_attn_out, g_mlstm_out, w_out, g_ff2, w_ff2_gate, w_ff2_up, w_ff2_down, g_ple, w_ple_gate, w_ple_proj, g_final):
    assert w_in.shape[0] == 1, "single-layer trunk"
    w = _prep_weights(g_ff1[0], w_ff1_gate[0], w_ff1_up[0], w_ff1_down[0], g_mix[0], w_in[0],
                      g_q[0], w_uq[0], g_kv[0], w_uk[0], w_uv[0], b_gate_i[0], b_gate_f[0],
                      g_attn_out[0], g_mlstm_out[0], w_out[0], g_ff2[0], w_ff2_gate[0],
                      w_ff2_up[0], w_ff2_down[0], g_ple[0], w_ple_gate[0], w_ple_proj[0], g_final)
    nb_p, seq, _ = x_prompt.shape
    nb_s = x_sample.shape[0]
    t_p = nb_p * seq

    tq_p, tk_p = _rope_tables(jnp.arange(seq))
    h_p = _ffn_call(x_prompt.reshape(t_p, D_MODEL), w["g_ff1"], w["w_ff1_gate"], w["w_ff1_up"],
                    w["w_ff1_down"], PROMPT_TILE)
    q_p, k_p, vt_p, ckv_p, kr_p, mq_p, mk_p, mv_p, mo_p, gates_p = _proj_call(
        h_p, w, tq_p, tk_p, PROMPT_TILE)
    seq3 = lambda a: a.reshape(nb_p, seq, a.shape[-1])
    a_p = _flash_call(seq3(q_p), seq3(k_p), vt_p)
    hm_p, c_p, n_p, m_p = _mlstm_call(seq3(mq_p), seq3(mk_p), seq3(mv_p), seq3(mo_p),
                                      seq3(gates_p), w["g_mlstm"])
    y_p = _merge_call(h_p, a_p.reshape(t_p, MLA_PAD_WIDTH), hm_p.reshape(t_p, MLSTM_WIDTH),
                      p_prompt.reshape(t_p, PLE_DIM), w, PROMPT_TILE)

    tq_s, tk_s = _rope_tables(jnp.full((nb_s,), PAST_LEN, jnp.int32))
    h_s = _ffn_call(x_sample.reshape(nb_s, D_MODEL), w["g_ff1"], w["w_ff1_gate"], w["w_ff1_up"],
                    w["w_ff1_down"], nb_s)
    q_s, _, _, ckv_s, kr_s, mq_s, mk_s, mv_s, mo_s, gates_s = _proj_call(h_s, w, tq_s, tk_s, nb_s)
    n_phys = cache_ckv.shape[1]
    a_s = _sample_attn_call(
        page_table, q_s.reshape(nb_s, 1, MLA_PAD_WIDTH), ckv_s.reshape(nb_s, 1, KV_LORA),
        kr_s.reshape(nb_s, 1, ROPE_DIM), cache_ckv.reshape(n_phys, PAGE_SIZE, KV_LORA),
        jnp.swapaxes(cache_krope.reshape(n_phys, PAGE_SIZE, ROPE_DIM), 1, 2), w["w_abs"], w["w_uv"])
    hm_s, c_s, n_s, m_s = _sample_mlstm_call(
        mq_s, mk_s, mv_s, mo_s, gates_s, w["g_mlstm"], state_C[0].astype(F32),
        state_n[0].astype(F32).reshape(nb_s, MLSTM_WIDTH), state_m[0].astype(F32))
    y_s = _merge_call(h_s, a_s.reshape(nb_s, MLA_PAD_WIDTH), hm_s, p_sample.reshape(nb_s, PLE_DIM),
                      w, nb_s)

    return (y_p.reshape(nb_p, seq, D_MODEL), y_s.reshape(nb_s, 1, D_MODEL),
            ckv_p.reshape(1, nb_p, seq, KV_LORA), kr_p.reshape(1, nb_p, seq, ROPE_DIM),
            c_p[None], n_p.reshape(1, nb_p, M_HEADS, M_DIM), m_p[:, :, 0, 0][None],
            ckv_s.reshape(1, nb_s, 1, KV_LORA), kr_s.reshape(1, nb_s, 1, ROPE_DIM),
            c_s[None], n_s.reshape(1, nb_s, M_HEADS, M_DIM), m_s[None])
```

```python
import functools
import math

import jax
import jax.numpy as jnp
from jax import lax
from jax.experimental import pallas as pl
from jax.experimental.pallas import tpu as pltpu

F32 = jnp.float32
BF16 = jnp.bfloat16

D_MODEL = 1024
SEQ = 8192
DEC_BATCH = 128
PAST_LEN = 8192
PAGE_SIZE = 128
N_PAGES = PAST_LEN // PAGE_SIZE
MLA_HEADS = 8
Q_LORA = 384
KV_LORA = 256
NOPE_DIM = 64
ROPE_DIM = 32
ROPE_HALF = ROPE_DIM // 2
V_DIM = 64
ROPE_THETA = 10000.0
M_HEADS = 4
M_DIM = 128
CHUNK = 128
MLSTM_WIDTH = M_HEADS * M_DIM
D_FF = 2816
PLE_DIM = 256
EPS = 1e-6

LANES = 128
SUBLANES = 8

HEAD_PAD = LANES
MLA_PAD_WIDTH = MLA_HEADS * HEAD_PAD
MLA_WIDTH = MLA_HEADS * V_DIM
V_ROWS = 80
GATE_I_LANE = ROPE_DIM
GATE_F_LANE = ROPE_DIM + M_HEADS
QK_SCALE = (NOPE_DIM + ROPE_DIM) ** -0.5 * math.log2(math.e)

VMEM_LIMIT = 56 * 1024 * 1024


def _dot(a, b):
    return jnp.dot(a, b, preferred_element_type=F32)


def _dot_nt(a, b):
    return lax.dot_general(a, b, (((1,), (1,)), ((), ())), preferred_element_type=F32)


def _dot_tn(a, b):
    return lax.dot_general(a, b, (((0,), (0,)), ((), ())), preferred_element_type=F32)


def _rms(x, g):
    ms = jnp.sum(x * x, axis=-1, keepdims=True) * (1.0 / x.shape[-1])
    return x * lax.rsqrt(ms + EPS) * g


def _resident(shape):
    return pl.BlockSpec(shape, lambda *_: (0,) * len(shape), pipeline_mode=pl.Buffered(1))


def _params(n_axes):
    return pltpu.CompilerParams(dimension_semantics=("arbitrary",) * n_axes,
                                vmem_limit_bytes=VMEM_LIMIT)


FF_SPLIT = 2
FF_CHUNK = D_FF // FF_SPLIT


def _swiglu_half(xn, wg_ref, wu_ref, wd_ref):
    out = None
    for c in range(FF_SPLIT):
        cols = slice(c * FF_CHUNK, (c + 1) * FF_CHUNK)
        gate = _dot(xn, wg_ref[:, cols])
        up = _dot(xn, wu_ref[:, cols])
        act = (jax.nn.silu(gate) * up).astype(BF16)
        part = _dot(act, wd_ref[cols, :])
        out = part if out is None else out + part
    return 0.5 * out


def _ffn_body(x_ref, g_ref, wg_ref, wu_ref, wd_ref, o_ref):
    x = x_ref[...]
    xn = _rms(x, g_ref[...]).astype(BF16)
    o_ref[...] = x + _swiglu_half(xn, wg_ref, wu_ref, wd_ref)


def _ffn_call(x, g, wg, wu, wd, tm):
    t = x.shape[0]
    row = lambda i: (i, 0)
    return pl.pallas_call(
        _ffn_body,
        grid=(t // tm,),
        in_specs=[pl.BlockSpec((tm, D_MODEL), row), _resident((1, D_MODEL)),
                  _resident((D_MODEL, D_FF)), _resident((D_MODEL, D_FF)), _resident((D_FF, D_MODEL))],
        out_specs=pl.BlockSpec((tm, D_MODEL), row),
        out_shape=jax.ShapeDtypeStruct((t, D_MODEL), F32),
        compiler_params=_params(1),
        name="ffn1",
    )(x, g, wg, wu, wd)


def _rope128(x, cos_tab, sin_tab, x2_start):
    lane = lax.broadcasted_iota(jnp.int32, x.shape, 1)
    partner = jnp.where(lane < x2_start, pltpu.roll(x, LANES - ROPE_HALF, 1),
                        pltpu.roll(x, ROPE_HALF, 1))
    return x * cos_tab + partner * sin_tab


def _proj_body(h_ref, gmix_ref, wq_ref, wkv_ref, wsm_ref, wm_ref, gq_ref, wuq_ref, gkv_ref,
               wuk_ref, wuvt_ref, place_ref, bias_ref, tab_ref,
               q_out, k_out, vt_out, ckv_out, kr_out, mq_out, mk_out, mv_out, mo_out, gate_out):
    u = _rms(h_ref[...], gmix_ref[...]).astype(BF16)
    ckv = _rms(_dot(u, wkv_ref[...]), gkv_ref[...])
    ckv_out[...] = ckv
    ckv_b = ckv.astype(BF16)
    zs = _dot(u, wsm_ref[...])
    cos_q = tab_ref[:, 0:LANES]
    sin_q = tab_ref[:, LANES:2 * LANES]
    lane = lax.broadcasted_iota(jnp.int32, zs.shape, 1)
    cos_k = jnp.where(lane < ROPE_DIM, pltpu.roll(cos_q, LANES - NOPE_DIM, 1), 0.0)
    sin_k = pltpu.roll(sin_q, LANES - NOPE_DIM, 1)
    kr = _rope128(zs, cos_k, sin_k, ROPE_HALF)
    kr_out[...] = kr[:, :ROPE_DIM]
    k_out[...] = (_dot(ckv_b, wuk_ref[...]) + _dot(kr.astype(BF16), place_ref[...])).astype(BF16)
    vt = _dot_nt(wuvt_ref[...], ckv_b)
    vrow = lax.broadcasted_iota(jnp.int32, vt.shape, 0)
    vt_out[0, 0] = jnp.where(vrow % V_ROWS == V_DIM, 1.0, vt).astype(BF16)
    qn = _rms(_dot(u, wq_ref[...]), gq_ref[...]).astype(BF16)
    q = _dot(qn, wuq_ref[...])
    cos_qs = cos_q * QK_SCALE
    sin_qs = sin_q * QK_SCALE
    for hd in range(MLA_HEADS):
        lanes = slice(hd * HEAD_PAD, (hd + 1) * HEAD_PAD)
        q_out[:, lanes] = _rope128(q[:, lanes], cos_qs, sin_qs, NOPE_DIM + ROPE_HALF).astype(BF16)
    zm = _dot(u, wm_ref[...])
    mq_out[...] = zm[:, 0:MLSTM_WIDTH].astype(BF16)
    mk_out[...] = (zm[:, MLSTM_WIDTH:2 * MLSTM_WIDTH] * (M_DIM ** -0.5)).astype(BF16)
    mv_out[...] = zm[:, 2 * MLSTM_WIDTH:3 * MLSTM_WIDTH].astype(BF16)
    mo_out[...] = zm[:, 3 * MLSTM_WIDTH:4 * MLSTM_WIDTH]
    zb = zs + bias_ref[...]
    log_sig = jnp.minimum(zb, 0.0) - jnp.log1p(jnp.exp(-jnp.abs(zb)))
    is_f = (lane >= GATE_F_LANE) & (lane < GATE_F_LANE + M_HEADS)
    gate_out[...] = jnp.where(is_f, log_sig, zb)


def _proj_call(h, w, rope_tab, tm):
    t = h.shape[0]
    n_tab = rope_tab.shape[0] // tm
    row = lambda i: (i, 0)
    tab = lambda i: (i % n_tab, 0)
    v_width = MLA_HEADS * V_ROWS
    tok_outs = ([jax.ShapeDtypeStruct((t, MLA_PAD_WIDTH), BF16)] * 2
                + [jax.ShapeDtypeStruct((t, KV_LORA), F32), jax.ShapeDtypeStruct((t, ROPE_DIM), F32)]
                + [jax.ShapeDtypeStruct((t, MLSTM_WIDTH), BF16)] * 3
                + [jax.ShapeDtypeStruct((t, MLSTM_WIDTH), F32), jax.ShapeDtypeStruct((t, LANES), F32)])
    tok_specs = [pl.BlockSpec((tm, s.shape[1]), row) for s in tok_outs]
    vt_shape = jax.ShapeDtypeStruct((t // tm // n_tab, n_tab, v_width, tm), BF16)
    vt_spec = pl.BlockSpec((1, 1, v_width, tm), lambda i: (i // n_tab, i % n_tab, 0, 0))
    out_specs = tok_specs[:2] + [vt_spec] + tok_specs[2:]
    out_shape = tok_outs[:2] + [vt_shape] + tok_outs[2:]
    return pl.pallas_call(
        _proj_body,
        grid=(t // tm,),
        in_specs=[pl.BlockSpec((tm, D_MODEL), row), _resident((1, D_MODEL)),
                  _resident((D_MODEL, Q_LORA)), _resident((D_MODEL, KV_LORA)),
                  _resident((D_MODEL, LANES)), _resident((D_MODEL, 4 * MLSTM_WIDTH)),
                  _resident((1, Q_LORA)), _resident((Q_LORA, MLA_PAD_WIDTH)),
                  _resident((1, KV_LORA)), _resident((KV_LORA, MLA_PAD_WIDTH)),
                  _resident((v_width, KV_LORA)), _resident((LANES, MLA_PAD_WIDTH)),
                  _resident((1, LANES)),
                  pl.BlockSpec((tm, 2 * LANES), tab)],
        out_specs=out_specs,
        out_shape=out_shape,
        compiler_params=_params(1),
        name="proj",
    )(h, w["g_mix"], w["w_q"], w["w_kv"], w["w_sm"], w["w_m"], w["g_q"], w["w_uq"], w["g_kv"],
      w["w_uk"], w["w_uv_t"], w["place"], w["gate_bias"], rope_tab)


ATT_BLOCK = 512
ATT_HEADS = 8
ATT_LANES = ATT_HEADS * HEAD_PAD
ATT_V_ROWS = ATT_HEADS * V_ROWS


def _flash_body(q_ref, k_ref, vt_ref, o_ref):
    qi = pl.program_id(2)
    blk = ATT_BLOCK
    head_lanes = [slice(hd * HEAD_PAD, (hd + 1) * HEAD_PAD) for hd in range(ATT_HEADS)]
    qs = [q_ref[0, :, lanes] for lanes in head_lanes]

    def step(j, carry, diagonal):
        start = pl.multiple_of(j * blk, blk)
        scores = [_dot_nt(k_ref[0, pl.ds(start, blk), head_lanes[hd]], qs[hd])
                  for hd in range(ATT_HEADS)]
        if diagonal:
            key = lax.broadcasted_iota(jnp.int32, (blk, blk), 0)
            qry = lax.broadcasted_iota(jnp.int32, (blk, blk), 1)
            visible = key <= qry
        out = []
        for hd in range(ATT_HEADS):
            m, acc = carry[hd]
            s = jnp.where(visible, scores[hd], -jnp.inf) if diagonal else scores[hd]
            m_new = jnp.maximum(m, jnp.max(s, axis=0, keepdims=True))
            p = jnp.exp2(s - m_new).astype(BF16)
            vt = vt_ref[0, j, hd * V_ROWS:(hd + 1) * V_ROWS, :]
            acc = jnp.exp2(m - m_new) * acc + _dot(vt, p)
            out.append((m_new, acc))
        return tuple(out)

    init = tuple((jnp.full((1, blk), -jnp.inf, F32), jnp.zeros((V_ROWS, blk), F32))
                 for _ in range(ATT_HEADS))
    carry = lax.fori_loop(0, qi, lambda j, c: step(j, c, False), init)
    final = step(qi, carry, True)
    for pair in range(ATT_HEADS // 2):
        o_t = jnp.concatenate([final[hd][1][:V_DIM] / final[hd][1][V_DIM:V_DIM + 1]
                               for hd in (2 * pair, 2 * pair + 1)], axis=0)
        o_ref[0, :, pair * LANES:(pair + 1) * LANES] = o_t.T.astype(BF16)


def _flash_call(q, k, vt):
    b, s, _ = q.shape
    assert vt.shape == (b, s // ATT_BLOCK, MLA_HEADS * V_ROWS, ATT_BLOCK)
    qmap = lambda bi, gi, qi: (bi, qi, gi)
    return pl.pallas_call(
        _flash_body,
        grid=(b, MLA_HEADS // ATT_HEADS, s // ATT_BLOCK),
        in_specs=[pl.BlockSpec((1, ATT_BLOCK, ATT_LANES), qmap),
                  pl.BlockSpec((1, s, ATT_LANES), lambda bi, gi, qi: (bi, 0, gi),
                               pipeline_mode=pl.Buffered(1)),
                  pl.BlockSpec((1, s // ATT_BLOCK, ATT_V_ROWS, ATT_BLOCK),
                               lambda bi, gi, qi: (bi, 0, gi, 0), pipeline_mode=pl.Buffered(1))],
        out_specs=pl.BlockSpec((1, ATT_BLOCK, ATT_HEADS * V_DIM), qmap),
        out_shape=jax.ShapeDtypeStruct((b, s, MLA_WIDTH), BF16),
        compiler_params=_params(3),
        name="prompt_attn",
    )(q, k, vt)


STATE_LANES = 2 * M_DIM


def _lane_scan(x, op, fill):
    lane = lax.broadcasted_iota(jnp.int32, x.shape, 1)
    shift = 1
    while shift < x.shape[1]:
        x = op(x, jnp.where(lane >= shift, pltpu.roll(x, shift, 1), fill))
        shift *= 2
    return x


def _mlstm_body(q_ref, k_ref, v_ref, o_ref, g_ref, gm_ref, hm_out, cx_out, m_out, cx_s, m_s):
    ci = pl.program_id(0)
    n_seq = q_ref.shape[0]
    L = CHUNK

    @pl.when(ci == 0)
    def _():
        cx_s[...] = jnp.zeros_like(cx_s)
        m_s[...] = jnp.zeros_like(m_s)

    t_idx = lax.broadcasted_iota(jnp.int32, (L, L), 0)
    s_idx = lax.broadcasted_iota(jnp.int32, (L, L), 1)
    causal = s_idx <= t_idx
    unit = (lax.broadcasted_iota(jnp.int32, (L, M_DIM), 1) == 0).astype(F32)
    unit_b = unit.astype(BF16)

    streams = []
    for sq in range(n_seq):
        rows = g_ref[sq].T[GATE_I_LANE:GATE_I_LANE + 2 * M_HEADS, :]
        b_rows = pltpu.roll(_lane_scan(rows, jnp.add, 0.0), M_HEADS, 0)
        head_row = lax.broadcasted_iota(jnp.int32, rows.shape, 0) < M_HEADS
        a_rows = jnp.where(head_row, rows - b_rows, 0.0)
        run_max = _lane_scan(a_rows, jnp.maximum, -jnp.inf)
        top = jnp.broadcast_to(jnp.max(a_rows, axis=1, keepdims=True), rows.shape)
        b_last = pltpu.roll(jnp.broadcast_to(jnp.sum(rows, axis=1, keepdims=True), rows.shape),
                            M_HEADS, 0)
        src = jnp.exp(a_rows - top)
        cols = jnp.concatenate([run_max, b_rows, src, jnp.zeros((L - 3 * SUBLANES, L), F32)],
                               axis=0).T
        for hd in range(M_HEADS):
            lanes = slice(hd * M_DIM, (hd + 1) * M_DIM)
            idx = sq * M_HEADS + hd
            q = q_ref[sq, :, lanes]
            k = k_ref[sq, :, lanes]
            v = v_ref[sq, :, lanes]
            m_prev = m_s[idx][0:1, :]
            m_prev_col = jnp.broadcast_to(m_prev, (L, L))[:, 0:1]
            big_m = jnp.maximum(m_prev_col, cols[:, hd:hd + 1])
            decay = jnp.where(causal, jnp.exp(a_rows[hd:hd + 1, :] - big_m), 0.0)
            cx_prev = cx_s[idx]
            src_col = cols[:, 2 * SUBLANES + hd:2 * SUBLANES + hd + 1]
            rhs = (src_col * jnp.concatenate([v.astype(F32), unit], axis=1)).astype(BF16)
            streams.append(dict(
                sq=sq, lanes=lanes, idx=idx, m_prev=m_prev, big_m=big_m, cx_prev=cx_prev,
                w_inter=jnp.exp(m_prev_col - big_m),
                floor=jnp.exp(-(cols[:, SUBLANES + hd:SUBLANES + hd + 1] + big_m)),
                top=top[hd:hd + 1, :], b_last=b_last[hd:hd + 1, :],
                v_ext=jnp.concatenate([v, unit_b], axis=1),
                sqk=_dot_nt(q, k) * decay,
                qe=_dot(q, cx_prev.astype(BF16)),
                kv=_dot(k.astype(F32).T.astype(BF16), rhs)))

    for st in streams:
        ue = _dot(st["sqk"].astype(BF16), st["v_ext"])
        num = ue[:, :M_DIM] + st["qe"][:, :M_DIM] * st["w_inter"]
        den = ue[:, M_DIM:M_DIM + 1] + st["qe"][:, M_DIM:M_DIM + 1] * st["w_inter"]
        inv = 1.0 / jnp.maximum(jnp.abs(den), st["floor"])
        t = jax.nn.sigmoid(o_ref[st["sq"], :, st["lanes"]]) * num
        ms = jnp.sum(t * t, axis=1, keepdims=True) * (1.0 / M_DIM)
        scale = inv * lax.rsqrt(ms * inv * inv + EPS)
        hm_out[st["sq"], :, st["lanes"]] = (t * scale * gm_ref[:, st["lanes"]]).astype(BF16)
        m_last = jnp.maximum(st["m_prev"], st["top"])
        keep = jnp.exp(st["m_prev"] - m_last)
        gain = jnp.exp(st["top"] - m_last)
        cx_s[st["idx"]] = (jnp.concatenate([keep, keep], axis=1) * st["cx_prev"]
                           + jnp.concatenate([gain, gain], axis=1) * st["kv"])
        m_s[st["idx"]] = jnp.broadcast_to(st["b_last"] + m_last, (SUBLANES, LANES))

    @pl.when(ci == pl.num_programs(0) - 1)
    def _():
        cx_out[...] = cx_s[...]
        m_out[...] = m_s[...]


def _mlstm_call(mq, mk, mv, mo, gates, gm):
    b, s, _ = mq.shape
    n_streams = b * M_HEADS
    tok = lambda ci: (0, ci, 0)
    whole3 = lambda ci: (0, 0, 0)
    return pl.pallas_call(
        _mlstm_body,
        grid=(s // CHUNK,),
        in_specs=[pl.BlockSpec((b, CHUNK, MLSTM_WIDTH), tok)] * 4
                 + [pl.BlockSpec((b, CHUNK, LANES), tok), _resident((1, MLSTM_WIDTH))],
        out_specs=[pl.BlockSpec((b, CHUNK, MLSTM_WIDTH), tok),
                   pl.BlockSpec((n_streams, M_DIM, STATE_LANES), whole3),
                   pl.BlockSpec((n_streams, SUBLANES, LANES), whole3)],
        out_shape=[jax.ShapeDtypeStruct((b, s, MLSTM_WIDTH), BF16),
                   jax.ShapeDtypeStruct((n_streams, M_DIM, STATE_LANES), F32),
                   jax.ShapeDtypeStruct((n_streams, SUBLANES, LANES), F32)],
        scratch_shapes=[pltpu.VMEM((n_streams, M_DIM, STATE_LANES), F32),
                        pltpu.VMEM((n_streams, SUBLANES, LANES), F32)],
        compiler_params=_params(1),
        name="prompt_mlstm",
    )(mq, mk, mv, mo, gates, gm)


def _merge_body(h_ref, a_ref, hm_ref, p_ref, ga_ref, woa_ref, wom_ref, gff_ref, wg_ref, wu_ref,
                wd_ref, gple_ref, wpg_ref, wpp_ref, gfin_ref, y_ref):
    a = a_ref[...].astype(F32)
    an = _rms(a, ga_ref[...]).astype(BF16)
    h = h_ref[...] + _dot(an, woa_ref[...]) + _dot(hm_ref[...], wom_ref[...])
    h = h + _swiglu_half(_rms(h, gff_ref[...]).astype(BF16), wg_ref, wu_ref, wd_ref)
    gate = jax.nn.sigmoid(_dot(_rms(h, gple_ref[...]).astype(BF16), wpg_ref[...]))
    h = h + gate * _dot(p_ref[...].astype(BF16), wpp_ref[...])
    y_ref[...] = _rms(h, gfin_ref[...])


def _merge_call(h, a, hm, p, w, tm):
    t = h.shape[0]
    row = lambda i: (i, 0)
    return pl.pallas_call(
        _merge_body,
        grid=(t // tm,),
        in_specs=[pl.BlockSpec((tm, D_MODEL), row), pl.BlockSpec((tm, MLA_WIDTH), row),
                  pl.BlockSpec((tm, MLSTM_WIDTH), row), pl.BlockSpec((tm, PLE_DIM), row),
                  _resident((1, MLA_WIDTH)), _resident((MLA_WIDTH, D_MODEL)),
                  _resident((MLSTM_WIDTH, D_MODEL)), _resident((1, D_MODEL)),
                  _resident((D_MODEL, D_FF)), _resident((D_MODEL, D_FF)), _resident((D_FF, D_MODEL)),
                  _resident((1, D_MODEL)), _resident((D_MODEL, D_MODEL)),
                  _resident((PLE_DIM, D_MODEL)), _resident((1, D_MODEL))],
        out_specs=pl.BlockSpec((tm, D_MODEL), row),
        out_shape=jax.ShapeDtypeStruct((t, D_MODEL), F32),
        compiler_params=_params(1),
        name="merge",
    )(h, a, hm, p, w["g_attn"], w["w_out_a"], w["w_out_m"], w["g_ff2"], w["w_ff2_gate"],
      w["w_ff2_up"], w["w_ff2_down"], w["g_ple"], w["w_ple_gate"], w["w_ple_proj"], w["g_final"])


N_SLOTS = 2
SAMPLE_CHUNKS = 4
SAMPLE_CHUNK_PAGES = N_PAGES // SAMPLE_CHUNKS
SAMPLE_CHUNK_KEYS = SAMPLE_CHUNK_PAGES * PAGE_SIZE


def _sample_attn_body(pt_ref, q_ref, ckv_ref, kr_ref, cache_c, cache_r, wabs_ref, wuv_ref,
                      o_ref, cbuf, rbuf, sem_c, sem_r):
    b = pl.program_id(0)
    nb = pl.num_programs(0)
    slot = b % N_SLOTS

    def copies(bi, sl):
        out = []
        for j in range(N_PAGES):
            page = pt_ref[bi, j]
            out.append(pltpu.make_async_copy(cache_c.at[page], cbuf.at[sl, j], sem_c.at[sl]))
            out.append(pltpu.make_async_copy(
                cache_r.at[page], rbuf.at[sl, :, pl.ds(j * PAGE_SIZE, PAGE_SIZE)], sem_r.at[sl]))
        return out

    @pl.when(b == 0)
    def _():
        for bi in range(min(N_SLOTS, DEC_BATCH)):
            for cp in copies(bi, bi):
                cp.start()

    q_row = q_ref[0]
    sub = lax.broadcasted_iota(jnp.int32, (MLA_HEADS, MLA_PAD_WIDTH), 0)
    lane = lax.broadcasted_iota(jnp.int32, (MLA_HEADS, MLA_PAD_WIDTH), 1)
    own = (lane // HEAD_PAD) == sub
    q_bd = jnp.where(own, jnp.broadcast_to(q_row.astype(F32), own.shape), 0.0).astype(BF16)
    q_ext = _dot(q_bd, wabs_ref[...])
    q_abs = q_ext[:, :KV_LORA].astype(BF16)
    q_rope = q_ext[:, KV_LORA:KV_LORA + ROPE_DIM].astype(BF16)

    c_new = ckv_ref[0].astype(BF16).astype(F32)
    r_new = kr_ref[0].astype(BF16).astype(F32)
    s_new = (jnp.sum(q_abs.astype(F32) * c_new, axis=1, keepdims=True)
             + jnp.sum(q_rope.astype(F32) * r_new, axis=1, keepdims=True))

    for cp in copies(b, slot):
        cp.wait()

    def chunk_keys(i):
        pages = slice(i * SAMPLE_CHUNK_PAGES, (i + 1) * SAMPLE_CHUNK_PAGES)
        keys = slice(i * SAMPLE_CHUNK_KEYS, (i + 1) * SAMPLE_CHUNK_KEYS)
        kc = cbuf[slot, pages].reshape(SAMPLE_CHUNK_KEYS, KV_LORA).astype(BF16)
        return kc, _dot_nt(q_abs, kc) + _dot(q_rope, rbuf[slot, :, keys].astype(BF16))

    m = s_new
    l = jnp.ones_like(s_new)
    acc = jnp.broadcast_to(c_new, (MLA_HEADS, KV_LORA))
    kc, s = chunk_keys(0)
    for i in range(SAMPLE_CHUNKS):
        nxt = chunk_keys(i + 1) if i + 1 < SAMPLE_CHUNKS else None
        m_new = jnp.maximum(m, jnp.max(s, axis=1, keepdims=True))
        alpha = jnp.exp2(m - m_new)
        p = jnp.exp2(s - m_new)
        l = alpha * l + jnp.sum(p, axis=1, keepdims=True)
        acc = alpha * acc + _dot(p.astype(BF16), kc)
        m = m_new
        if nxt is not None:
            kc, s = nxt
    o_lat = (acc / l).astype(BF16)
    res = _dot(o_lat, wuv_ref[...])
    own_v = (lax.broadcasted_iota(jnp.int32, res.shape, 1) // V_DIM
             == lax.broadcasted_iota(jnp.int32, res.shape, 0))
    o_ref[0] = jnp.sum(jnp.where(own_v, res, 0.0), axis=0, keepdims=True).astype(BF16)

    @pl.when(b + N_SLOTS < nb)
    def _():
        for cp in copies(b + N_SLOTS, slot):
            cp.start()


def _sample_attn_call(page_table, q, ckv, kr, cache_c, cache_r, w_abs, w_uv):
    nb = q.shape[0]
    tok = lambda bi, pt: (bi, 0, 0)
    whole = lambda shape: pl.BlockSpec(shape, lambda bi, pt: (0,) * len(shape),
                                       pipeline_mode=pl.Buffered(1))
    grid_spec = pltpu.PrefetchScalarGridSpec(
        num_scalar_prefetch=1,
        grid=(nb,),
        in_specs=[pl.BlockSpec((1, 1, MLA_PAD_WIDTH), tok), pl.BlockSpec((1, 1, KV_LORA), tok),
                  pl.BlockSpec((1, 1, ROPE_DIM), tok),
                  pl.BlockSpec(memory_space=pl.ANY), pl.BlockSpec(memory_space=pl.ANY),
                  whole((MLA_PAD_WIDTH, KV_LORA + LANES)), whole((KV_LORA, MLA_WIDTH))],
        out_specs=pl.BlockSpec((1, 1, MLA_WIDTH), tok),
        scratch_shapes=[pltpu.VMEM((N_SLOTS, N_PAGES, PAGE_SIZE, KV_LORA), F32),
                        pltpu.VMEM((N_SLOTS, ROPE_DIM, PAST_LEN), F32),
                        pltpu.SemaphoreType.DMA((N_SLOTS,)), pltpu.SemaphoreType.DMA((N_SLOTS,))])
    return pl.pallas_call(
        _sample_attn_body,
        grid_spec=grid_spec,
        out_shape=jax.ShapeDtypeStruct((nb, 1, MLA_WIDTH), BF16),
        compiler_params=_params(1),
        name="sample_attn",
    )(page_table, q, ckv, kr, cache_c, cache_r, w_abs, w_uv)


SAMPLE_ROWS = 32


def _sample_mlstm_body(q_ref, k_ref, v_ref, o_ref, g_ref, gm_ref, c_ref, n_ref, m_ref,
                       hm_out, c_out, n_out, m_out):
    hd = pl.program_id(1)
    g = g_ref[...]
    lane = lax.broadcasted_iota(jnp.int32, g.shape, 1)
    ig = jnp.sum(jnp.where(lane == GATE_I_LANE + hd, g, 0.0), axis=1, keepdims=True)
    lf = jnp.sum(jnp.where(lane == GATE_F_LANE + hd, g, 0.0), axis=1, keepdims=True)
    m_all = m_ref[...]
    head_lane = lax.broadcasted_iota(jnp.int32, m_all.shape, 1)
    m0 = jnp.sum(jnp.where(head_lane == hd, m_all, 0.0), axis=1, keepdims=True)
    q = q_ref[...].astype(F32)
    k = k_ref[...].astype(F32)
    v = v_ref[...].astype(F32)
    n0 = n_ref[...]
    m_new = jnp.maximum(lf + m0, ig)
    keep = jnp.exp(lf + m0 - m_new)
    w_src = jnp.exp(ig - m_new)
    sqk = jnp.sum(q * k, axis=1, keepdims=True) * w_src
    wk = w_src * k
    qc = jnp.zeros_like(v)
    for d in range(M_DIM):
        c_d = c_ref[:, 0, d, :]
        qc = qc + q[:, d:d + 1] * c_d
        c_out[:, 0, d, :] = keep * c_d + wk[:, d:d + 1] * v
    num = sqk * v + qc * keep
    den = sqk + jnp.sum(q * n0, axis=1, keepdims=True) * keep
    den = jnp.maximum(jnp.abs(den), jnp.exp(-m_new))
    hh = jax.nn.sigmoid(o_ref[...]) * (num / den)
    hm_out[...] = _rms(hh, gm_ref[...]).astype(BF16)
    n_out[...] = keep * n0 + wk

    @pl.when(hd == 0)
    def _():
        m_out[...] = jnp.broadcast_to(m_new, m_all.shape)

    @pl.when(hd > 0)
    def _():
        m_out[...] = jnp.where(head_lane == hd, m_new, m_out[...])


def _sample_mlstm_call(mq, mk, mv, mo, gates, gm, c0, n0, m0):
    nb = mq.shape[0]
    rows = SAMPLE_ROWS
    tok = lambda bi, hi: (bi, hi)
    return pl.pallas_call(
        _sample_mlstm_body,
        grid=(nb // rows, M_HEADS),
        in_specs=[pl.BlockSpec((rows, M_DIM), tok)] * 4
                 + [pl.BlockSpec((rows, LANES), lambda bi, hi: (bi, 0)),
                    pl.BlockSpec((1, M_DIM), lambda bi, hi: (0, hi)),
                    pl.BlockSpec((rows, 1, M_DIM, M_DIM), lambda bi, hi: (bi, hi, 0, 0)),
                    pl.BlockSpec((rows, M_DIM), tok),
                    pl.BlockSpec((rows, M_HEADS), lambda bi, hi: (bi, 0))],
        out_specs=[pl.BlockSpec((rows, M_DIM), tok),
                   pl.BlockSpec((rows, 1, M_DIM, M_DIM), lambda bi, hi: (bi, hi, 0, 0)),
                   pl.BlockSpec((rows, M_DIM), tok),
                   pl.BlockSpec((rows, M_HEADS), lambda bi, hi: (bi, 0))],
        out_shape=[jax.ShapeDtypeStruct((nb, MLSTM_WIDTH), BF16),
                   jax.ShapeDtypeStruct((nb, M_HEADS, M_DIM, M_DIM), F32),
                   jax.ShapeDtypeStruct((nb, MLSTM_WIDTH), F32),
                   jax.ShapeDtypeStruct((nb, M_HEADS), F32)],
        compiler_params=_params(2),
        name="sample_mlstm",
    )(mq, mk, mv, mo, gates, gm, c0, n0, m0)


def _pad_heads(w, head_dim):
    rows = w.shape[0]
    w = w.reshape(rows, MLA_HEADS, head_dim)
    w = jnp.pad(w, ((0, 0), (0, 0), (0, HEAD_PAD - head_dim)))
    return w.reshape(rows, MLA_PAD_WIDTH)


def _rope_tables(pos):
    inv = ROPE_THETA ** (-jnp.arange(ROPE_HALF, dtype=F32) / ROPE_HALF)
    ang = pos.astype(F32)[:, None] * inv[None, :]
    cos, sin = jnp.cos(ang), jnp.sin(ang)
    n = pos.shape[0]
    pad = jnp.zeros((n, LANES - NOPE_DIM - ROPE_DIM), F32)
    return jnp.concatenate([jnp.ones((n, NOPE_DIM), F32), cos, cos, pad,
                            jnp.zeros((n, NOPE_DIM), F32), -sin, sin, pad], axis=1)


def _prep_weights(g_ff1, w_ff1_gate, w_ff1_up, w_ff1_down, g_mix, w_in, g_q, w_uq, g_kv, w_uk,
                  w_uv, b_gate_i, b_gate_f, g_attn_out, g_mlstm_out, w_out, g_ff2, w_ff2_gate,
                  w_ff2_up, w_ff2_down, g_ple, w_ple_gate, w_ple_proj, g_final):
    bf = lambda a: a.astype(BF16)
    row = lambda a: a.reshape(1, -1).astype(F32)
    off_kv, off_kr = Q_LORA, Q_LORA + KV_LORA
    off_m = off_kr + ROPE_DIM
    off_i = off_m + 4 * MLSTM_WIDTH
    small_pad = LANES - ROPE_DIM - 2 * M_HEADS
    w_sm = jnp.concatenate([w_in[:, off_kr:off_m], w_in[:, off_i:off_i + 2 * M_HEADS],
                            jnp.zeros((D_MODEL, small_pad), F32)], axis=1)
    gate_bias = jnp.concatenate([jnp.zeros((ROPE_DIM,), F32), b_gate_i, b_gate_f,
                                 jnp.zeros((small_pad,), F32)])
    src = jnp.arange(LANES)[:, None]
    dst = jnp.arange(MLA_PAD_WIDTH)[None, :]
    place = ((dst % HEAD_PAD) == (src + NOPE_DIM)) & (src < ROPE_DIM)
    w_abs = jnp.pad(w_uk.reshape(KV_LORA, MLA_HEADS, NOPE_DIM).transpose(1, 2, 0),
                    ((0, 0), (0, HEAD_PAD - NOPE_DIM), (0, 0))).reshape(MLA_PAD_WIDTH, KV_LORA)
    rope_sel = (dst.T % HEAD_PAD) == (src.T + NOPE_DIM)
    rope_sel = rope_sel & (src.T < ROPE_DIM)
    w_abs = jnp.concatenate([w_abs, rope_sel.astype(F32)], axis=1)
    return dict(
        g_ff1=row(g_ff1), w_ff1_gate=bf(w_ff1_gate), w_ff1_up=bf(w_ff1_up), w_ff1_down=bf(w_ff1_down),
        g_mix=row(g_mix), w_q=bf(w_in[:, :off_kv]), w_kv=bf(w_in[:, off_kv:off_kr]), w_sm=bf(w_sm),
        w_m=bf(w_in[:, off_m:off_i]), g_q=row(g_q), w_uq=bf(_pad_heads(w_uq, NOPE_DIM + ROPE_DIM)),
        g_kv=row(g_kv), w_uk=bf(_pad_heads(w_uk, NOPE_DIM)), w_uv=bf(w_uv),
        w_uv_t=bf(jnp.pad(w_uv.T.reshape(MLA_HEADS, V_DIM, KV_LORA),
                          ((0, 0), (0, V_ROWS - V_DIM), (0, 0))).reshape(MLA_HEADS * V_ROWS, KV_LORA)),
        place=place.astype(BF16), gate_bias=row(gate_bias), w_abs=bf(w_abs),
        g_attn=row(g_attn_out), w_out_a=bf(w_out[:MLA_WIDTH]),
        w_out_m=bf(w_out[MLA_WIDTH:]), g_mlstm=row(g_mlstm_out),
        g_ff2=row(g_ff2), w_ff2_gate=bf(w_ff2_gate), w_ff2_up=bf(w_ff2_up), w_ff2_down=bf(w_ff2_down),
        g_ple=row(g_ple), w_ple_gate=bf(w_ple_gate), w_ple_proj=bf(w_ple_proj), g_final=row(g_final))


PROMPT_TILE = 512


def kernel(x_prompt, x_sample, p_prompt, p_sample, cache_ckv, cache_krope, state_C, state_n, state_m, page_table, g_ff1, w_ff1_gate, w_ff1_up, w_ff1_down, g_mix, w_in, g_q, w_uq, g_kv, w_uk, w_uv, b_gate_i, b_gate_f, g_attn_out, g_mlstm_out, w_out, g_ff2, w_ff2_gate, w_ff2_up, w_ff2_down, g_ple, w_ple_gate, w_ple_proj, g_final):
    assert w_in.shape[0] == 1, "single-layer trunk"
    w = _prep_weights(g_ff1[0], w_ff1_gate[0], w_ff1_up[0], w_ff1_down[0], g_mix[0], w_in[0],
                      g_q[0], w_uq[0], g_kv[0], w_uk[0], w_uv[0], b_gate_i[0], b_gate_f[0],
                      g_attn_out[0], g_mlstm_out[0], w_out[0], g_ff2[0], w_ff2_gate[0],
                      w_ff2_up[0], w_ff2_down[0], g_ple[0], w_ple_gate[0], w_ple_proj[0], g_final)
    nb_p, seq, _ = x_prompt.shape
    nb_s = x_sample.shape[0]
    t_p = nb_p * seq

    tab_p = _rope_tables(jnp.arange(seq))
    h_p = _ffn_call(x_prompt.reshape(t_p, D_MODEL), w["g_ff1"], w["w_ff1_gate"], w["w_ff1_up"],
                    w["w_ff1_down"], PROMPT_TILE)
    q_p, k_p, vt_p, ckv_p, kr_p, mq_p, mk_p, mv_p, mo_p, gates_p = _proj_call(
        h_p, w, tab_p, PROMPT_TILE)
    seq3 = lambda a: a.reshape(nb_p, seq, a.shape[-1])
    a_p = _flash_call(seq3(q_p), seq3(k_p), vt_p)
    hm_p, cx_p, m_p = _mlstm_call(seq3(mq_p), seq3(mk_p), seq3(mv_p), seq3(mo_p),
                                  seq3(gates_p), w["g_mlstm"])
    cx_p = cx_p.reshape(nb_p, M_HEADS, M_DIM, STATE_LANES)
    y_p = _merge_call(h_p, a_p.reshape(t_p, MLA_WIDTH), hm_p.reshape(t_p, MLSTM_WIDTH),
                      p_prompt.reshape(t_p, PLE_DIM), w, PROMPT_TILE)

    tab_s = _rope_tables(jnp.full((nb_s,), PAST_LEN, jnp.int32))
    h_s = _ffn_call(x_sample.reshape(nb_s, D_MODEL), w["g_ff1"], w["w_ff1_gate"], w["w_ff1_up"],
                    w["w_ff1_down"], nb_s)
    q_s, _, _, ckv_s, kr_s, mq_s, mk_s, mv_s, mo_s, gates_s = _proj_call(h_s, w, tab_s, nb_s)
    n_phys = cache_ckv.shape[1]
    a_s = _sample_attn_call(
        page_table, q_s.reshape(nb_s, 1, MLA_PAD_WIDTH), ckv_s.reshape(nb_s, 1, KV_LORA),
        kr_s.reshape(nb_s, 1, ROPE_DIM), cache_ckv.reshape(n_phys, PAGE_SIZE, KV_LORA),
        jnp.swapaxes(cache_krope.reshape(n_phys, PAGE_SIZE, ROPE_DIM), 1, 2), w["w_abs"], w["w_uv"])
    hm_s, c_s, n_s, m_s = _sample_mlstm_call(
        mq_s, mk_s, mv_s, mo_s, gates_s, w["g_mlstm"], state_C[0].astype(F32),
        state_n[0].astype(F32).reshape(nb_s, MLSTM_WIDTH), state_m[0].astype(F32))
    y_s = _merge_call(h_s, a_s.reshape(nb_s, MLA_WIDTH), hm_s, p_sample.reshape(nb_s, PLE_DIM),
                      w, nb_s)

    return (y_p.reshape(nb_p, seq, D_MODEL), y_s.reshape(nb_s, 1, D_MODEL),
            ckv_p.reshape(1, nb_p, seq, KV_LORA), kr_p.reshape(1, nb_p, seq, ROPE_DIM),
            cx_p[None, ..., :M_DIM], cx_p[None, ..., M_DIM], m_p[:, 0, 0].reshape(1, nb_p, M_HEADS),
            ckv_s.reshape(1, nb_s, 1, KV_LORA), kr_s.reshape(1, nb_s, 1, ROPE_DIM),
            c_s[None], n_s.reshape(1, nb_s, M_HEADS, M_DIM), m_s[None])
```

```python
import functools
import math

import jax
import jax.numpy as jnp
from jax import lax
from jax.experimental import pallas as pl
from jax.experimental.pallas import tpu as pltpu

F32 = jnp.float32
BF16 = jnp.bfloat16

D_MODEL = 1024
SEQ = 8192
DEC_BATCH = 128
PAST_LEN = 8192
PAGE_SIZE = 128
N_PAGES = PAST_LEN // PAGE_SIZE
MLA_HEADS = 8
Q_LORA = 384
KV_LORA = 256
NOPE_DIM = 64
ROPE_DIM = 32
ROPE_HALF = ROPE_DIM // 2
V_DIM = 64
ROPE_THETA = 10000.0
M_HEADS = 4
M_DIM = 128
CHUNK = 128
MLSTM_WIDTH = M_HEADS * M_DIM
D_FF = 2816
PLE_DIM = 256
EPS = 1e-6

LANES = 128
SUBLANES = 8

HEAD_PAD = LANES
MLA_PAD_WIDTH = MLA_HEADS * HEAD_PAD
MLA_WIDTH = MLA_HEADS * V_DIM
V_ROWS = 80
GATE_I_LANE = ROPE_DIM
GATE_F_LANE = ROPE_DIM + M_HEADS
QK_SCALE = (NOPE_DIM + ROPE_DIM) ** -0.5 * math.log2(math.e)

VMEM_LIMIT = 56 * 1024 * 1024


def _dot(a, b):
    return jnp.dot(a, b, preferred_element_type=F32)


def _dot_nt(a, b):
    return lax.dot_general(a, b, (((1,), (1,)), ((), ())), preferred_element_type=F32)


def _dot_tn(a, b):
    return lax.dot_general(a, b, (((0,), (0,)), ((), ())), preferred_element_type=F32)


def _rms(x, g):
    ms = jnp.sum(x * x, axis=-1, keepdims=True) * (1.0 / x.shape[-1])
    return x * lax.rsqrt(ms + EPS) * g


def _resident(shape):
    return pl.BlockSpec(shape, lambda *_: (0,) * len(shape), pipeline_mode=pl.Buffered(1))


def _params(n_axes):
    return pltpu.CompilerParams(dimension_semantics=("arbitrary",) * n_axes,
                                vmem_limit_bytes=VMEM_LIMIT)


MXU_TILE = 256
FF_CHUNK_BOUNDS = (0, 6 * MXU_TILE, D_FF)
assert D_FF % MXU_TILE == 0


def _swiglu_half(xn, wg_ref, wu_ref, wd_ref):
    out = None
    for lo, hi in zip(FF_CHUNK_BOUNDS[:-1], FF_CHUNK_BOUNDS[1:]):
        cols = slice(lo, hi)
        gate = _dot(xn, wg_ref[:, cols])
        up = _dot(xn, wu_ref[:, cols])
        act = (jax.nn.silu(gate) * up).astype(BF16)
        part = _dot(act, wd_ref[cols, :])
        out = part if out is None else out + part
    return 0.5 * out


def _ffn_body(x_ref, g_ref, wg_ref, wu_ref, wd_ref, o_ref):
    x = x_ref[...]
    xn = _rms(x, g_ref[...]).astype(BF16)
    o_ref[...] = x + _swiglu_half(xn, wg_ref, wu_ref, wd_ref)


def _ffn_call(x, g, wg, wu, wd, tm):
    t = x.shape[0]
    row = lambda i: (i, 0)
    return pl.pallas_call(
        _ffn_body,
        grid=(t // tm,),
        in_specs=[pl.BlockSpec((tm, D_MODEL), row), _resident((1, D_MODEL)),
                  _resident((D_MODEL, D_FF)), _resident((D_MODEL, D_FF)), _resident((D_FF, D_MODEL))],
        out_specs=pl.BlockSpec((tm, D_MODEL), row),
        out_shape=jax.ShapeDtypeStruct((t, D_MODEL), F32),
        compiler_params=_params(1),
        name="ffn1",
    )(x, g, wg, wu, wd)


def _rope128(x, cos_tab, sin_tab, x2_start):
    lane = lax.broadcasted_iota(jnp.int32, x.shape, 1)
    partner = jnp.where(lane < x2_start, pltpu.roll(x, LANES - ROPE_HALF, 1),
                        pltpu.roll(x, ROPE_HALF, 1))
    return x * cos_tab + partner * sin_tab


def _proj_body(h_ref, gmix_ref, wq_ref, wkv_ref, wsm_ref, wm_ref, gq_ref, wuq_ref, gkv_ref,
               wuk_ref, wuvt_ref, bias_ref, tab_ref,
               q_out, k_out, vt_out, ckv_out, kr_out, krt_out, mq_out, mk_out, mv_out, mo_out, gate_out):
    u = _rms(h_ref[...], gmix_ref[...]).astype(BF16)
    ckv = _rms(_dot(u, wkv_ref[...]), gkv_ref[...])
    ckv_out[...] = ckv
    ckv_b = ckv.astype(BF16)
    zs = _dot(u, wsm_ref[...])
    n_tok = zs.shape[0]
    cs = jnp.concatenate([tab_ref[...], jnp.zeros((LANES - ROPE_DIM, n_tok), F32)], axis=0).T
    lane = lax.broadcasted_iota(jnp.int32, zs.shape, 1)
    shifted = lambda by: pltpu.roll(cs, by, 1)
    cos_k = jnp.where(lane < ROPE_HALF, cs, jnp.where(lane < ROPE_DIM, shifted(ROPE_HALF), 0.0))
    sin_k = jnp.where(lane < ROPE_HALF, -shifted(LANES - ROPE_HALF), jnp.where(lane < ROPE_DIM, cs, 0.0))
    kr = _rope128(zs, cos_k, sin_k, ROPE_HALF)
    kr_out[...] = kr[:, :ROPE_DIM]
    krt_out[0] = kr.T[:ROPE_DIM, :]
    kn = _dot(ckv_b, wuk_ref[...])
    kr_head = pltpu.roll(kr, NOPE_DIM, 1)
    for hd in range(MLA_HEADS):
        lanes = slice(hd * HEAD_PAD, (hd + 1) * HEAD_PAD)
        k_out[:, lanes] = (kn[:, lanes] + kr_head).astype(BF16)
    x1 = (lane >= NOPE_DIM) & (lane < NOPE_DIM + ROPE_HALF)
    x2 = (lane >= NOPE_DIM + ROPE_HALF) & (lane < NOPE_DIM + ROPE_DIM)
    cos_q = jnp.where(lane < NOPE_DIM, 1.0, jnp.where(x1, shifted(NOPE_DIM), jnp.where(
        x2, shifted(NOPE_DIM + ROPE_HALF), 0.0)))
    sin_q = jnp.where(x1, -shifted(NOPE_DIM - ROPE_HALF), jnp.where(x2, shifted(NOPE_DIM), 0.0))
    vt = _dot_nt(wuvt_ref[...], ckv_b)
    vrow = lax.broadcasted_iota(jnp.int32, vt.shape, 0)
    vt_out[0, 0] = jnp.where(vrow % V_ROWS == V_DIM, 1.0, vt).astype(BF16)
    qn = _rms(_dot(u, wq_ref[...]), gq_ref[...]).astype(BF16)
    q = _dot(qn, wuq_ref[...])
    cos_qs = cos_q * QK_SCALE
    sin_qs = sin_q * QK_SCALE
    for hd in range(MLA_HEADS):
        lanes = slice(hd * HEAD_PAD, (hd + 1) * HEAD_PAD)
        q_out[:, lanes] = _rope128(q[:, lanes], cos_qs, sin_qs, NOPE_DIM + ROPE_HALF).astype(BF16)
    zm = _dot(u, wm_ref[...])
    mq_out[...] = zm[:, 0:MLSTM_WIDTH].astype(BF16)
    mk_out[...] = (zm[:, MLSTM_WIDTH:2 * MLSTM_WIDTH] * (M_DIM ** -0.5)).astype(BF16)
    mv_out[...] = zm[:, 2 * MLSTM_WIDTH:3 * MLSTM_WIDTH].astype(BF16)
    mo_out[...] = zm[:, 3 * MLSTM_WIDTH:4 * MLSTM_WIDTH]
    zb = zs + bias_ref[...]
    log_sig = jnp.minimum(zb, 0.0) - jnp.log1p(jnp.exp(-jnp.abs(zb)))
    is_f = (lane >= GATE_F_LANE) & (lane < GATE_F_LANE + M_HEADS)
    gate_out[...] = jnp.where(is_f, log_sig, zb)


def _proj_call(h, w, rope_tab, tm):
    t = h.shape[0]
    n_tab = rope_tab.shape[1] // tm
    row = lambda i: (i, 0)
    tab = lambda i: (0, i % n_tab)
    v_width = MLA_HEADS * V_ROWS
    tok_outs = ([jax.ShapeDtypeStruct((t, MLA_PAD_WIDTH), BF16)] * 2
                + [jax.ShapeDtypeStruct((t, KV_LORA), F32), jax.ShapeDtypeStruct((t, ROPE_DIM), F32)]
                + [jax.ShapeDtypeStruct((t, MLSTM_WIDTH), BF16)] * 3
                + [jax.ShapeDtypeStruct((t, MLSTM_WIDTH), F32), jax.ShapeDtypeStruct((t, LANES), F32)])
    tok_specs = [pl.BlockSpec((tm, s.shape[1]), row) for s in tok_outs]
    vt_shape = jax.ShapeDtypeStruct((t // tm // n_tab, n_tab, v_width, tm), BF16)
    vt_spec = pl.BlockSpec((1, 1, v_width, tm), lambda i: (i // n_tab, i % n_tab, 0, 0))
    krt_shape = jax.ShapeDtypeStruct((t // tm // n_tab, ROPE_DIM, n_tab * tm), F32)
    krt_spec = pl.BlockSpec((1, ROPE_DIM, tm), lambda i: (i // n_tab, 0, i % n_tab))
    out_specs = tok_specs[:2] + [vt_spec] + tok_specs[2:4] + [krt_spec] + tok_specs[4:]
    out_shape = tok_outs[:2] + [vt_shape] + tok_outs[2:4] + [krt_shape] + tok_outs[4:]
    return pl.pallas_call(
        _proj_body,
        grid=(t // tm,),
        in_specs=[pl.BlockSpec((tm, D_MODEL), row), _resident((1, D_MODEL)),
                  _resident((D_MODEL, Q_LORA)), _resident((D_MODEL, KV_LORA)),
                  _resident((D_MODEL, LANES)), _resident((D_MODEL, 4 * MLSTM_WIDTH)),
                  _resident((1, Q_LORA)), _resident((Q_LORA, MLA_PAD_WIDTH)),
                  _resident((1, KV_LORA)), _resident((KV_LORA, MLA_PAD_WIDTH)),
                  _resident((v_width, KV_LORA)),
                  _resident((1, LANES)),
                  pl.BlockSpec((ROPE_DIM, tm), tab)],
        out_specs=out_specs,
        out_shape=out_shape,
        compiler_params=_params(1),
        name="proj",
    )(h, w["g_mix"], w["w_q"], w["w_kv"], w["w_sm"], w["w_m"], w["g_q"], w["w_uq"], w["g_kv"],
      w["w_uk"], w["w_uv_t"], w["gate_bias"], rope_tab)


ATT_BLOCK = 512
ATT_HEADS = 8
ATT_LANES = ATT_HEADS * HEAD_PAD
ATT_V_ROWS = ATT_HEADS * V_ROWS
ATT_LOOKAHEAD = 3


def _flash_body(q_ref, k_ref, vt_ref, o_ref):
    qi = pl.program_id(2)
    blk = ATT_BLOCK
    head_lanes = [slice(hd * HEAD_PAD, (hd + 1) * HEAD_PAD) for hd in range(ATT_HEADS)]
    qs = [q_ref[0, :, lanes] for lanes in head_lanes]

    def step(j, carry, diagonal):
        start = pl.multiple_of(j * blk, blk)
        scores = [None] * ATT_HEADS
        if diagonal:
            key = lax.broadcasted_iota(jnp.int32, (blk, blk), 0)
            qry = lax.broadcasted_iota(jnp.int32, (blk, blk), 1)
            visible = key <= qry
        out = []
        for hd in range(ATT_HEADS):
            m, acc = carry[hd]
            for nxt in range(hd if hd else 0, min(hd + ATT_LOOKAHEAD, ATT_HEADS - 1) + 1):
                if scores[nxt] is None:
                    scores[nxt] = _dot_nt(k_ref[0, pl.ds(start, blk), head_lanes[nxt]], qs[nxt])
            s = jnp.where(visible, scores[hd], -jnp.inf) if diagonal else scores[hd]
            m_new = jnp.maximum(m, jnp.max(s, axis=0, keepdims=True))
            p = jnp.exp2(s - m_new).astype(BF16)
            vt = vt_ref[0, j, hd * V_ROWS:(hd + 1) * V_ROWS, :]
            acc = jnp.exp2(m - m_new) * acc + _dot(vt, p)
            out.append((m_new, acc))
        return tuple(out)

    init = tuple((jnp.full((1, blk), -jnp.inf, F32), jnp.zeros((V_ROWS, blk), F32))
                 for _ in range(ATT_HEADS))
    carry = lax.fori_loop(0, qi, lambda j, c: step(j, c, False), init)
    final = step(qi, carry, True)
    for pair in range(ATT_HEADS // 2):
        o_t = jnp.concatenate([final[hd][1][:V_DIM] / final[hd][1][V_DIM:V_DIM + 1]
                               for hd in (2 * pair, 2 * pair + 1)], axis=0)
        o_ref[0, :, pair * LANES:(pair + 1) * LANES] = o_t.T.astype(BF16)


def _flash_call(q, k, vt):
    b, s, _ = q.shape
    assert vt.shape == (b, s // ATT_BLOCK, MLA_HEADS * V_ROWS, ATT_BLOCK)
    qmap = lambda bi, gi, qi: (bi, qi, gi)
    return pl.pallas_call(
        _flash_body,
        grid=(b, MLA_HEADS // ATT_HEADS, s // ATT_BLOCK),
        in_specs=[pl.BlockSpec((1, ATT_BLOCK, ATT_LANES), qmap),
                  pl.BlockSpec((1, s, ATT_LANES), lambda bi, gi, qi: (bi, 0, gi),
                               pipeline_mode=pl.Buffered(1)),
                  pl.BlockSpec((1, s // ATT_BLOCK, ATT_V_ROWS, ATT_BLOCK),
                               lambda bi, gi, qi: (bi, 0, gi, 0), pipeline_mode=pl.Buffered(1))],
        out_specs=pl.BlockSpec((1, ATT_BLOCK, ATT_HEADS * V_DIM), qmap),
        out_shape=jax.ShapeDtypeStruct((b, s, MLA_WIDTH), BF16),
        compiler_params=_params(3),
        name="prompt_attn",
    )(q, k, vt)


STATE_LANES = 2 * M_DIM


def _lane_scan(x, op, fill):
    lane = lax.broadcasted_iota(jnp.int32, x.shape, 1)
    shift = 1
    while shift < x.shape[1]:
        x = op(x, jnp.where(lane >= shift, pltpu.roll(x, shift, 1), fill))
        shift *= 2
    return x


def _mlstm_body(q_ref, k_ref, v_ref, o_ref, g_ref, gm_ref, hm_out, cx_out, m_out, cx_s, m_s):
    ci = pl.program_id(0)
    n_seq = q_ref.shape[0]
    L = CHUNK

    @pl.when(ci == 0)
    def _():
        cx_s[...] = jnp.zeros_like(cx_s)
        m_s[...] = jnp.zeros_like(m_s)

    t_idx = lax.broadcasted_iota(jnp.int32, (L, L), 0)
    s_idx = lax.broadcasted_iota(jnp.int32, (L, L), 1)
    causal = s_idx <= t_idx
    unit = (lax.broadcasted_iota(jnp.int32, (L, M_DIM), 1) == 0).astype(F32)
    unit_b = unit.astype(BF16)

    streams = []
    for sq in range(n_seq):
        rows = g_ref[sq].T[GATE_I_LANE:GATE_I_LANE + 2 * M_HEADS, :]
        b_rows = pltpu.roll(_lane_scan(rows, jnp.add, 0.0), M_HEADS, 0)
        head_row = lax.broadcasted_iota(jnp.int32, rows.shape, 0) < M_HEADS
        a_rows = jnp.where(head_row, rows - b_rows, 0.0)
        run_max = _lane_scan(a_rows, jnp.maximum, -jnp.inf)
        top = jnp.broadcast_to(jnp.max(a_rows, axis=1, keepdims=True), rows.shape)
        b_last = pltpu.roll(jnp.broadcast_to(jnp.sum(rows, axis=1, keepdims=True), rows.shape),
                            M_HEADS, 0)
        src = jnp.exp(a_rows - top)
        cols = jnp.concatenate([run_max, b_rows, src, jnp.zeros((L - 3 * SUBLANES, L), F32)],
                               axis=0).T
        for hd in range(M_HEADS):
            lanes = slice(hd * M_DIM, (hd + 1) * M_DIM)
            idx = sq * M_HEADS + hd
            q = q_ref[sq, :, lanes]
            k = k_ref[sq, :, lanes]
            v = v_ref[sq, :, lanes]
            m_prev = m_s[idx][0:1, :]
            m_prev_col = jnp.broadcast_to(m_prev, (L, L))[:, 0:1]
            big_m = jnp.maximum(m_prev_col, cols[:, hd:hd + 1])
            decay = jnp.where(causal, jnp.exp(a_rows[hd:hd + 1, :] - big_m), 0.0)
            cx_prev = cx_s[idx]
            src_col = cols[:, 2 * SUBLANES + hd:2 * SUBLANES + hd + 1]
            rhs = (src_col * jnp.concatenate([v.astype(F32), unit], axis=1)).astype(BF16)
            streams.append(dict(
                sq=sq, lanes=lanes, idx=idx, m_prev=m_prev, big_m=big_m, cx_prev=cx_prev,
                w_inter=jnp.exp(m_prev_col - big_m),
                floor=jnp.exp(-(cols[:, SUBLANES + hd:SUBLANES + hd + 1] + big_m)),
                top=top[hd:hd + 1, :], b_last=b_last[hd:hd + 1, :],
                v_ext=jnp.concatenate([v, unit_b], axis=1),
                sqk=_dot_nt(q, k) * decay,
                qe=_dot(q, cx_prev.astype(BF16)),
                kv=_dot(k.astype(F32).T.astype(BF16), rhs)))

    for st in streams:
        ue = _dot(st["sqk"].astype(BF16), st["v_ext"])
        num = ue[:, :M_DIM] + st["qe"][:, :M_DIM] * st["w_inter"]
        den = ue[:, M_DIM:M_DIM + 1] + st["qe"][:, M_DIM:M_DIM + 1] * st["w_inter"]
        inv = 1.0 / jnp.maximum(jnp.abs(den), st["floor"])
        t = jax.nn.sigmoid(o_ref[st["sq"], :, st["lanes"]]) * num
        ms = jnp.sum(t * t, axis=1, keepdims=True) * (1.0 / M_DIM)
        scale = inv * lax.rsqrt(ms * inv * inv + EPS)
        hm_out[st["sq"], :, st["lanes"]] = (t * scale * gm_ref[:, st["lanes"]]).astype(BF16)
        m_last = jnp.maximum(st["m_prev"], st["top"])
        keep = jnp.exp(st["m_prev"] - m_last)
        gain = jnp.exp(st["top"] - m_last)
        cx_s[st["idx"]] = (jnp.concatenate([keep, keep], axis=1) * st["cx_prev"]
                           + jnp.concatenate([gain, gain], axis=1) * st["kv"])
        m_s[st["idx"]] = jnp.broadcast_to(st["b_last"] + m_last, (SUBLANES, LANES))

    @pl.when(ci == pl.num_programs(0) - 1)
    def _():
        cx_out[...] = cx_s[...]
        m_out[...] = m_s[...]


def _mlstm_call(mq, mk, mv, mo, gates, gm):
    b, s, _ = mq.shape
    n_streams = b * M_HEADS
    tok = lambda ci: (0, ci, 0)
    whole3 = lambda ci: (0, 0, 0)
    return pl.pallas_call(
        _mlstm_body,
        grid=(s // CHUNK,),
        in_specs=[pl.BlockSpec((b, CHUNK, MLSTM_WIDTH), tok)] * 4
                 + [pl.BlockSpec((b, CHUNK, LANES), tok), _resident((1, MLSTM_WIDTH))],
        out_specs=[pl.BlockSpec((b, CHUNK, MLSTM_WIDTH), tok),
                   pl.BlockSpec((n_streams, M_DIM, STATE_LANES), whole3),
                   pl.BlockSpec((n_streams, SUBLANES, LANES), whole3)],
        out_shape=[jax.ShapeDtypeStruct((b, s, MLSTM_WIDTH), BF16),
                   jax.ShapeDtypeStruct((n_streams, M_DIM, STATE_LANES), F32),
                   jax.ShapeDtypeStruct((n_streams, SUBLANES, LANES), F32)],
        scratch_shapes=[pltpu.VMEM((n_streams, M_DIM, STATE_LANES), F32),
                        pltpu.VMEM((n_streams, SUBLANES, LANES), F32)],
        compiler_params=_params(1),
        name="prompt_mlstm",
    )(mq, mk, mv, mo, gates, gm)


def _merge_body(h_ref, a_ref, hm_ref, p_ref, ga_ref, woa_ref, wom_ref, gff_ref, wg_ref, wu_ref,
                wd_ref, gple_ref, wpg_ref, wpp_ref, gfin_ref, y_ref):
    a = a_ref[...].astype(F32)
    an = _rms(a, ga_ref[...]).astype(BF16)
    h = h_ref[...] + _dot(an, woa_ref[...]) + _dot(hm_ref[...], wom_ref[...])
    h = h + _swiglu_half(_rms(h, gff_ref[...]).astype(BF16), wg_ref, wu_ref, wd_ref)
    gate = jax.nn.sigmoid(_dot(_rms(h, gple_ref[...]).astype(BF16), wpg_ref[...]))
    h = h + gate * _dot(p_ref[...].astype(BF16), wpp_ref[...])
    y_ref[...] = _rms(h, gfin_ref[...])


def _merge_call(h, a, hm, p, w, tm):
    t = h.shape[0]
    row = lambda i: (i, 0)
    return pl.pallas_call(
        _merge_body,
        grid=(t // tm,),
        in_specs=[pl.BlockSpec((tm, D_MODEL), row), pl.BlockSpec((tm, MLA_WIDTH), row),
                  pl.BlockSpec((tm, MLSTM_WIDTH), row), pl.BlockSpec((tm, PLE_DIM), row),
                  _resident((1, MLA_WIDTH)), _resident((MLA_WIDTH, D_MODEL)),
                  _resident((MLSTM_WIDTH, D_MODEL)), _resident((1, D_MODEL)),
                  _resident((D_MODEL, D_FF)), _resident((D_MODEL, D_FF)), _resident((D_FF, D_MODEL)),
                  _resident((1, D_MODEL)), _resident((D_MODEL, D_MODEL)),
                  _resident((PLE_DIM, D_MODEL)), _resident((1, D_MODEL))],
        out_specs=pl.BlockSpec((tm, D_MODEL), row),
        out_shape=jax.ShapeDtypeStruct((t, D_MODEL), F32),
        compiler_params=_params(1),
        name="merge",
    )(h, a, hm, p, w["g_attn"], w["w_out_a"], w["w_out_m"], w["g_ff2"], w["w_ff2_gate"],
      w["w_ff2_up"], w["w_ff2_down"], w["g_ple"], w["w_ple_gate"], w["w_ple_proj"], w["g_final"])


N_SLOTS = 2
SAMPLE_CHUNKS = 4
SAMPLE_CHUNK_PAGES = N_PAGES // SAMPLE_CHUNKS
SAMPLE_CHUNK_KEYS = SAMPLE_CHUNK_PAGES * PAGE_SIZE


def _sample_attn_body(pt_ref, q_ref, ckv_ref, kr_ref, cache_c, cache_r, wabs_ref, wuv_ref,
                      o_ref, cbuf, rbuf, sem_c, sem_r):
    b = pl.program_id(0)
    nb = pl.num_programs(0)
    slot = b % N_SLOTS

    def copies(bi, sl):
        out = []
        for j in range(N_PAGES):
            page = pt_ref[bi, j]
            out.append(pltpu.make_async_copy(cache_c.at[page], cbuf.at[sl, j], sem_c.at[sl]))
            out.append(pltpu.make_async_copy(
                cache_r.at[page], rbuf.at[sl, :, pl.ds(j * PAGE_SIZE, PAGE_SIZE)], sem_r.at[sl]))
        return out

    @pl.when(b == 0)
    def _():
        for bi in range(min(N_SLOTS, DEC_BATCH)):
            for cp in copies(bi, bi):
                cp.start()

    q_row = q_ref[0]
    sub = lax.broadcasted_iota(jnp.int32, (MLA_HEADS, MLA_PAD_WIDTH), 0)
    lane = lax.broadcasted_iota(jnp.int32, (MLA_HEADS, MLA_PAD_WIDTH), 1)
    own = (lane // HEAD_PAD) == sub
    q_bd = jnp.where(own, jnp.broadcast_to(q_row.astype(F32), own.shape), 0.0).astype(BF16)
    q_ext = _dot(q_bd, wabs_ref[...])
    q_abs = q_ext[:, :KV_LORA].astype(BF16)
    q_rope = q_ext[:, KV_LORA:KV_LORA + ROPE_DIM].astype(BF16)

    c_new = ckv_ref[0].astype(BF16).astype(F32)
    r_new = kr_ref[0].astype(BF16).astype(F32)
    s_new = (jnp.sum(q_abs.astype(F32) * c_new, axis=1, keepdims=True)
             + jnp.sum(q_rope.astype(F32) * r_new, axis=1, keepdims=True))

    for cp in copies(b, slot):
        cp.wait()

    def chunk_keys(i):
        pages = slice(i * SAMPLE_CHUNK_PAGES, (i + 1) * SAMPLE_CHUNK_PAGES)
        keys = slice(i * SAMPLE_CHUNK_KEYS, (i + 1) * SAMPLE_CHUNK_KEYS)
        kc = cbuf[slot, pages].reshape(SAMPLE_CHUNK_KEYS, KV_LORA).astype(BF16)
        return kc, _dot_nt(q_abs, kc) + _dot(q_rope, rbuf[slot, :, keys].astype(BF16))

    m = s_new
    l = jnp.ones_like(s_new)
    acc = jnp.broadcast_to(c_new, (MLA_HEADS, KV_LORA))
    kc, s = chunk_keys(0)
    for i in range(SAMPLE_CHUNKS):
        nxt = chunk_keys(i + 1) if i + 1 < SAMPLE_CHUNKS else None
        m_new = jnp.maximum(m, jnp.max(s, axis=1, keepdims=True))
        alpha = jnp.exp2(m - m_new)
        p = jnp.exp2(s - m_new)
        l = alpha * l + jnp.sum(p, axis=1, keepdims=True)
        acc = alpha * acc + _dot(p.astype(BF16), kc)
        m = m_new
        if nxt is not None:
            kc, s = nxt
    o_lat = (acc / l).astype(BF16)
    res = _dot(o_lat, wuv_ref[...])
    own_v = (lax.broadcasted_iota(jnp.int32, res.shape, 1) // V_DIM
             == lax.broadcasted_iota(jnp.int32, res.shape, 0))
    o_ref[0] = jnp.sum(jnp.where(own_v, res, 0.0), axis=0, keepdims=True).astype(BF16)

    @pl.when(b + N_SLOTS < nb)
    def _():
        for cp in copies(b + N_SLOTS, slot):
            cp.start()


def _sample_attn_call(page_table, q, ckv, kr, cache_c, cache_r, w_abs, w_uv):
    nb = q.shape[0]
    tok = lambda bi, pt: (bi, 0, 0)
    whole = lambda shape: pl.BlockSpec(shape, lambda bi, pt: (0,) * len(shape),
                                       pipeline_mode=pl.Buffered(1))
    grid_spec = pltpu.PrefetchScalarGridSpec(
        num_scalar_prefetch=1,
        grid=(nb,),
        in_specs=[pl.BlockSpec((1, 1, MLA_PAD_WIDTH), tok), pl.BlockSpec((1, 1, KV_LORA), tok),
                  pl.BlockSpec((1, 1, ROPE_DIM), tok),
                  pl.BlockSpec(memory_space=pl.ANY), pl.BlockSpec(memory_space=pl.ANY),
                  whole((MLA_PAD_WIDTH, KV_LORA + LANES)), whole((KV_LORA, MLA_WIDTH))],
        out_specs=pl.BlockSpec((1, 1, MLA_WIDTH), tok),
        scratch_shapes=[pltpu.VMEM((N_SLOTS, N_PAGES, PAGE_SIZE, KV_LORA), F32),
                        pltpu.VMEM((N_SLOTS, ROPE_DIM, PAST_LEN), F32),
                        pltpu.SemaphoreType.DMA((N_SLOTS,)), pltpu.SemaphoreType.DMA((N_SLOTS,))])
    return pl.pallas_call(
        _sample_attn_body,
        grid_spec=grid_spec,
        out_shape=jax.ShapeDtypeStruct((nb, 1, MLA_WIDTH), BF16),
        compiler_params=_params(1),
        name="sample_attn",
    )(page_table, q, ckv, kr, cache_c, cache_r, w_abs, w_uv)


SAMPLE_ROWS = 32


def _sample_mlstm_body(q_ref, k_ref, v_ref, o_ref, g_ref, gm_ref, c_ref, n_ref, m_ref,
                       hm_out, c_out, n_out, m_out):
    hd = pl.program_id(1)
    g = g_ref[...]
    lane = lax.broadcasted_iota(jnp.int32, g.shape, 1)
    ig = jnp.sum(jnp.where(lane == GATE_I_LANE + hd, g, 0.0), axis=1, keepdims=True)
    lf = jnp.sum(jnp.where(lane == GATE_F_LANE + hd, g, 0.0), axis=1, keepdims=True)
    m_all = m_ref[...]
    head_lane = lax.broadcasted_iota(jnp.int32, m_all.shape, 1)
    m0 = jnp.sum(jnp.where(head_lane == hd, m_all, 0.0), axis=1, keepdims=True)
    q = q_ref[...].astype(F32)
    k = k_ref[...].astype(F32)
    v = v_ref[...].astype(F32)
    n0 = n_ref[...]
    m_new = jnp.maximum(lf + m0, ig)
    keep = jnp.exp(lf + m0 - m_new)
    w_src = jnp.exp(ig - m_new)
    sqk = jnp.sum(q * k, axis=1, keepdims=True) * w_src
    wk = w_src * k
    qc = jnp.zeros_like(v)
    for d in range(M_DIM):
        c_d = c_ref[:, 0, d, :]
        qc = qc + q[:, d:d + 1] * c_d
        c_out[:, 0, d, :] = keep * c_d + wk[:, d:d + 1] * v
    num = sqk * v + qc * keep
    den = sqk + jnp.sum(q * n0, axis=1, keepdims=True) * keep
    den = jnp.maximum(jnp.abs(den), jnp.exp(-m_new))
    hh = jax.nn.sigmoid(o_ref[...]) * (num / den)
    hm_out[...] = _rms(hh, gm_ref[...]).astype(BF16)
    n_out[...] = keep * n0 + wk

    @pl.when(hd == 0)
    def _():
        m_out[...] = jnp.broadcast_to(m_new, m_all.shape)

    @pl.when(hd > 0)
    def _():
        m_out[...] = jnp.where(head_lane == hd, m_new, m_out[...])


def _sample_mlstm_call(mq, mk, mv, mo, gates, gm, c0, n0, m0):
    nb = mq.shape[0]
    rows = SAMPLE_ROWS
    tok = lambda bi, hi: (bi, hi)
    return pl.pallas_call(
        _sample_mlstm_body,
        grid=(nb // rows, M_HEADS),
        in_specs=[pl.BlockSpec((rows, M_DIM), tok)] * 4
                 + [pl.BlockSpec((rows, LANES), lambda bi, hi: (bi, 0)),
                    pl.BlockSpec((1, M_DIM), lambda bi, hi: (0, hi)),
                    pl.BlockSpec((rows, 1, M_DIM, M_DIM), lambda bi, hi: (bi, hi, 0, 0)),
                    pl.BlockSpec((rows, M_DIM), tok),
                    pl.BlockSpec((rows, M_HEADS), lambda bi, hi: (bi, 0))],
        out_specs=[pl.BlockSpec((rows, M_DIM), tok),
                   pl.BlockSpec((rows, 1, M_DIM, M_DIM), lambda bi, hi: (bi, hi, 0, 0)),
                   pl.BlockSpec((rows, M_DIM), tok),
                   pl.BlockSpec((rows, M_HEADS), lambda bi, hi: (bi, 0))],
        out_shape=[jax.ShapeDtypeStruct((nb, MLSTM_WIDTH), BF16),
                   jax.ShapeDtypeStruct((nb, M_HEADS, M_DIM, M_DIM), F32),
                   jax.ShapeDtypeStruct((nb, MLSTM_WIDTH), F32),
                   jax.ShapeDtypeStruct((nb, M_HEADS), F32)],
        compiler_params=_params(2),
        name="sample_mlstm",
    )(mq, mk, mv, mo, gates, gm, c0, n0, m0)


def _pad_heads(w, head_dim):
    rows = w.shape[0]
    w = w.reshape(rows, MLA_HEADS, head_dim)
    w = jnp.pad(w, ((0, 0), (0, 0), (0, HEAD_PAD - head_dim)))
    return w.reshape(rows, MLA_PAD_WIDTH)


def _rope_tables(pos):
    inv = ROPE_THETA ** (-jnp.arange(ROPE_HALF, dtype=F32) / ROPE_HALF)
    ang = inv[:, None] * pos.astype(F32)[None, :]
    return jnp.concatenate([jnp.cos(ang), jnp.sin(ang)], axis=0)


def _prep_weights(g_ff1, w_ff1_gate, w_ff1_up, w_ff1_down, g_mix, w_in, g_q, w_uq, g_kv, w_uk,
                  w_uv, b_gate_i, b_gate_f, g_attn_out, g_mlstm_out, w_out, g_ff2, w_ff2_gate,
                  w_ff2_up, w_ff2_down, g_ple, w_ple_gate, w_ple_proj, g_final):
    bf = lambda a: a.astype(BF16)
    row = lambda a: a.reshape(1, -1).astype(F32)
    off_kv, off_kr = Q_LORA, Q_LORA + KV_LORA
    off_m = off_kr + ROPE_DIM
    off_i = off_m + 4 * MLSTM_WIDTH
    small_pad = LANES - ROPE_DIM - 2 * M_HEADS
    w_sm = jnp.concatenate([w_in[:, off_kr:off_m], w_in[:, off_i:off_i + 2 * M_HEADS],
                            jnp.zeros((D_MODEL, small_pad), F32)], axis=1)
    gate_bias = jnp.concatenate([jnp.zeros((ROPE_DIM,), F32), b_gate_i, b_gate_f,
                                 jnp.zeros((small_pad,), F32)])
    src = jnp.arange(LANES)[:, None]
    dst = jnp.arange(MLA_PAD_WIDTH)[None, :]
    w_abs = jnp.pad(w_uk.reshape(KV_LORA, MLA_HEADS, NOPE_DIM).transpose(1, 2, 0),
                    ((0, 0), (0, HEAD_PAD - NOPE_DIM), (0, 0))).reshape(MLA_PAD_WIDTH, KV_LORA)
    rope_sel = (dst.T % HEAD_PAD) == (src.T + NOPE_DIM)
    rope_sel = rope_sel & (src.T < ROPE_DIM)
    w_abs = jnp.concatenate([w_abs, rope_sel.astype(F32)], axis=1)
    return dict(
        g_ff1=row(g_ff1), w_ff1_gate=bf(w_ff1_gate), w_ff1_up=bf(w_ff1_up), w_ff1_down=bf(w_ff1_down),
        g_mix=row(g_mix), w_q=bf(w_in[:, :off_kv]), w_kv=bf(w_in[:, off_kv:off_kr]), w_sm=bf(w_sm),
        w_m=bf(w_in[:, off_m:off_i]), g_q=row(g_q), w_uq=bf(_pad_heads(w_uq, NOPE_DIM + ROPE_DIM)),
        g_kv=row(g_kv), w_uk=bf(_pad_heads(w_uk, NOPE_DIM)), w_uv=bf(w_uv),
        w_uv_t=bf(jnp.pad(w_uv.T.reshape(MLA_HEADS, V_DIM, KV_LORA),
                          ((0, 0), (0, V_ROWS - V_DIM), (0, 0))).reshape(MLA_HEADS * V_ROWS, KV_LORA)),
        gate_bias=row(gate_bias), w_abs=bf(w_abs),
        g_attn=row(g_attn_out), w_out_a=bf(w_out[:MLA_WIDTH]),
        w_out_m=bf(w_out[MLA_WIDTH:]), g_mlstm=row(g_mlstm_out),
        g_ff2=row(g_ff2), w_ff2_gate=bf(w_ff2_gate), w_ff2_up=bf(w_ff2_up), w_ff2_down=bf(w_ff2_down),
        g_ple=row(g_ple), w_ple_gate=bf(w_ple_gate), w_ple_proj=bf(w_ple_proj), g_final=row(g_final))


PROMPT_TILE = 512


def kernel(x_prompt, x_sample, p_prompt, p_sample, cache_ckv, cache_krope, state_C, state_n, state_m, page_table, g_ff1, w_ff1_gate, w_ff1_up, w_ff1_down, g_mix, w_in, g_q, w_uq, g_kv, w_uk, w_uv, b_gate_i, b_gate_f, g_attn_out, g_mlstm_out, w_out, g_ff2, w_ff2_gate, w_ff2_up, w_ff2_down, g_ple, w_ple_gate, w_ple_proj, g_final):
    assert w_in.shape[0] == 1, "single-layer trunk"
    w = _prep_weights(g_ff1[0], w_ff1_gate[0], w_ff1_up[0], w_ff1_down[0], g_mix[0], w_in[0],
                      g_q[0], w_uq[0], g_kv[0], w_uk[0], w_uv[0], b_gate_i[0], b_gate_f[0],
                      g_attn_out[0], g_mlstm_out[0], w_out[0], g_ff2[0], w_ff2_gate[0],
                      w_ff2_up[0], w_ff2_down[0], g_ple[0], w_ple_gate[0], w_ple_proj[0], g_final)
    nb_p, seq, _ = x_prompt.shape
    nb_s = x_sample.shape[0]
    t_p = nb_p * seq

    tab_p = _rope_tables(jnp.arange(seq))
    h_p = _ffn_call(x_prompt.reshape(t_p, D_MODEL), w["g_ff1"], w["w_ff1_gate"], w["w_ff1_up"],
                    w["w_ff1_down"], PROMPT_TILE)
    q_p, k_p, vt_p, ckv_p, _, krt_p, mq_p, mk_p, mv_p, mo_p, gates_p = _proj_call(
        h_p, w, tab_p, PROMPT_TILE)
    seq3 = lambda a: a.reshape(nb_p, seq, a.shape[-1])
    a_p = _flash_call(seq3(q_p), seq3(k_p), vt_p)
    hm_p, cx_p, m_p = _mlstm_call(seq3(mq_p), seq3(mk_p), seq3(mv_p), seq3(mo_p),
                                  seq3(gates_p), w["g_mlstm"])
    cx_p = cx_p.reshape(nb_p, M_HEADS, M_DIM, STATE_LANES)
    y_p = _merge_call(h_p, a_p.reshape(t_p, MLA_WIDTH), hm_p.reshape(t_p, MLSTM_WIDTH),
                      p_prompt.reshape(t_p, PLE_DIM), w, PROMPT_TILE)

    tab_s = _rope_tables(jnp.full((nb_s,), PAST_LEN, jnp.int32))
    h_s = _ffn_call(x_sample.reshape(nb_s, D_MODEL), w["g_ff1"], w["w_ff1_gate"], w["w_ff1_up"],
                    w["w_ff1_down"], nb_s)
    q_s, _, _, ckv_s, kr_s, krt_s, mq_s, mk_s, mv_s, mo_s, gates_s = _proj_call(h_s, w, tab_s, nb_s)
    n_phys = cache_ckv.shape[1]
    a_s = _sample_attn_call(
        page_table, q_s.reshape(nb_s, 1, MLA_PAD_WIDTH), ckv_s.reshape(nb_s, 1, KV_LORA),
        kr_s.reshape(nb_s, 1, ROPE_DIM), cache_ckv.reshape(n_phys, PAGE_SIZE, KV_LORA),
        jnp.swapaxes(cache_krope.reshape(n_phys, PAGE_SIZE, ROPE_DIM), 1, 2), w["w_abs"], w["w_uv"])
    hm_s, c_s, n_s, m_s = _sample_mlstm_call(
        mq_s, mk_s, mv_s, mo_s, gates_s, w["g_mlstm"], state_C[0].astype(F32),
        state_n[0].astype(F32).reshape(nb_s, MLSTM_WIDTH), state_m[0].astype(F32))
    y_s = _merge_call(h_s, a_s.reshape(nb_s, MLA_WIDTH), hm_s, p_sample.reshape(nb_s, PLE_DIM),
                      w, nb_s)

    return (y_p.reshape(nb_p, seq, D_MODEL), y_s.reshape(nb_s, 1, D_MODEL),
            ckv_p.reshape(1, nb_p, seq, KV_LORA), jnp.swapaxes(krt_p, 1, 2)[None],
            cx_p[None, ..., :M_DIM], cx_p[None, ..., M_DIM], m_p[:, 0, 0].reshape(1, nb_p, M_HEADS),
            ckv_s.reshape(1, nb_s, 1, KV_LORA), jnp.swapaxes(krt_s, 1, 2).reshape(1, nb_s, 1, ROPE_DIM),
            c_s[None], n_s.reshape(1, nb_s, M_HEADS, M_DIM), m_s[None])
```

```python
import functools
import math

import jax
import jax.numpy as jnp
from jax import lax
from jax.experimental import pallas as pl
from jax.experimental.pallas import tpu as pltpu

F32 = jnp.float32
BF16 = jnp.bfloat16

D_MODEL = 1024
SEQ = 8192
DEC_BATCH = 128
PAST_LEN = 8192
PAGE_SIZE = 128
N_PAGES = PAST_LEN // PAGE_SIZE
MLA_HEADS = 8
Q_LORA = 384
KV_LORA = 256
NOPE_DIM = 64
ROPE_DIM = 32
ROPE_HALF = ROPE_DIM // 2
V_DIM = 64
ROPE_THETA = 10000.0
M_HEADS = 4
M_DIM = 128
CHUNK = 128
MLSTM_WIDTH = M_HEADS * M_DIM
D_FF = 2816
PLE_DIM = 256
EPS = 1e-6

LANES = 128
SUBLANES = 8

HEAD_PAD = LANES
MLA_PAD_WIDTH = MLA_HEADS * HEAD_PAD
MLA_WIDTH = MLA_HEADS * V_DIM
V_ROWS = 80
GATE_I_LANE = ROPE_DIM
GATE_F_LANE = ROPE_DIM + M_HEADS
QK_SCALE = (NOPE_DIM + ROPE_DIM) ** -0.5 * math.log2(math.e)

VMEM_LIMIT = 56 * 1024 * 1024


def _dot(a, b):
    return jnp.dot(a, b, preferred_element_type=F32)


def _dot_nt(a, b):
    return lax.dot_general(a, b, (((1,), (1,)), ((), ())), preferred_element_type=F32)


def _dot_tn(a, b):
    return lax.dot_general(a, b, (((0,), (0,)), ((), ())), preferred_element_type=F32)


def _rms(x, g):
    ms = jnp.sum(x * x, axis=-1, keepdims=True) * (1.0 / x.shape[-1])
    return x * lax.rsqrt(ms + EPS) * g


def _resident(shape):
    return pl.BlockSpec(shape, lambda *_: (0,) * len(shape), pipeline_mode=pl.Buffered(1))


def _params(n_axes):
    return pltpu.CompilerParams(dimension_semantics=("arbitrary",) * n_axes,
                                vmem_limit_bytes=VMEM_LIMIT)


MXU_TILE = 256
FF_CHUNK_BOUNDS = (0, 6 * MXU_TILE, D_FF)
assert D_FF % MXU_TILE == 0


def _swiglu_half(xn, wg_ref, wu_ref, wd_ref):
    out = None
    for lo, hi in zip(FF_CHUNK_BOUNDS[:-1], FF_CHUNK_BOUNDS[1:]):
        cols = slice(lo, hi)
        gate = _dot(xn, wg_ref[:, cols])
        up = _dot(xn, wu_ref[:, cols])
        act = (jax.nn.silu(gate) * up).astype(BF16)
        part = _dot(act, wd_ref[cols, :])
        out = part if out is None else out + part
    return 0.5 * out


def _ffn_body(x_ref, g_ref, wg_ref, wu_ref, wd_ref, o_ref):
    x = x_ref[...]
    xn = _rms(x, g_ref[...]).astype(BF16)
    o_ref[...] = x + _swiglu_half(xn, wg_ref, wu_ref, wd_ref)


def _ffn_call(x, g, wg, wu, wd, tm):
    t = x.shape[0]
    row = lambda i: (i, 0)
    return pl.pallas_call(
        _ffn_body,
        grid=(t // tm,),
        in_specs=[pl.BlockSpec((tm, D_MODEL), row), _resident((1, D_MODEL)),
                  _resident((D_MODEL, D_FF)), _resident((D_MODEL, D_FF)), _resident((D_FF, D_MODEL))],
        out_specs=pl.BlockSpec((tm, D_MODEL), row),
        out_shape=jax.ShapeDtypeStruct((t, D_MODEL), F32),
        compiler_params=_params(1),
        name="ffn1",
    )(x, g, wg, wu, wd)


def _rope128(x, cos_tab, sin_tab, x2_start):
    lane = lax.broadcasted_iota(jnp.int32, x.shape, 1)
    partner = jnp.where(lane < x2_start, pltpu.roll(x, LANES - ROPE_HALF, 1),
                        pltpu.roll(x, ROPE_HALF, 1))
    return x * cos_tab + partner * sin_tab


CELL_V_ROWS = M_DIM + 16


def _proj_body(cell_feature_major, h_ref, gmix_ref, wq_ref, wkv_ref, wsm_ref, gq_ref, wuq_ref,
               gkv_ref, wuk_ref, wuvt_ref, bias_ref, tab_ref, *refs):
    if cell_feature_major:
        wmk_ref, wqvot_ref = refs[:2]
        q_out, k_out, vt_out, ckv_out, kr_out, krt_out, mk_out, mqt_out, mvt_out, mot_out, gt_out = refs[2:]
    else:
        wm_ref = refs[0]
        q_out, k_out, vt_out, ckv_out, kr_out, krt_out, mq_out, mk_out, mv_out, mo_out, gate_out = refs[1:]
    u = _rms(h_ref[...], gmix_ref[...]).astype(BF16)
    ckv = _rms(_dot(u, wkv_ref[...]), gkv_ref[...])
    ckv_out[...] = ckv
    ckv_b = ckv.astype(BF16)
    zs = _dot(u, wsm_ref[...])
    n_tok = zs.shape[0]
    cs = jnp.concatenate([tab_ref[...], jnp.zeros((LANES - ROPE_DIM, n_tok), F32)], axis=0).T
    lane = lax.broadcasted_iota(jnp.int32, zs.shape, 1)
    shifted = lambda by: pltpu.roll(cs, by, 1)
    cos_k = jnp.where(lane < ROPE_HALF, cs, jnp.where(lane < ROPE_DIM, shifted(ROPE_HALF), 0.0))
    sin_k = jnp.where(lane < ROPE_HALF, -shifted(LANES - ROPE_HALF), jnp.where(lane < ROPE_DIM, cs, 0.0))
    kr = _rope128(zs, cos_k, sin_k, ROPE_HALF)
    kr_out[...] = kr[:, :ROPE_DIM]
    krt_out[0] = kr.T[:ROPE_DIM, :]
    kn = _dot(ckv_b, wuk_ref[...])
    kr_head = pltpu.roll(kr, NOPE_DIM, 1)
    for hd in range(MLA_HEADS):
        lanes = slice(hd * HEAD_PAD, (hd + 1) * HEAD_PAD)
        k_out[:, lanes] = (kn[:, lanes] + kr_head).astype(BF16)
    x1 = (lane >= NOPE_DIM) & (lane < NOPE_DIM + ROPE_HALF)
    x2 = (lane >= NOPE_DIM + ROPE_HALF) & (lane < NOPE_DIM + ROPE_DIM)
    cos_q = jnp.where(lane < NOPE_DIM, 1.0, jnp.where(x1, shifted(NOPE_DIM), jnp.where(
        x2, shifted(NOPE_DIM + ROPE_HALF), 0.0)))
    sin_q = jnp.where(x1, -shifted(NOPE_DIM - ROPE_HALF), jnp.where(x2, shifted(NOPE_DIM), 0.0))
    vt = _dot_nt(wuvt_ref[...], ckv_b)
    vrow = lax.broadcasted_iota(jnp.int32, vt.shape, 0)
    vt_out[0, 0] = jnp.where(vrow % V_ROWS == V_DIM, 1.0, vt).astype(BF16)
    qn = _rms(_dot(u, wq_ref[...]), gq_ref[...]).astype(BF16)
    q = _dot(qn, wuq_ref[...])
    cos_qs = cos_q * QK_SCALE
    sin_qs = sin_q * QK_SCALE
    for hd in range(MLA_HEADS):
        lanes = slice(hd * HEAD_PAD, (hd + 1) * HEAD_PAD)
        q_out[:, lanes] = _rope128(q[:, lanes], cos_qs, sin_qs, NOPE_DIM + ROPE_HALF).astype(BF16)
    zb = zs + bias_ref[...]
    log_sig = jnp.minimum(zb, 0.0) - jnp.log1p(jnp.exp(-jnp.abs(zb)))
    is_f = (lane >= GATE_F_LANE) & (lane < GATE_F_LANE + M_HEADS)
    gates = jnp.where(is_f, log_sig, zb)
    k_scale = M_DIM ** -0.5
    if cell_feature_major:
        mk_out[...] = (_dot(u, wmk_ref[...]) * k_scale).astype(BF16)
        zt = _dot_nt(wqvot_ref[...], u)
        mqt_out[0, 0] = zt[0:MLSTM_WIDTH].astype(BF16)
        unit_rows = (lax.broadcasted_iota(jnp.int32, (CELL_V_ROWS - M_DIM, n_tok), 0) == 0)
        for hd in range(M_HEADS):
            rows = slice(MLSTM_WIDTH + hd * M_DIM, MLSTM_WIDTH + (hd + 1) * M_DIM)
            mvt_out[0, 0, hd * CELL_V_ROWS:hd * CELL_V_ROWS + M_DIM, :] = zt[rows].astype(BF16)
            mvt_out[0, 0, hd * CELL_V_ROWS + M_DIM:(hd + 1) * CELL_V_ROWS, :] = (
                unit_rows.astype(F32).astype(BF16))
        mot_out[0, 0] = zt[2 * MLSTM_WIDTH:3 * MLSTM_WIDTH]
        gt_out[0, 0] = gates.T[GATE_I_LANE:GATE_I_LANE + 2 * M_HEADS, :]
    else:
        zm = _dot(u, wm_ref[...])
        mq_out[...] = zm[:, 0:MLSTM_WIDTH].astype(BF16)
        mk_out[...] = (zm[:, MLSTM_WIDTH:2 * MLSTM_WIDTH] * k_scale).astype(BF16)
        mv_out[...] = zm[:, 2 * MLSTM_WIDTH:3 * MLSTM_WIDTH].astype(BF16)
        mo_out[...] = zm[:, 3 * MLSTM_WIDTH:4 * MLSTM_WIDTH]
        gate_out[...] = gates


def _proj_call(h, w, rope_tab, tm, cell_feature_major):
    t = h.shape[0]
    n_tab = rope_tab.shape[1] // tm
    n_seq = t // tm // n_tab
    row = lambda i: (i, 0)
    tab = lambda i: (0, i % n_tab)
    tok = lambda width, dtype: (jax.ShapeDtypeStruct((t, width), dtype),
                                pl.BlockSpec((tm, width), row))
    slab = lambda rows, dtype: (jax.ShapeDtypeStruct((n_seq, n_tab, rows, tm), dtype),
                                pl.BlockSpec((1, 1, rows, tm), lambda i: (i // n_tab, i % n_tab, 0, 0)))
    outs = [tok(MLA_PAD_WIDTH, BF16), tok(MLA_PAD_WIDTH, BF16), slab(MLA_HEADS * V_ROWS, BF16),
            tok(KV_LORA, F32), tok(ROPE_DIM, F32),
            (jax.ShapeDtypeStruct((n_seq, ROPE_DIM, n_tab * tm), F32),
             pl.BlockSpec((1, ROPE_DIM, tm), lambda i: (i // n_tab, 0, i % n_tab)))]
    if cell_feature_major:
        cell_w = [w["w_mk"], w["w_qvo_t"]]
        outs += [tok(MLSTM_WIDTH, BF16), slab(MLSTM_WIDTH, BF16), slab(M_HEADS * CELL_V_ROWS, BF16),
                 slab(MLSTM_WIDTH, F32), slab(2 * M_HEADS, F32)]
    else:
        cell_w = [w["w_m"]]
        outs += [tok(MLSTM_WIDTH, BF16)] * 3 + [tok(MLSTM_WIDTH, F32), tok(LANES, F32)]
    return pl.pallas_call(
        functools.partial(_proj_body, cell_feature_major),
        grid=(t // tm,),
        in_specs=[pl.BlockSpec((tm, D_MODEL), row), _resident((1, D_MODEL)),
                  _resident((D_MODEL, Q_LORA)), _resident((D_MODEL, KV_LORA)),
                  _resident((D_MODEL, LANES)),
                  _resident((1, Q_LORA)), _resident((Q_LORA, MLA_PAD_WIDTH)),
                  _resident((1, KV_LORA)), _resident((KV_LORA, MLA_PAD_WIDTH)),
                  _resident((MLA_HEADS * V_ROWS, KV_LORA)),
                  _resident((1, LANES)),
                  pl.BlockSpec((ROPE_DIM, tm), tab)] + [_resident(cw.shape) for cw in cell_w],
        out_specs=[o[1] for o in outs],
        out_shape=[o[0] for o in outs],
        compiler_params=_params(1),
        name="proj",
    )(h, w["g_mix"], w["w_q"], w["w_kv"], w["w_sm"], w["g_q"], w["w_uq"], w["g_kv"],
      w["w_uk"], w["w_uv_t"], w["gate_bias"], rope_tab, *cell_w)


ATT_BLOCK = 512
ATT_HEADS = 8
ATT_LANES = ATT_HEADS * HEAD_PAD
ATT_V_ROWS = ATT_HEADS * V_ROWS
ATT_LOOKAHEAD = 3


def _flash_body(q_ref, k_ref, vt_ref, o_ref):
    qi = pl.program_id(2)
    blk = ATT_BLOCK
    head_lanes = [slice(hd * HEAD_PAD, (hd + 1) * HEAD_PAD) for hd in range(ATT_HEADS)]
    qs = [q_ref[0, :, lanes] for lanes in head_lanes]

    def step(j, carry, diagonal):
        start = pl.multiple_of(j * blk, blk)
        scores = [None] * ATT_HEADS
        if diagonal:
            key = lax.broadcasted_iota(jnp.int32, (blk, blk), 0)
            qry = lax.broadcasted_iota(jnp.int32, (blk, blk), 1)
            visible = key <= qry
        out = []
        for hd in range(ATT_HEADS):
            m, acc = carry[hd]
            for nxt in range(hd if hd else 0, min(hd + ATT_LOOKAHEAD, ATT_HEADS - 1) + 1):
                if scores[nxt] is None:
                    scores[nxt] = _dot_nt(k_ref[0, pl.ds(start, blk), head_lanes[nxt]], qs[nxt])
            s = jnp.where(visible, scores[hd], -jnp.inf) if diagonal else scores[hd]
            m_new = jnp.maximum(m, jnp.max(s, axis=0, keepdims=True))
            p = jnp.exp2(s - m_new).astype(BF16)
            vt = vt_ref[0, j, hd * V_ROWS:(hd + 1) * V_ROWS, :]
            acc = jnp.exp2(m - m_new) * acc + _dot(vt, p)
            out.append((m_new, acc))
        return tuple(out)

    init = tuple((jnp.full((1, blk), -jnp.inf, F32), jnp.zeros((V_ROWS, blk), F32))
                 for _ in range(ATT_HEADS))
    carry = lax.fori_loop(0, qi, lambda j, c: step(j, c, False), init)
    final = step(qi, carry, True)
    for pair in range(ATT_HEADS // 2):
        o_t = jnp.concatenate([final[hd][1][:V_DIM] / final[hd][1][V_DIM:V_DIM + 1]
                               for hd in (2 * pair, 2 * pair + 1)], axis=0)
        o_ref[0, :, pair * LANES:(pair + 1) * LANES] = o_t.T.astype(BF16)


def _flash_call(q, k, vt):
    b, s, _ = q.shape
    assert vt.shape == (b, s // ATT_BLOCK, MLA_HEADS * V_ROWS, ATT_BLOCK)
    qmap = lambda bi, gi, qi: (bi, qi, gi)
    return pl.pallas_call(
        _flash_body,
        grid=(b, MLA_HEADS // ATT_HEADS, s // ATT_BLOCK),
        in_specs=[pl.BlockSpec((1, ATT_BLOCK, ATT_LANES), qmap),
                  pl.BlockSpec((1, s, ATT_LANES), lambda bi, gi, qi: (bi, 0, gi),
                               pipeline_mode=pl.Buffered(1)),
                  pl.BlockSpec((1, s // ATT_BLOCK, ATT_V_ROWS, ATT_BLOCK),
                               lambda bi, gi, qi: (bi, 0, gi, 0), pipeline_mode=pl.Buffered(1))],
        out_specs=pl.BlockSpec((1, ATT_BLOCK, ATT_HEADS * V_DIM), qmap),
        out_shape=jax.ShapeDtypeStruct((b, s, MLA_WIDTH), BF16),
        compiler_params=_params(3),
        name="prompt_attn",
    )(q, k, vt)


def _lane_scan(x, op, fill):
    lane = lax.broadcasted_iota(jnp.int32, x.shape, 1)
    shift = 1
    while shift < x.shape[1]:
        x = op(x, jnp.where(lane >= shift, pltpu.roll(x, shift, 1), fill))
        shift *= 2
    return x


def _mlstm_body(k_ref, qt_ref, vt_ref, ot_ref, gt_ref, gm_ref, hm_out, cxt_out, m_out, cxt_s, m_s):
    ci = pl.program_id(0)
    n_seq = k_ref.shape[0]
    L = CHUNK

    @pl.when(ci == 0)
    def _():
        cxt_s[...] = jnp.zeros_like(cxt_s)
        m_s[...] = jnp.zeros_like(m_s)

    s_idx = lax.broadcasted_iota(jnp.int32, (L, L), 0)
    t_idx = lax.broadcasted_iota(jnp.int32, (L, L), 1)
    causal = s_idx <= t_idx

    streams = []
    for sq in range(n_seq):
        rows = gt_ref[sq, 0]
        b_rows = pltpu.roll(_lane_scan(rows, jnp.add, 0.0), M_HEADS, 0)
        head_row = lax.broadcasted_iota(jnp.int32, rows.shape, 0) < M_HEADS
        a_rows = jnp.where(head_row, rows - b_rows, 0.0)
        run_max = _lane_scan(a_rows, jnp.maximum, -jnp.inf)
        top = jnp.broadcast_to(jnp.max(a_rows, axis=1, keepdims=True), rows.shape)
        b_last = pltpu.roll(jnp.broadcast_to(jnp.sum(rows, axis=1, keepdims=True), rows.shape),
                            M_HEADS, 0)
        src = jnp.exp(a_rows - top)
        a_cols = jnp.concatenate([a_rows, jnp.zeros((L - SUBLANES, L), F32)], axis=0).T
        for hd in range(M_HEADS):
            feat = slice(hd * M_DIM, (hd + 1) * M_DIM)
            idx = sq * M_HEADS + hd
            k = k_ref[sq, :, feat]
            qt = qt_ref[sq, 0, feat, :]
            vt = vt_ref[sq, 0, hd * CELL_V_ROWS:(hd + 1) * CELL_V_ROWS, :]
            m_prev = m_s[idx][0:1, :]
            big_m = jnp.maximum(m_prev, run_max[hd:hd + 1, :])
            decay = jnp.where(causal, jnp.exp(a_cols[:, hd:hd + 1] - big_m), 0.0)
            cxt_prev = cxt_s[idx]
            weighted_vt = (vt.astype(F32) * src[hd:hd + 1, :]).astype(BF16)
            streams.append(dict(
                sq=sq, feat=feat, idx=idx, m_prev=m_prev, cxt_prev=cxt_prev, vt=vt,
                w_inter=jnp.exp(m_prev - big_m),
                floor=jnp.exp(-(b_rows[hd:hd + 1, :] + big_m)),
                top=top[hd:hd + 1, :], b_last=b_last[hd:hd + 1, :],
                sqk=_dot(k, qt) * decay,
                qe=_dot(cxt_prev.astype(BF16), qt),
                kv=_dot(weighted_vt, k)))

    for st in streams:
        ue = _dot(st["vt"], st["sqk"].astype(BF16))
        num = ue[:M_DIM] + st["qe"][:M_DIM] * st["w_inter"]
        den = ue[M_DIM:M_DIM + 1] + st["qe"][M_DIM:M_DIM + 1] * st["w_inter"]
        inv = 1.0 / jnp.maximum(jnp.abs(den), st["floor"])
        t = jax.nn.sigmoid(ot_ref[st["sq"], 0, st["feat"], :]) * num
        ms = jnp.sum(t * t, axis=0, keepdims=True) * (1.0 / M_DIM)
        scale = inv * lax.rsqrt(ms * inv * inv + EPS)
        hm_out[st["sq"], :, st["feat"]] = (t * scale * gm_ref[st["feat"], :]).T.astype(BF16)
        m_last = jnp.maximum(st["m_prev"], st["top"])
        keep = jnp.exp(st["m_prev"] - m_last)
        gain = jnp.exp(st["top"] - m_last)
        cxt_s[st["idx"]] = keep * st["cxt_prev"] + gain * st["kv"]
        m_s[st["idx"]] = jnp.broadcast_to(st["b_last"] + m_last, (SUBLANES, LANES))

    @pl.when(ci == pl.num_programs(0) - 1)
    def _():
        cxt_out[...] = cxt_s[...]
        m_out[...] = m_s[...]


def _mlstm_call(mk, mqt, mvt, mot, gt, gm_lanes):
    b, s, _ = mk.shape
    tile = mqt.shape[-1]
    per_tile = tile // CHUNK
    n_streams = b * M_HEADS
    slab = lambda rows: pl.BlockSpec((b, 1, rows, CHUNK),
                                     lambda ci: (0, ci // per_tile, 0, ci % per_tile))
    tok = pl.BlockSpec((b, CHUNK, MLSTM_WIDTH), lambda ci: (0, ci, 0))
    whole3 = lambda ci: (0, 0, 0)
    return pl.pallas_call(
        _mlstm_body,
        grid=(s // CHUNK,),
        in_specs=[tok, slab(MLSTM_WIDTH), slab(M_HEADS * CELL_V_ROWS), slab(MLSTM_WIDTH),
                  slab(2 * M_HEADS), _resident((MLSTM_WIDTH, LANES))],
        out_specs=[tok,
                   pl.BlockSpec((n_streams, CELL_V_ROWS, M_DIM), whole3),
                   pl.BlockSpec((n_streams, SUBLANES, LANES), whole3)],
        out_shape=[jax.ShapeDtypeStruct((b, s, MLSTM_WIDTH), BF16),
                   jax.ShapeDtypeStruct((n_streams, CELL_V_ROWS, M_DIM), F32),
                   jax.ShapeDtypeStruct((n_streams, SUBLANES, LANES), F32)],
        scratch_shapes=[pltpu.VMEM((n_streams, CELL_V_ROWS, M_DIM), F32),
                        pltpu.VMEM((n_streams, SUBLANES, LANES), F32)],
        compiler_params=_params(1),
        name="prompt_mlstm",
    )(mk, mqt, mvt, mot, gt, gm_lanes)


def _merge_body(h_ref, a_ref, hm_ref, p_ref, ga_ref, woa_ref, wom_ref, gff_ref, wg_ref, wu_ref,
                wd_ref, gple_ref, wpg_ref, wpp_ref, gfin_ref, y_ref):
    a = a_ref[...].astype(F32)
    an = _rms(a, ga_ref[...]).astype(BF16)
    h = h_ref[...] + _dot(an, woa_ref[...]) + _dot(hm_ref[...], wom_ref[...])
    h = h + _swiglu_half(_rms(h, gff_ref[...]).astype(BF16), wg_ref, wu_ref, wd_ref)
    gate = jax.nn.sigmoid(_dot(_rms(h, gple_ref[...]).astype(BF16), wpg_ref[...]))
    h = h + gate * _dot(p_ref[...].astype(BF16), wpp_ref[...])
    y_ref[...] = _rms(h, gfin_ref[...])


def _merge_call(h, a, hm, p, w, tm):
    t = h.shape[0]
    row = lambda i: (i, 0)
    return pl.pallas_call(
        _merge_body,
        grid=(t // tm,),
        in_specs=[pl.BlockSpec((tm, D_MODEL), row), pl.BlockSpec((tm, MLA_WIDTH), row),
                  pl.BlockSpec((tm, MLSTM_WIDTH), row), pl.BlockSpec((tm, PLE_DIM), row),
                  _resident((1, MLA_WIDTH)), _resident((MLA_WIDTH, D_MODEL)),
                  _resident((MLSTM_WIDTH, D_MODEL)), _resident((1, D_MODEL)),
                  _resident((D_MODEL, D_FF)), _resident((D_MODEL, D_FF)), _resident((D_FF, D_MODEL)),
                  _resident((1, D_MODEL)), _resident((D_MODEL, D_MODEL)),
                  _resident((PLE_DIM, D_MODEL)), _resident((1, D_MODEL))],
        out_specs=pl.BlockSpec((tm, D_MODEL), row),
        out_shape=jax.ShapeDtypeStruct((t, D_MODEL), F32),
        compiler_params=_params(1),
        name="merge",
    )(h, a, hm, p, w["g_attn"], w["w_out_a"], w["w_out_m"], w["g_ff2"], w["w_ff2_gate"],
      w["w_ff2_up"], w["w_ff2_down"], w["g_ple"], w["w_ple_gate"], w["w_ple_proj"], w["g_final"])


N_SLOTS = 2
SAMPLE_CHUNKS = 4
SAMPLE_CHUNK_PAGES = N_PAGES // SAMPLE_CHUNKS
SAMPLE_CHUNK_KEYS = SAMPLE_CHUNK_PAGES * PAGE_SIZE


def _sample_attn_body(pt_ref, q_ref, ckv_ref, kr_ref, cache_c, cache_r, wabs_ref, wuv_ref,
                      o_ref, cbuf, rbuf, sem_c, sem_r):
    b = pl.program_id(0)
    nb = pl.num_programs(0)
    slot = b % N_SLOTS

    def copies(bi, sl):
        out = []
        for j in range(N_PAGES):
            page = pt_ref[bi, j]
            out.append(pltpu.make_async_copy(cache_c.at[page], cbuf.at[sl, j], sem_c.at[sl]))
            out.append(pltpu.make_async_copy(
                cache_r.at[page], rbuf.at[sl, :, pl.ds(j * PAGE_SIZE, PAGE_SIZE)], sem_r.at[sl]))
        return out

    @pl.when(b == 0)
    def _():
        for bi in range(min(N_SLOTS, DEC_BATCH)):
            for cp in copies(bi, bi):
                cp.start()

    q_row = q_ref[0]
    sub = lax.broadcasted_iota(jnp.int32, (MLA_HEADS, MLA_PAD_WIDTH), 0)
    lane = lax.broadcasted_iota(jnp.int32, (MLA_HEADS, MLA_PAD_WIDTH), 1)
    own = (lane // HEAD_PAD) == sub
    q_bd = jnp.where(own, jnp.broadcast_to(q_row.astype(F32), own.shape), 0.0).astype(BF16)
    q_ext = _dot(q_bd, wabs_ref[...])
    q_abs = q_ext[:, :KV_LORA].astype(BF16)
    q_rope = q_ext[:, KV_LORA:KV_LORA + ROPE_DIM].astype(BF16)

    c_new = ckv_ref[0].astype(BF16).astype(F32)
    r_new = kr_ref[0].astype(BF16).astype(F32)
    s_new = (jnp.sum(q_abs.astype(F32) * c_new, axis=1, keepdims=True)
             + jnp.sum(q_rope.astype(F32) * r_new, axis=1, keepdims=True))

    for cp in copies(b, slot):
        cp.wait()

    def chunk_keys(i):
        pages = slice(i * SAMPLE_CHUNK_PAGES, (i + 1) * SAMPLE_CHUNK_PAGES)
        keys = slice(i * SAMPLE_CHUNK_KEYS, (i + 1) * SAMPLE_CHUNK_KEYS)
        kc = cbuf[slot, pages].reshape(SAMPLE_CHUNK_KEYS, KV_LORA).astype(BF16)
        return kc, _dot_nt(q_abs, kc) + _dot(q_rope, rbuf[slot, :, keys].astype(BF16))

    m = s_new
    l = jnp.ones_like(s_new)
    acc = jnp.broadcast_to(c_new, (MLA_HEADS, KV_LORA))
    kc, s = chunk_keys(0)
    for i in range(SAMPLE_CHUNKS):
        nxt = chunk_keys(i + 1) if i + 1 < SAMPLE_CHUNKS else None
        m_new = jnp.maximum(m, jnp.max(s, axis=1, keepdims=True))
        alpha = jnp.exp2(m - m_new)
        p = jnp.exp2(s - m_new)
        l = alpha * l + jnp.sum(p, axis=1, keepdims=True)
        acc = alpha * acc + _dot(p.astype(BF16), kc)
        m = m_new
        if nxt is not None:
            kc, s = nxt
    o_lat = (acc / l).astype(BF16)
    res = _dot(o_lat, wuv_ref[...])
    own_v = (lax.broadcasted_iota(jnp.int32, res.shape, 1) // V_DIM
             == lax.broadcasted_iota(jnp.int32, res.shape, 0))
    o_ref[0] = jnp.sum(jnp.where(own_v, res, 0.0), axis=0, keepdims=True).astype(BF16)

    @pl.when(b + N_SLOTS < nb)
    def _():
        for cp in copies(b + N_SLOTS, slot):
            cp.start()


def _sample_attn_call(page_table, q, ckv, kr, cache_c, cache_r, w_abs, w_uv):
    nb = q.shape[0]
    tok = lambda bi, pt: (bi, 0, 0)
    whole = lambda shape: pl.BlockSpec(shape, lambda bi, pt: (0,) * len(shape),
                                       pipeline_mode=pl.Buffered(1))
    grid_spec = pltpu.PrefetchScalarGridSpec(
        num_scalar_prefetch=1,
        grid=(nb,),
        in_specs=[pl.BlockSpec((1, 1, MLA_PAD_WIDTH), tok), pl.BlockSpec((1, 1, KV_LORA), tok),
                  pl.BlockSpec((1, 1, ROPE_DIM), tok),
                  pl.BlockSpec(memory_space=pl.ANY), pl.BlockSpec(memory_space=pl.ANY),
                  whole((MLA_PAD_WIDTH, KV_LORA + LANES)), whole((KV_LORA, MLA_WIDTH))],
        out_specs=pl.BlockSpec((1, 1, MLA_WIDTH), tok),
        scratch_shapes=[pltpu.VMEM((N_SLOTS, N_PAGES, PAGE_SIZE, KV_LORA), F32),
                        pltpu.VMEM((N_SLOTS, ROPE_DIM, PAST_LEN), F32),
                        pltpu.SemaphoreType.DMA((N_SLOTS,)), pltpu.SemaphoreType.DMA((N_SLOTS,))])
    return pl.pallas_call(
        _sample_attn_body,
        grid_spec=grid_spec,
        out_shape=jax.ShapeDtypeStruct((nb, 1, MLA_WIDTH), BF16),
        compiler_params=_params(1),
        name="sample_attn",
    )(page_table, q, ckv, kr, cache_c, cache_r, w_abs, w_uv)


SAMPLE_ROWS = 32


def _sample_mlstm_body(q_ref, k_ref, v_ref, o_ref, g_ref, gm_ref, c_ref, n_ref, m_ref,
                       hm_out, c_out, n_out, m_out):
    hd = pl.program_id(1)
    g = g_ref[...]
    lane = lax.broadcasted_iota(jnp.int32, g.shape, 1)
    ig = jnp.sum(jnp.where(lane == GATE_I_LANE + hd, g, 0.0), axis=1, keepdims=True)
    lf = jnp.sum(jnp.where(lane == GATE_F_LANE + hd, g, 0.0), axis=1, keepdims=True)
    m_all = m_ref[...]
    head_lane = lax.broadcasted_iota(jnp.int32, m_all.shape, 1)
    m0 = jnp.sum(jnp.where(head_lane == hd, m_all, 0.0), axis=1, keepdims=True)
    q = q_ref[...].astype(F32)
    k = k_ref[...].astype(F32)
    v = v_ref[...].astype(F32)
    n0 = n_ref[...]
    m_new = jnp.maximum(lf + m0, ig)
    keep = jnp.exp(lf + m0 - m_new)
    w_src = jnp.exp(ig - m_new)
    sqk = jnp.sum(q * k, axis=1, keepdims=True) * w_src
    wk = w_src * k
    qc = jnp.zeros_like(v)
    for d in range(M_DIM):
        c_d = c_ref[:, 0, d, :]
        qc = qc + q[:, d:d + 1] * c_d
        c_out[:, 0, d, :] = keep * c_d + wk[:, d:d + 1] * v
    num = sqk * v + qc * keep
    den = sqk + jnp.sum(q * n0, axis=1, keepdims=True) * keep
    den = jnp.maximum(jnp.abs(den), jnp.exp(-m_new))
    hh = jax.nn.sigmoid(o_ref[...]) * (num / den)
    hm_out[...] = _rms(hh, gm_ref[...]).astype(BF16)
    n_out[...] = keep * n0 + wk

    @pl.when(hd == 0)
    def _():
        m_out[...] = jnp.broadcast_to(m_new, m_all.shape)

    @pl.when(hd > 0)
    def _():
        m_out[...] = jnp.where(head_lane == hd, m_new, m_out[...])


def _sample_mlstm_call(mq, mk, mv, mo, gates, gm, c0, n0, m0):
    nb = mq.shape[0]
    rows = SAMPLE_ROWS
    tok = lambda bi, hi: (bi, hi)
    return pl.pallas_call(
        _sample_mlstm_body,
        grid=(nb // rows, M_HEADS),
        in_specs=[pl.BlockSpec((rows, M_DIM), tok)] * 4
                 + [pl.BlockSpec((rows, LANES), lambda bi, hi: (bi, 0)),
                    pl.BlockSpec((1, M_DIM), lambda bi, hi: (0, hi)),
                    pl.BlockSpec((rows, 1, M_DIM, M_DIM), lambda bi, hi: (bi, hi, 0, 0)),
                    pl.BlockSpec((rows, M_DIM), tok),
                    pl.BlockSpec((rows, M_HEADS), lambda bi, hi: (bi, 0))],
        out_specs=[pl.BlockSpec((rows, M_DIM), tok),
                   pl.BlockSpec((rows, 1, M_DIM, M_DIM), lambda bi, hi: (bi, hi, 0, 0)),
                   pl.BlockSpec((rows, M_DIM), tok),
                   pl.BlockSpec((rows, M_HEADS), lambda bi, hi: (bi, 0))],
        out_shape=[jax.ShapeDtypeStruct((nb, MLSTM_WIDTH), BF16),
                   jax.ShapeDtypeStruct((nb, M_HEADS, M_DIM, M_DIM), F32),
                   jax.ShapeDtypeStruct((nb, MLSTM_WIDTH), F32),
                   jax.ShapeDtypeStruct((nb, M_HEADS), F32)],
        compiler_params=_params(2),
        name="sample_mlstm",
    )(mq, mk, mv, mo, gates, gm, c0, n0, m0)


def _pad_heads(w, head_dim):
    rows = w.shape[0]
    w = w.reshape(rows, MLA_HEADS, head_dim)
    w = jnp.pad(w, ((0, 0), (0, 0), (0, HEAD_PAD - head_dim)))
    return w.reshape(rows, MLA_PAD_WIDTH)


def _rope_tables(pos):
    inv = ROPE_THETA ** (-jnp.arange(ROPE_HALF, dtype=F32) / ROPE_HALF)
    ang = inv[:, None] * pos.astype(F32)[None, :]
    return jnp.concatenate([jnp.cos(ang), jnp.sin(ang)], axis=0)


def _prep_weights(g_ff1, w_ff1_gate, w_ff1_up, w_ff1_down, g_mix, w_in, g_q, w_uq, g_kv, w_uk,
                  w_uv, b_gate_i, b_gate_f, g_attn_out, g_mlstm_out, w_out, g_ff2, w_ff2_gate,
                  w_ff2_up, w_ff2_down, g_ple, w_ple_gate, w_ple_proj, g_final):
    bf = lambda a: a.astype(BF16)
    row = lambda a: a.reshape(1, -1).astype(F32)
    off_kv, off_kr = Q_LORA, Q_LORA + KV_LORA
    off_m = off_kr + ROPE_DIM
    off_i = off_m + 4 * MLSTM_WIDTH
    small_pad = LANES - ROPE_DIM - 2 * M_HEADS
    w_sm = jnp.concatenate([w_in[:, off_kr:off_m], w_in[:, off_i:off_i + 2 * M_HEADS],
                            jnp.zeros((D_MODEL, small_pad), F32)], axis=1)
    gate_bias = jnp.concatenate([jnp.zeros((ROPE_DIM,), F32), b_gate_i, b_gate_f,
                                 jnp.zeros((small_pad,), F32)])
    src = jnp.arange(LANES)[:, None]
    dst = jnp.arange(MLA_PAD_WIDTH)[None, :]
    w_abs = jnp.pad(w_uk.reshape(KV_LORA, MLA_HEADS, NOPE_DIM).transpose(1, 2, 0),
                    ((0, 0), (0, HEAD_PAD - NOPE_DIM), (0, 0))).reshape(MLA_PAD_WIDTH, KV_LORA)
    rope_sel = (dst.T % HEAD_PAD) == (src.T + NOPE_DIM)
    rope_sel = rope_sel & (src.T < ROPE_DIM)
    w_abs = jnp.concatenate([w_abs, rope_sel.astype(F32)], axis=1)
    return dict(
        g_ff1=row(g_ff1), w_ff1_gate=bf(w_ff1_gate), w_ff1_up=bf(w_ff1_up), w_ff1_down=bf(w_ff1_down),
        g_mix=row(g_mix), w_q=bf(w_in[:, :off_kv]), w_kv=bf(w_in[:, off_kv:off_kr]), w_sm=bf(w_sm),
        w_m=bf(w_in[:, off_m:off_i]), w_mk=bf(w_in[:, off_m + MLSTM_WIDTH:off_m + 2 * MLSTM_WIDTH]),
        w_qvo_t=bf(jnp.concatenate([w_in[:, off_m:off_m + MLSTM_WIDTH],
                                    w_in[:, off_m + 2 * MLSTM_WIDTH:off_i]], axis=1).T), g_q=row(g_q), w_uq=bf(_pad_heads(w_uq, NOPE_DIM + ROPE_DIM)),
        g_kv=row(g_kv), w_uk=bf(_pad_heads(w_uk, NOPE_DIM)), w_uv=bf(w_uv),
        w_uv_t=bf(jnp.pad(w_uv.T.reshape(MLA_HEADS, V_DIM, KV_LORA),
                          ((0, 0), (0, V_ROWS - V_DIM), (0, 0))).reshape(MLA_HEADS * V_ROWS, KV_LORA)),
        gate_bias=row(gate_bias), w_abs=bf(w_abs),
        g_attn=row(g_attn_out), w_out_a=bf(w_out[:MLA_WIDTH]),
        w_out_m=bf(w_out[MLA_WIDTH:]), g_mlstm=row(g_mlstm_out),
        g_mlstm_lanes=jnp.broadcast_to(g_mlstm_out.reshape(MLSTM_WIDTH, 1).astype(F32),
                                       (MLSTM_WIDTH, LANES)),
        g_ff2=row(g_ff2), w_ff2_gate=bf(w_ff2_gate), w_ff2_up=bf(w_ff2_up), w_ff2_down=bf(w_ff2_down),
        g_ple=row(g_ple), w_ple_gate=bf(w_ple_gate), w_ple_proj=bf(w_ple_proj), g_final=row(g_final))


PROMPT_TILE = 512


def kernel(x_prompt, x_sample, p_prompt, p_sample, cache_ckv, cache_krope, state_C, state_n, state_m, page_table, g_ff1, w_ff1_gate, w_ff1_up, w_ff1_down, g_mix, w_in, g_q, w_uq, g_kv, w_uk, w_uv, b_gate_i, b_gate_f, g_attn_out, g_mlstm_out, w_out, g_ff2, w_ff2_gate, w_ff2_up, w_ff2_down, g_ple, w_ple_gate, w_ple_proj, g_final):
    assert w_in.shape[0] == 1, "single-layer trunk"
    w = _prep_weights(g_ff1[0], w_ff1_gate[0], w_ff1_up[0], w_ff1_down[0], g_mix[0], w_in[0],
                      g_q[0], w_uq[0], g_kv[0], w_uk[0], w_uv[0], b_gate_i[0], b_gate_f[0],
                      g_attn_out[0], g_mlstm_out[0], w_out[0], g_ff2[0], w_ff2_gate[0],
                      w_ff2_up[0], w_ff2_down[0], g_ple[0], w_ple_gate[0], w_ple_proj[0], g_final)
    nb_p, seq, _ = x_prompt.shape
    nb_s = x_sample.shape[0]
    t_p = nb_p * seq

    tab_p = _rope_tables(jnp.arange(seq))
    h_p = _ffn_call(x_prompt.reshape(t_p, D_MODEL), w["g_ff1"], w["w_ff1_gate"], w["w_ff1_up"],
                    w["w_ff1_down"], PROMPT_TILE)
    q_p, k_p, vt_p, ckv_p, _, krt_p, mk_p, mqt_p, mvt_p, mot_p, gt_p = _proj_call(
        h_p, w, tab_p, PROMPT_TILE, cell_feature_major=True)
    seq3 = lambda a: a.reshape(nb_p, seq, a.shape[-1])
    a_p = _flash_call(seq3(q_p), seq3(k_p), vt_p)
    hm_p, cxt_p, m_p = _mlstm_call(seq3(mk_p), mqt_p, mvt_p, mot_p, gt_p, w["g_mlstm_lanes"])
    cxt_p = cxt_p.reshape(nb_p, M_HEADS, CELL_V_ROWS, M_DIM)
    y_p = _merge_call(h_p, a_p.reshape(t_p, MLA_WIDTH), hm_p.reshape(t_p, MLSTM_WIDTH),
                      p_prompt.reshape(t_p, PLE_DIM), w, PROMPT_TILE)

    tab_s = _rope_tables(jnp.full((nb_s,), PAST_LEN, jnp.int32))
    h_s = _ffn_call(x_sample.reshape(nb_s, D_MODEL), w["g_ff1"], w["w_ff1_gate"], w["w_ff1_up"],
                    w["w_ff1_down"], nb_s)
    q_s, _, _, ckv_s, kr_s, krt_s, mq_s, mk_s, mv_s, mo_s, gates_s = _proj_call(
        h_s, w, tab_s, nb_s, cell_feature_major=False)
    n_phys = cache_ckv.shape[1]
    a_s = _sample_attn_call(
        page_table, q_s.reshape(nb_s, 1, MLA_PAD_WIDTH), ckv_s.reshape(nb_s, 1, KV_LORA),
        kr_s.reshape(nb_s, 1, ROPE_DIM), cache_ckv.reshape(n_phys, PAGE_SIZE, KV_LORA),
        jnp.swapaxes(cache_krope.reshape(n_phys, PAGE_SIZE, ROPE_DIM), 1, 2), w["w_abs"], w["w_uv"])
    hm_s, c_s, n_s, m_s = _sample_mlstm_call(
        mq_s, mk_s, mv_s, mo_s, gates_s, w["g_mlstm"], state_C[0].astype(F32),
        state_n[0].astype(F32).reshape(nb_s, MLSTM_WIDTH), state_m[0].astype(F32))
    y_s = _merge_call(h_s, a_s.reshape(nb_s, MLA_WIDTH), hm_s, p_sample.reshape(nb_s, PLE_DIM),
                      w, nb_s)

    return (y_p.reshape(nb_p, seq, D_MODEL), y_s.reshape(nb_s, 1, D_MODEL),
            ckv_p.reshape(1, nb_p, seq, KV_LORA), jnp.swapaxes(krt_p, 1, 2)[None],
            jnp.swapaxes(cxt_p[:, :, :M_DIM, :], 2, 3)[None], cxt_p[None, :, :, M_DIM, :],
            m_p[:, 0, 0].reshape(1, nb_p, M_HEADS),
            ckv_s.reshape(1, nb_s, 1, KV_LORA), jnp.swapaxes(krt_s, 1, 2).reshape(1, nb_s, 1, ROPE_DIM),
            c_s[None], n_s.reshape(1, nb_s, M_HEADS, M_DIM), m_s[None])
```

```python
import functools
import math

import jax
import jax.numpy as jnp
from jax import lax
from jax.experimental import pallas as pl
from jax.experimental.pallas import tpu as pltpu

F32 = jnp.float32
BF16 = jnp.bfloat16

D_MODEL = 1024
SEQ = 8192
DEC_BATCH = 128
PAST_LEN = 8192
PAGE_SIZE = 128
N_PAGES = PAST_LEN // PAGE_SIZE
MLA_HEADS = 8
Q_LORA = 384
KV_LORA = 256
NOPE_DIM = 64
ROPE_DIM = 32
ROPE_HALF = ROPE_DIM // 2
V_DIM = 64
ROPE_THETA = 10000.0
M_HEADS = 4
M_DIM = 128
CHUNK = 128
MLSTM_WIDTH = M_HEADS * M_DIM
D_FF = 2816
PLE_DIM = 256
EPS = 1e-6

LANES = 128
SUBLANES = 8

HEAD_PAD = LANES
MLA_PAD_WIDTH = MLA_HEADS * HEAD_PAD
MLA_WIDTH = MLA_HEADS * V_DIM
V_ROWS = 80
GATE_I_LANE = ROPE_DIM
GATE_F_LANE = ROPE_DIM + M_HEADS
QK_SCALE = (NOPE_DIM + ROPE_DIM) ** -0.5 * math.log2(math.e)

VMEM_LIMIT = 56 * 1024 * 1024


def _dot(a, b):
    return jnp.dot(a, b, preferred_element_type=F32)


def _dot_nt(a, b):
    return lax.dot_general(a, b, (((1,), (1,)), ((), ())), preferred_element_type=F32)


def _dot_tn(a, b):
    return lax.dot_general(a, b, (((0,), (0,)), ((), ())), preferred_element_type=F32)


def _rms(x, g):
    ms = jnp.sum(x * x, axis=-1, keepdims=True) * (1.0 / x.shape[-1])
    return x * lax.rsqrt(ms + EPS) * g


def _resident(shape):
    return pl.BlockSpec(shape, lambda *_: (0,) * len(shape), pipeline_mode=pl.Buffered(1))


def _params(n_axes):
    return pltpu.CompilerParams(dimension_semantics=("arbitrary",) * n_axes,
                                vmem_limit_bytes=VMEM_LIMIT)


MXU_TILE = 256
FF_CHUNK_BOUNDS = (0, 6 * MXU_TILE, D_FF)
assert D_FF % MXU_TILE == 0


def _swiglu_half(xn, wg_ref, wu_ref, wd_ref):
    out = None
    for lo, hi in zip(FF_CHUNK_BOUNDS[:-1], FF_CHUNK_BOUNDS[1:]):
        cols = slice(lo, hi)
        gate = _dot(xn, wg_ref[:, cols])
        up = _dot(xn, wu_ref[:, cols])
        act = (jax.nn.silu(gate) * up).astype(BF16)
        part = _dot(act, wd_ref[cols, :])
        out = part if out is None else out + part
    return 0.5 * out


def _ffn_body(x_ref, g_ref, wg_ref, wu_ref, wd_ref, o_ref):
    x = x_ref[...]
    xn = _rms(x, g_ref[...]).astype(BF16)
    o_ref[...] = x + _swiglu_half(xn, wg_ref, wu_ref, wd_ref)


def _ffn_call(x, g, wg, wu, wd, tm):
    t = x.shape[0]
    row = lambda i: (i, 0)
    return pl.pallas_call(
        _ffn_body,
        grid=(t // tm,),
        in_specs=[pl.BlockSpec((tm, D_MODEL), row), _resident((1, D_MODEL)),
                  _resident((D_MODEL, D_FF)), _resident((D_MODEL, D_FF)), _resident((D_FF, D_MODEL))],
        out_specs=pl.BlockSpec((tm, D_MODEL), row),
        out_shape=jax.ShapeDtypeStruct((t, D_MODEL), F32),
        compiler_params=_params(1),
        name="ffn1",
    )(x, g, wg, wu, wd)


def _rope128(x, cos_tab, sin_tab, x2_start):
    lane = lax.broadcasted_iota(jnp.int32, x.shape, 1)
    partner = jnp.where(lane < x2_start, pltpu.roll(x, LANES - ROPE_HALF, 1),
                        pltpu.roll(x, ROPE_HALF, 1))
    return x * cos_tab + partner * sin_tab


CELL_V_ROWS = M_DIM + 16


def _proj_body(cell_feature_major, h_ref, gmix_ref, wqs_ref, wkv_ref, gq_ref, wuq_ref,
               gkv_ref, wuk_ref, wuvt_ref, bias_ref, tab_ref, *refs):
    if cell_feature_major:
        wmk_ref, wqvot_ref = refs[:2]
        q_out, k_out, vt_out, ckv_out, kr_out, krt_out, mk_out, mqt_out, mvt_out, mot_out, gt_out = refs[2:]
    else:
        wm_ref = refs[0]
        q_out, k_out, vt_out, ckv_out, kr_out, krt_out, mq_out, mk_out, mv_out, mo_out, gate_out = refs[1:]
    u = _rms(h_ref[...], gmix_ref[...]).astype(BF16)
    ckv = _rms(_dot(u, wkv_ref[...]), gkv_ref[...])
    ckv_out[...] = ckv
    ckv_b = ckv.astype(BF16)
    zqs = _dot(u, wqs_ref[...])
    zs = zqs[:, Q_LORA:Q_LORA + LANES]
    n_tok = zs.shape[0]
    cs = jnp.concatenate([tab_ref[...], jnp.zeros((LANES - ROPE_DIM, n_tok), F32)], axis=0).T
    lane = lax.broadcasted_iota(jnp.int32, zs.shape, 1)
    shifted = lambda by: pltpu.roll(cs, by, 1)
    cos_k = jnp.where(lane < ROPE_HALF, cs, jnp.where(lane < ROPE_DIM, shifted(ROPE_HALF), 0.0))
    sin_k = jnp.where(lane < ROPE_HALF, -shifted(LANES - ROPE_HALF), jnp.where(lane < ROPE_DIM, cs, 0.0))
    kr = _rope128(zs, cos_k, sin_k, ROPE_HALF)
    kr_out[...] = kr[:, :ROPE_DIM]
    krt_out[0] = kr.T[:ROPE_DIM, :]
    kn = _dot(ckv_b, wuk_ref[...])
    kr_head = pltpu.roll(kr, NOPE_DIM, 1)
    for hd in range(MLA_HEADS):
        lanes = slice(hd * HEAD_PAD, (hd + 1) * HEAD_PAD)
        k_out[:, lanes] = (kn[:, lanes] + kr_head).astype(BF16)
    x1 = (lane >= NOPE_DIM) & (lane < NOPE_DIM + ROPE_HALF)
    x2 = (lane >= NOPE_DIM + ROPE_HALF) & (lane < NOPE_DIM + ROPE_DIM)
    cos_q = jnp.where(lane < NOPE_DIM, 1.0, jnp.where(x1, shifted(NOPE_DIM), jnp.where(
        x2, shifted(NOPE_DIM + ROPE_HALF), 0.0)))
    sin_q = jnp.where(x1, -shifted(NOPE_DIM - ROPE_HALF), jnp.where(x2, shifted(NOPE_DIM), 0.0))
    vt = _dot_nt(wuvt_ref[...], ckv_b)
    vrow = lax.broadcasted_iota(jnp.int32, vt.shape, 0)
    vt_out[0, 0] = jnp.where(vrow % V_ROWS == V_DIM, 1.0, vt).astype(BF16)
    qn = _rms(zqs[:, :Q_LORA], gq_ref[...]).astype(BF16)
    q = _dot(qn, wuq_ref[...])
    cos_qs = cos_q * QK_SCALE
    sin_qs = sin_q * QK_SCALE
    for hd in range(MLA_HEADS):
        lanes = slice(hd * HEAD_PAD, (hd + 1) * HEAD_PAD)
        q_out[:, lanes] = _rope128(q[:, lanes], cos_qs, sin_qs, NOPE_DIM + ROPE_HALF).astype(BF16)
    zb = zs + bias_ref[...]
    log_sig = jnp.minimum(zb, 0.0) - jnp.log1p(jnp.exp(-jnp.abs(zb)))
    is_f = (lane >= GATE_F_LANE) & (lane < GATE_F_LANE + M_HEADS)
    gates = jnp.where(is_f, log_sig, zb)
    k_scale = M_DIM ** -0.5
    if cell_feature_major:
        mk_out[...] = (_dot(u, wmk_ref[...]) * k_scale).astype(BF16)
        zt = _dot_nt(wqvot_ref[...], u)
        mqt_out[0, 0] = zt[0:MLSTM_WIDTH].astype(BF16)
        unit_rows = (lax.broadcasted_iota(jnp.int32, (CELL_V_ROWS - M_DIM, n_tok), 0) == 0)
        for hd in range(M_HEADS):
            rows = slice(MLSTM_WIDTH + hd * M_DIM, MLSTM_WIDTH + (hd + 1) * M_DIM)
            mvt_out[0, 0, hd * CELL_V_ROWS:hd * CELL_V_ROWS + M_DIM, :] = zt[rows].astype(BF16)
            mvt_out[0, 0, hd * CELL_V_ROWS + M_DIM:(hd + 1) * CELL_V_ROWS, :] = (
                unit_rows.astype(F32).astype(BF16))
        mot_out[0, 0] = zt[2 * MLSTM_WIDTH:3 * MLSTM_WIDTH]
        gt_out[0, 0] = gates.T[GATE_I_LANE:GATE_I_LANE + 2 * M_HEADS, :]
    else:
        zm = _dot(u, wm_ref[...])
        mq_out[...] = zm[:, 0:MLSTM_WIDTH].astype(BF16)
        mk_out[...] = (zm[:, MLSTM_WIDTH:2 * MLSTM_WIDTH] * k_scale).astype(BF16)
        mv_out[...] = zm[:, 2 * MLSTM_WIDTH:3 * MLSTM_WIDTH].astype(BF16)
        mo_out[...] = zm[:, 3 * MLSTM_WIDTH:4 * MLSTM_WIDTH]
        gate_out[...] = gates


def _proj_call(h, w, rope_tab, tm, cell_feature_major):
    t = h.shape[0]
    n_tab = rope_tab.shape[1] // tm
    n_seq = t // tm // n_tab
    row = lambda i: (i, 0)
    tab = lambda i: (0, i % n_tab)
    tok = lambda width, dtype: (jax.ShapeDtypeStruct((t, width), dtype),
                                pl.BlockSpec((tm, width), row))
    slab = lambda rows, dtype: (jax.ShapeDtypeStruct((n_seq, n_tab, rows, tm), dtype),
                                pl.BlockSpec((1, 1, rows, tm), lambda i: (i // n_tab, i % n_tab, 0, 0)))
    outs = [tok(MLA_PAD_WIDTH, BF16), tok(MLA_PAD_WIDTH, BF16), slab(MLA_HEADS * V_ROWS, BF16),
            tok(KV_LORA, F32), tok(ROPE_DIM, F32),
            (jax.ShapeDtypeStruct((n_seq, ROPE_DIM, n_tab * tm), F32),
             pl.BlockSpec((1, ROPE_DIM, tm), lambda i: (i // n_tab, 0, i % n_tab)))]
    if cell_feature_major:
        cell_w = [w["w_mk"], w["w_qvo_t"]]
        outs += [tok(MLSTM_WIDTH, BF16), slab(MLSTM_WIDTH, BF16), slab(M_HEADS * CELL_V_ROWS, BF16),
                 slab(MLSTM_WIDTH, F32), slab(2 * M_HEADS, F32)]
    else:
        cell_w = [w["w_m"]]
        outs += [tok(MLSTM_WIDTH, BF16)] * 3 + [tok(MLSTM_WIDTH, F32), tok(LANES, F32)]
    return pl.pallas_call(
        functools.partial(_proj_body, cell_feature_major),
        grid=(t // tm,),
        in_specs=[pl.BlockSpec((tm, D_MODEL), row), _resident((1, D_MODEL)),
                  _resident((D_MODEL, Q_LORA + LANES)), _resident((D_MODEL, KV_LORA)),
                  _resident((1, Q_LORA)), _resident((Q_LORA, MLA_PAD_WIDTH)),
                  _resident((1, KV_LORA)), _resident((KV_LORA, MLA_PAD_WIDTH)),
                  _resident((MLA_HEADS * V_ROWS, KV_LORA)),
                  _resident((1, LANES)),
                  pl.BlockSpec((ROPE_DIM, tm), tab)] + [_resident(cw.shape) for cw in cell_w],
        out_specs=[o[1] for o in outs],
        out_shape=[o[0] for o in outs],
        compiler_params=_params(1),
        name="proj",
    )(h, w["g_mix"], w["w_qs"], w["w_kv"], w["g_q"], w["w_uq"], w["g_kv"],
      w["w_uk"], w["w_uv_t"], w["gate_bias"], rope_tab, *cell_w)


ATT_BLOCK = 512
ATT_HEADS = 8
ATT_LANES = ATT_HEADS * HEAD_PAD
ATT_V_ROWS = ATT_HEADS * V_ROWS
ATT_LOOKAHEAD = 3


def _flash_body(q_ref, k_ref, vt_ref, o_ref):
    qi = pl.program_id(2)
    blk = ATT_BLOCK
    head_lanes = [slice(hd * HEAD_PAD, (hd + 1) * HEAD_PAD) for hd in range(ATT_HEADS)]
    qs = [q_ref[0, :, lanes] for lanes in head_lanes]

    def step(j, carry, diagonal):
        start = pl.multiple_of(j * blk, blk)
        scores = [None] * ATT_HEADS
        if diagonal:
            key = lax.broadcasted_iota(jnp.int32, (blk, blk), 0)
            qry = lax.broadcasted_iota(jnp.int32, (blk, blk), 1)
            visible = key <= qry
        out = []
        for hd in range(ATT_HEADS):
            m, acc = carry[hd]
            for nxt in range(hd if hd else 0, min(hd + ATT_LOOKAHEAD, ATT_HEADS - 1) + 1):
                if scores[nxt] is None:
                    scores[nxt] = _dot_nt(k_ref[0, pl.ds(start, blk), head_lanes[nxt]], qs[nxt])
            s = jnp.where(visible, scores[hd], -jnp.inf) if diagonal else scores[hd]
            m_new = jnp.maximum(m, jnp.max(s, axis=0, keepdims=True))
            p = jnp.exp2(s - m_new).astype(BF16)
            vt = vt_ref[0, j, hd * V_ROWS:(hd + 1) * V_ROWS, :]
            acc = jnp.exp2(m - m_new) * acc + _dot(vt, p)
            out.append((m_new, acc))
        return tuple(out)

    init = tuple((jnp.full((1, blk), -jnp.inf, F32), jnp.zeros((V_ROWS, blk), F32))
                 for _ in range(ATT_HEADS))
    carry = lax.fori_loop(0, qi, lambda j, c: step(j, c, False), init)
    final = step(qi, carry, True)
    for pair in range(ATT_HEADS // 2):
        o_t = jnp.concatenate([final[hd][1][:V_DIM] / final[hd][1][V_DIM:V_DIM + 1]
                               for hd in (2 * pair, 2 * pair + 1)], axis=0)
        o_ref[0, :, pair * LANES:(pair + 1) * LANES] = o_t.T.astype(BF16)


def _flash_call(q, k, vt):
    b, s, _ = q.shape
    assert vt.shape == (b, s // ATT_BLOCK, MLA_HEADS * V_ROWS, ATT_BLOCK)
    qmap = lambda bi, gi, qi: (bi, qi, gi)
    return pl.pallas_call(
        _flash_body,
        grid=(b, MLA_HEADS // ATT_HEADS, s // ATT_BLOCK),
        in_specs=[pl.BlockSpec((1, ATT_BLOCK, ATT_LANES), qmap),
                  pl.BlockSpec((1, s, ATT_LANES), lambda bi, gi, qi: (bi, 0, gi),
                               pipeline_mode=pl.Buffered(1)),
                  pl.BlockSpec((1, s // ATT_BLOCK, ATT_V_ROWS, ATT_BLOCK),
                               lambda bi, gi, qi: (bi, 0, gi, 0), pipeline_mode=pl.Buffered(1))],
        out_specs=pl.BlockSpec((1, ATT_BLOCK, ATT_HEADS * V_DIM), qmap),
        out_shape=jax.ShapeDtypeStruct((b, s, MLA_WIDTH), BF16),
        compiler_params=_params(3),
        name="prompt_attn",
    )(q, k, vt)


def _lane_scan(x, op, fill):
    lane = lax.broadcasted_iota(jnp.int32, x.shape, 1)
    shift = 1
    while shift < x.shape[1]:
        x = op(x, jnp.where(lane >= shift, pltpu.roll(x, shift, 1), fill))
        shift *= 2
    return x


def _mlstm_body(k_ref, qt_ref, vt_ref, ot_ref, gt_ref, gm_ref, hm_out, cxt_out, m_out, cxt_s, m_s):
    ci = pl.program_id(0)
    n_seq = k_ref.shape[0]
    L = CHUNK

    @pl.when(ci == 0)
    def _():
        cxt_s[...] = jnp.zeros_like(cxt_s)
        m_s[...] = jnp.zeros_like(m_s)

    s_idx = lax.broadcasted_iota(jnp.int32, (L, L), 0)
    t_idx = lax.broadcasted_iota(jnp.int32, (L, L), 1)
    causal = s_idx <= t_idx

    streams = []
    for sq in range(n_seq):
        rows = gt_ref[sq, 0]
        b_rows = pltpu.roll(_lane_scan(rows, jnp.add, 0.0), M_HEADS, 0)
        head_row = lax.broadcasted_iota(jnp.int32, rows.shape, 0) < M_HEADS
        a_rows = jnp.where(head_row, rows - b_rows, 0.0)
        run_max = _lane_scan(a_rows, jnp.maximum, -jnp.inf)
        top = jnp.broadcast_to(jnp.max(a_rows, axis=1, keepdims=True), rows.shape)
        b_last = pltpu.roll(jnp.broadcast_to(jnp.sum(rows, axis=1, keepdims=True), rows.shape),
                            M_HEADS, 0)
        src = jnp.exp(a_rows - top)
        a_cols = jnp.concatenate([a_rows, jnp.zeros((L - SUBLANES, L), F32)], axis=0).T
        for hd in range(M_HEADS):
            feat = slice(hd * M_DIM, (hd + 1) * M_DIM)
            idx = sq * M_HEADS + hd
            k = k_ref[sq, :, feat]
            qt = qt_ref[sq, 0, feat, :]
            vt = vt_ref[sq, 0, hd * CELL_V_ROWS:(hd + 1) * CELL_V_ROWS, :]
            m_prev = m_s[idx][0:1, :]
            big_m = jnp.maximum(m_prev, run_max[hd:hd + 1, :])
            decay = jnp.where(causal, jnp.exp(a_cols[:, hd:hd + 1] - big_m), 0.0)
            cxt_prev = cxt_s[idx]
            weighted_vt = (vt.astype(F32) * src[hd:hd + 1, :]).astype(BF16)
            streams.append(dict(
                sq=sq, feat=feat, idx=idx, m_prev=m_prev, cxt_prev=cxt_prev, vt=vt,
                w_inter=jnp.exp(m_prev - big_m),
                floor=jnp.exp(-(b_rows[hd:hd + 1, :] + big_m)),
                top=top[hd:hd + 1, :], b_last=b_last[hd:hd + 1, :],
                sqk=_dot(k, qt) * decay,
                qe=_dot(cxt_prev.astype(BF16), qt),
                kv=_dot(weighted_vt, k)))

    for st in streams:
        ue = _dot(st["vt"], st["sqk"].astype(BF16))
        num = ue[:M_DIM] + st["qe"][:M_DIM] * st["w_inter"]
        den = ue[M_DIM:M_DIM + 1] + st["qe"][M_DIM:M_DIM + 1] * st["w_inter"]
        inv = 1.0 / jnp.maximum(jnp.abs(den), st["floor"])
        t = jax.nn.sigmoid(ot_ref[st["sq"], 0, st["feat"], :]) * num
        ms = jnp.sum(t * t, axis=0, keepdims=True) * (1.0 / M_DIM)
        scale = inv * lax.rsqrt(ms * inv * inv + EPS)
        hm_out[st["sq"], :, st["feat"]] = (t * scale * gm_ref[st["feat"], :]).T.astype(BF16)
        m_last = jnp.maximum(st["m_prev"], st["top"])
        keep = jnp.exp(st["m_prev"] - m_last)
        gain = jnp.exp(st["top"] - m_last)
        cxt_s[st["idx"]] = keep * st["cxt_prev"] + gain * st["kv"]
        m_s[st["idx"]] = jnp.broadcast_to(st["b_last"] + m_last, (SUBLANES, LANES))

    @pl.when(ci == pl.num_programs(0) - 1)
    def _():
        cxt_out[...] = cxt_s[...]
        m_out[...] = m_s[...]


def _mlstm_call(mk, mqt, mvt, mot, gt, gm_lanes):
    b, s, _ = mk.shape
    tile = mqt.shape[-1]
    per_tile = tile // CHUNK
    n_streams = b * M_HEADS
    slab = lambda rows: pl.BlockSpec((b, 1, rows, CHUNK),
                                     lambda ci: (0, ci // per_tile, 0, ci % per_tile))
    tok = pl.BlockSpec((b, CHUNK, MLSTM_WIDTH), lambda ci: (0, ci, 0))
    whole3 = lambda ci: (0, 0, 0)
    return pl.pallas_call(
        _mlstm_body,
        grid=(s // CHUNK,),
        in_specs=[tok, slab(MLSTM_WIDTH), slab(M_HEADS * CELL_V_ROWS), slab(MLSTM_WIDTH),
                  slab(2 * M_HEADS), _resident((MLSTM_WIDTH, LANES))],
        out_specs=[tok,
                   pl.BlockSpec((n_streams, CELL_V_ROWS, M_DIM), whole3),
                   pl.BlockSpec((n_streams, SUBLANES, LANES), whole3)],
        out_shape=[jax.ShapeDtypeStruct((b, s, MLSTM_WIDTH), BF16),
                   jax.ShapeDtypeStruct((n_streams, CELL_V_ROWS, M_DIM), F32),
                   jax.ShapeDtypeStruct((n_streams, SUBLANES, LANES), F32)],
        scratch_shapes=[pltpu.VMEM((n_streams, CELL_V_ROWS, M_DIM), F32),
                        pltpu.VMEM((n_streams, SUBLANES, LANES), F32)],
        compiler_params=_params(1),
        name="prompt_mlstm",
    )(mk, mqt, mvt, mot, gt, gm_lanes)


def _merge_body(h_ref, a_ref, hm_ref, p_ref, ga_ref, woa_ref, wom_ref, gff_ref, wg_ref, wu_ref,
                wd_ref, gple_ref, wpg_ref, wpp_ref, gfin_ref, y_ref):
    a = a_ref[...].astype(F32)
    an = _rms(a, ga_ref[...]).astype(BF16)
    h = h_ref[...] + _dot(an, woa_ref[...]) + _dot(hm_ref[...], wom_ref[...])
    h = h + _swiglu_half(_rms(h, gff_ref[...]).astype(BF16), wg_ref, wu_ref, wd_ref)
    gate = jax.nn.sigmoid(_dot(_rms(h, gple_ref[...]).astype(BF16), wpg_ref[...]))
    h = h + gate * _dot(p_ref[...].astype(BF16), wpp_ref[...])
    y_ref[...] = _rms(h, gfin_ref[...])


def _merge_call(h, a, hm, p, w, tm):
    t = h.shape[0]
    row = lambda i: (i, 0)
    return pl.pallas_call(
        _merge_body,
        grid=(t // tm,),
        in_specs=[pl.BlockSpec((tm, D_MODEL), row), pl.BlockSpec((tm, MLA_WIDTH), row),
                  pl.BlockSpec((tm, MLSTM_WIDTH), row), pl.BlockSpec((tm, PLE_DIM), row),
                  _resident((1, MLA_WIDTH)), _resident((MLA_WIDTH, D_MODEL)),
                  _resident((MLSTM_WIDTH, D_MODEL)), _resident((1, D_MODEL)),
                  _resident((D_MODEL, D_FF)), _resident((D_MODEL, D_FF)), _resident((D_FF, D_MODEL)),
                  _resident((1, D_MODEL)), _resident((D_MODEL, D_MODEL)),
                  _resident((PLE_DIM, D_MODEL)), _resident((1, D_MODEL))],
        out_specs=pl.BlockSpec((tm, D_MODEL), row),
        out_shape=jax.ShapeDtypeStruct((t, D_MODEL), F32),
        compiler_params=_params(1),
        name="merge",
    )(h, a, hm, p, w["g_attn"], w["w_out_a"], w["w_out_m"], w["g_ff2"], w["w_ff2_gate"],
      w["w_ff2_up"], w["w_ff2_down"], w["g_ple"], w["w_ple_gate"], w["w_ple_proj"], w["g_final"])


N_SLOTS = 3
SAMPLE_CHUNKS = 4
SAMPLE_CHUNK_PAGES = N_PAGES // SAMPLE_CHUNKS
SAMPLE_CHUNK_KEYS = SAMPLE_CHUNK_PAGES * PAGE_SIZE


def _sample_attn_body(pt_ref, q_ref, ckv_ref, kr_ref, cache_c, cache_r, wabs_ref, wuv_ref,
                      o_ref, cbuf, rbuf, sem_c, sem_r):
    b = pl.program_id(0)
    nb = pl.num_programs(0)
    slot = b % N_SLOTS

    def copies(bi, sl):
        out = []
        for j in range(N_PAGES):
            page = pt_ref[bi, j]
            out.append(pltpu.make_async_copy(cache_c.at[page], cbuf.at[sl, j], sem_c.at[sl]))
            out.append(pltpu.make_async_copy(
                cache_r.at[page], rbuf.at[sl, :, pl.ds(j * PAGE_SIZE, PAGE_SIZE)], sem_r.at[sl]))
        return out

    @pl.when(b == 0)
    def _():
        for bi in range(min(N_SLOTS, DEC_BATCH)):
            for cp in copies(bi, bi):
                cp.start()

    q_row = q_ref[0]
    sub = lax.broadcasted_iota(jnp.int32, (MLA_HEADS, MLA_PAD_WIDTH), 0)
    lane = lax.broadcasted_iota(jnp.int32, (MLA_HEADS, MLA_PAD_WIDTH), 1)
    own = (lane // HEAD_PAD) == sub
    q_bd = jnp.where(own, jnp.broadcast_to(q_row.astype(F32), own.shape), 0.0).astype(BF16)
    q_ext = _dot(q_bd, wabs_ref[...])
    q_abs = q_ext[:, :KV_LORA].astype(BF16)
    q_rope = q_ext[:, KV_LORA:KV_LORA + ROPE_DIM].astype(BF16)

    c_new = ckv_ref[0].astype(BF16).astype(F32)
    r_new = kr_ref[0].astype(BF16).astype(F32)
    s_new = (jnp.sum(q_abs.astype(F32) * c_new, axis=1, keepdims=True)
             + jnp.sum(q_rope.astype(F32) * r_new, axis=1, keepdims=True))

    for cp in copies(b, slot):
        cp.wait()

    def chunk_keys(i):
        pages = slice(i * SAMPLE_CHUNK_PAGES, (i + 1) * SAMPLE_CHUNK_PAGES)
        keys = slice(i * SAMPLE_CHUNK_KEYS, (i + 1) * SAMPLE_CHUNK_KEYS)
        kc = cbuf[slot, pages].reshape(SAMPLE_CHUNK_KEYS, KV_LORA).astype(BF16)
        return kc, _dot_nt(q_abs, kc) + _dot(q_rope, rbuf[slot, :, keys].astype(BF16))

    m = s_new
    l = jnp.ones_like(s_new)
    acc = jnp.broadcast_to(c_new, (MLA_HEADS, KV_LORA))
    kc, s = chunk_keys(0)
    for i in range(SAMPLE_CHUNKS):
        nxt = chunk_keys(i + 1) if i + 1 < SAMPLE_CHUNKS else None
        m_new = jnp.maximum(m, jnp.max(s, axis=1, keepdims=True))
        alpha = jnp.exp2(m - m_new)
        p = jnp.exp2(s - m_new)
        l = alpha * l + jnp.sum(p, axis=1, keepdims=True)
        acc = alpha * acc + _dot(p.astype(BF16), kc)
        m = m_new
        if nxt is not None:
            kc, s = nxt
    o_lat = (acc / l).astype(BF16)
    res = _dot(o_lat, wuv_ref[...])
    own_v = (lax.broadcasted_iota(jnp.int32, res.shape, 1) // V_DIM
             == lax.broadcasted_iota(jnp.int32, res.shape, 0))
    o_ref[0] = jnp.sum(jnp.where(own_v, res, 0.0), axis=0, keepdims=True).astype(BF16)

    @pl.when(b + N_SLOTS < nb)
    def _():
        for cp in copies(b + N_SLOTS, slot):
            cp.start()


def _sample_attn_call(page_table, q, ckv, kr, cache_c, cache_r, w_abs, w_uv):
    nb = q.shape[0]
    tok = lambda bi, pt: (bi, 0, 0)
    whole = lambda shape: pl.BlockSpec(shape, lambda bi, pt: (0,) * len(shape),
                                       pipeline_mode=pl.Buffered(1))
    grid_spec = pltpu.PrefetchScalarGridSpec(
        num_scalar_prefetch=1,
        grid=(nb,),
        in_specs=[pl.BlockSpec((1, 1, MLA_PAD_WIDTH), tok), pl.BlockSpec((1, 1, KV_LORA), tok),
                  pl.BlockSpec((1, 1, ROPE_DIM), tok),
                  pl.BlockSpec(memory_space=pl.ANY), pl.BlockSpec(memory_space=pl.ANY),
                  whole((MLA_PAD_WIDTH, KV_LORA + LANES)), whole((KV_LORA, MLA_WIDTH))],
        out_specs=pl.BlockSpec((1, 1, MLA_WIDTH), tok),
        scratch_shapes=[pltpu.VMEM((N_SLOTS, N_PAGES, PAGE_SIZE, KV_LORA), F32),
                        pltpu.VMEM((N_SLOTS, ROPE_DIM, PAST_LEN), F32),
                        pltpu.SemaphoreType.DMA((N_SLOTS,)), pltpu.SemaphoreType.DMA((N_SLOTS,))])
    return pl.pallas_call(
        _sample_attn_body,
        grid_spec=grid_spec,
        out_shape=jax.ShapeDtypeStruct((nb, 1, MLA_WIDTH), BF16),
        compiler_params=_params(1),
        name="sample_attn",
    )(page_table, q, ckv, kr, cache_c, cache_r, w_abs, w_uv)


SAMPLE_ROWS = 32


def _sample_mlstm_body(q_ref, k_ref, v_ref, o_ref, g_ref, gm_ref, c_ref, n_ref, m_ref,
                       hm_out, c_out, n_out, m_out):
    hd = pl.program_id(1)
    g = g_ref[...]
    lane = lax.broadcasted_iota(jnp.int32, g.shape, 1)
    ig = jnp.sum(jnp.where(lane == GATE_I_LANE + hd, g, 0.0), axis=1, keepdims=True)
    lf = jnp.sum(jnp.where(lane == GATE_F_LANE + hd, g, 0.0), axis=1, keepdims=True)
    m_all = m_ref[...]
    head_lane = lax.broadcasted_iota(jnp.int32, m_all.shape, 1)
    m0 = jnp.sum(jnp.where(head_lane == hd, m_all, 0.0), axis=1, keepdims=True)
    q = q_ref[...].astype(F32)
    k = k_ref[...].astype(F32)
    v = v_ref[...].astype(F32)
    n0 = n_ref[...]
    m_new = jnp.maximum(lf + m0, ig)
    keep = jnp.exp(lf + m0 - m_new)
    w_src = jnp.exp(ig - m_new)
    sqk = jnp.sum(q * k, axis=1, keepdims=True) * w_src
    wk = w_src * k
    qc = jnp.zeros_like(v)
    for d in range(M_DIM):
        c_d = c_ref[:, 0, d, :]
        qc = qc + q[:, d:d + 1] * c_d
        c_out[:, 0, d, :] = keep * c_d + wk[:, d:d + 1] * v
    num = sqk * v + qc * keep
    den = sqk + jnp.sum(q * n0, axis=1, keepdims=True) * keep
    den = jnp.maximum(jnp.abs(den), jnp.exp(-m_new))
    hh = jax.nn.sigmoid(o_ref[...]) * (num / den)
    hm_out[...] = _rms(hh, gm_ref[...]).astype(BF16)
    n_out[...] = keep * n0 + wk

    @pl.when(hd == 0)
    def _():
        m_out[...] = jnp.broadcast_to(m_new, m_all.shape)

    @pl.when(hd > 0)
    def _():
        m_out[...] = jnp.where(head_lane == hd, m_new, m_out[...])


def _sample_mlstm_call(mq, mk, mv, mo, gates, gm, c0, n0, m0):
    nb = mq.shape[0]
    rows = SAMPLE_ROWS
    tok = lambda bi, hi: (bi, hi)
    return pl.pallas_call(
        _sample_mlstm_body,
        grid=(nb // rows, M_HEADS),
        in_specs=[pl.BlockSpec((rows, M_DIM), tok)] * 4
                 + [pl.BlockSpec((rows, LANES), lambda bi, hi: (bi, 0)),
                    pl.BlockSpec((1, M_DIM), lambda bi, hi: (0, hi)),
                    pl.BlockSpec((rows, 1, M_DIM, M_DIM), lambda bi, hi: (bi, hi, 0, 0)),
                    pl.BlockSpec((rows, M_DIM), tok),
                    pl.BlockSpec((rows, M_HEADS), lambda bi, hi: (bi, 0))],
        out_specs=[pl.BlockSpec((rows, M_DIM), tok),
                   pl.BlockSpec((rows, 1, M_DIM, M_DIM), lambda bi, hi: (bi, hi, 0, 0)),
                   pl.BlockSpec((rows, M_DIM), tok),
                   pl.BlockSpec((rows, M_HEADS), lambda bi, hi: (bi, 0))],
        out_shape=[jax.ShapeDtypeStruct((nb, MLSTM_WIDTH), BF16),
                   jax.ShapeDtypeStruct((nb, M_HEADS, M_DIM, M_DIM), F32),
                   jax.ShapeDtypeStruct((nb, MLSTM_WIDTH), F32),
                   jax.ShapeDtypeStruct((nb, M_HEADS), F32)],
        compiler_params=_params(2),
        name="sample_mlstm",
    )(mq, mk, mv, mo, gates, gm, c0, n0, m0)


def _pad_heads(w, head_dim):
    rows = w.shape[0]
    w = w.reshape(rows, MLA_HEADS, head_dim)
    w = jnp.pad(w, ((0, 0), (0, 0), (0, HEAD_PAD - head_dim)))
    return w.reshape(rows, MLA_PAD_WIDTH)


def _rope_tables(pos):
    inv = ROPE_THETA ** (-jnp.arange(ROPE_HALF, dtype=F32) / ROPE_HALF)
    ang = inv[:, None] * pos.astype(F32)[None, :]
    return jnp.concatenate([jnp.cos(ang), jnp.sin(ang)], axis=0)


def _prep_weights(g_ff1, w_ff1_gate, w_ff1_up, w_ff1_down, g_mix, w_in, g_q, w_uq, g_kv, w_uk,
                  w_uv, b_gate_i, b_gate_f, g_attn_out, g_mlstm_out, w_out, g_ff2, w_ff2_gate,
                  w_ff2_up, w_ff2_down, g_ple, w_ple_gate, w_ple_proj, g_final):
    bf = lambda a: a.astype(BF16)
    row = lambda a: a.reshape(1, -1).astype(F32)
    off_kv, off_kr = Q_LORA, Q_LORA + KV_LORA
    off_m = off_kr + ROPE_DIM
    off_i = off_m + 4 * MLSTM_WIDTH
    small_pad = LANES - ROPE_DIM - 2 * M_HEADS
    w_sm = jnp.concatenate([w_in[:, off_kr:off_m], w_in[:, off_i:off_i + 2 * M_HEADS],
                            jnp.zeros((D_MODEL, small_pad), F32)], axis=1)
    gate_bias = jnp.concatenate([jnp.zeros((ROPE_DIM,), F32), b_gate_i, b_gate_f,
                                 jnp.zeros((small_pad,), F32)])
    src = jnp.arange(LANES)[:, None]
    dst = jnp.arange(MLA_PAD_WIDTH)[None, :]
    w_abs = jnp.pad(w_uk.reshape(KV_LORA, MLA_HEADS, NOPE_DIM).transpose(1, 2, 0),
                    ((0, 0), (0, HEAD_PAD - NOPE_DIM), (0, 0))).reshape(MLA_PAD_WIDTH, KV_LORA)
    rope_sel = (dst.T % HEAD_PAD) == (src.T + NOPE_DIM)
    rope_sel = rope_sel & (src.T < ROPE_DIM)
    w_abs = jnp.concatenate([w_abs, rope_sel.astype(F32)], axis=1)
    return dict(
        g_ff1=row(g_ff1), w_ff1_gate=bf(w_ff1_gate), w_ff1_up=bf(w_ff1_up), w_ff1_down=bf(w_ff1_down),
        g_mix=row(g_mix), w_qs=bf(jnp.concatenate([w_in[:, :off_kv], w_sm], axis=1)), w_kv=bf(w_in[:, off_kv:off_kr]),
        w_m=bf(w_in[:, off_m:off_i]), w_mk=bf(w_in[:, off_m + MLSTM_WIDTH:off_m + 2 * MLSTM_WIDTH]),
        w_qvo_t=bf(jnp.concatenate([w_in[:, off_m:off_m + MLSTM_WIDTH],
                                    w_in[:, off_m + 2 * MLSTM_WIDTH:off_i]], axis=1).T), g_q=row(g_q), w_uq=bf(_pad_heads(w_uq, NOPE_DIM + ROPE_DIM)),
        g_kv=row(g_kv), w_uk=bf(_pad_heads(w_uk, NOPE_DIM)), w_uv=bf(w_uv),
        w_uv_t=bf(jnp.pad(w_uv.T.reshape(MLA_HEADS, V_DIM, KV_LORA),
                          ((0, 0), (0, V_ROWS - V_DIM), (0, 0))).reshape(MLA_HEADS * V_ROWS, KV_LORA)),
        gate_bias=row(gate_bias), w_abs=bf(w_abs),
        g_attn=row(g_attn_out), w_out_a=bf(w_out[:MLA_WIDTH]),
        w_out_m=bf(w_out[MLA_WIDTH:]), g_mlstm=row(g_mlstm_out),
        g_mlstm_lanes=jnp.broadcast_to(g_mlstm_out.reshape(MLSTM_WIDTH, 1).astype(F32),
                                       (MLSTM_WIDTH, LANES)),
        g_ff2=row(g_ff2), w_ff2_gate=bf(w_ff2_gate), w_ff2_up=bf(w_ff2_up), w_ff2_down=bf(w_ff2_down),
        g_ple=row(g_ple), w_ple_gate=bf(w_ple_gate), w_ple_proj=bf(w_ple_proj), g_final=row(g_final))


PROMPT_TILE = 512


def kernel(x_prompt, x_sample, p_prompt, p_sample, cache_ckv, cache_krope, state_C, state_n, state_m, page_table, g_ff1, w_ff1_gate, w_ff1_up, w_ff1_down, g_mix, w_in, g_q, w_uq, g_kv, w_uk, w_uv, b_gate_i, b_gate_f, g_attn_out, g_mlstm_out, w_out, g_ff2, w_ff2_gate, w_ff2_up, w_ff2_down, g_ple, w_ple_gate, w_ple_proj, g_final):
    assert w_in.shape[0] == 1, "single-layer trunk"
    w = _prep_weights(g_ff1[0], w_ff1_gate[0], w_ff1_up[0], w_ff1_down[0], g_mix[0], w_in[0],
                      g_q[0], w_uq[0], g_kv[0], w_uk[0], w_uv[0], b_gate_i[0], b_gate_f[0],
                      g_attn_out[0], g_mlstm_out[0], w_out[0], g_ff2[0], w_ff2_gate[0],
                      w_ff2_up[0], w_ff2_down[0], g_ple[0], w_ple_gate[0], w_ple_proj[0], g_final)
    nb_p, seq, _ = x_prompt.shape
    nb_s = x_sample.shape[0]
    t_p = nb_p * seq

    tab_p = _rope_tables(jnp.arange(seq))
    h_p = _ffn_call(x_prompt.reshape(t_p, D_MODEL), w["g_ff1"], w["w_ff1_gate"], w["w_ff1_up"],
                    w["w_ff1_down"], PROMPT_TILE)
    q_p, k_p, vt_p, ckv_p, _, krt_p, mk_p, mqt_p, mvt_p, mot_p, gt_p = _proj_call(
        h_p, w, tab_p, PROMPT_TILE, cell_feature_major=True)
    seq3 = lambda a: a.reshape(nb_p, seq, a.shape[-1])
    a_p = _flash_call(seq3(q_p), seq3(k_p), vt_p)
    hm_p, cxt_p, m_p = _mlstm_call(seq3(mk_p), mqt_p, mvt_p, mot_p, gt_p, w["g_mlstm_lanes"])
    cxt_p = cxt_p.reshape(nb_p, M_HEADS, CELL_V_ROWS, M_DIM)
    y_p = _merge_call(h_p, a_p.reshape(t_p, MLA_WIDTH), hm_p.reshape(t_p, MLSTM_WIDTH),
                      p_prompt.reshape(t_p, PLE_DIM), w, PROMPT_TILE)

    tab_s = _rope_tables(jnp.full((nb_s,), PAST_LEN, jnp.int32))
    h_s = _ffn_call(x_sample.reshape(nb_s, D_MODEL), w["g_ff1"], w["w_ff1_gate"], w["w_ff1_up"],
                    w["w_ff1_down"], nb_s)
    q_s, _, _, ckv_s, kr_s, krt_s, mq_s, mk_s, mv_s, mo_s, gates_s = _proj_call(
        h_s, w, tab_s, nb_s, cell_feature_major=False)
    n_phys = cache_ckv.shape[1]
    a_s = _sample_attn_call(
        page_table, q_s.reshape(nb_s, 1, MLA_PAD_WIDTH), ckv_s.reshape(nb_s, 1, KV_LORA),
        kr_s.reshape(nb_s, 1, ROPE_DIM), cache_ckv.reshape(n_phys, PAGE_SIZE, KV_LORA),
        jnp.swapaxes(cache_krope.reshape(n_phys, PAGE_SIZE, ROPE_DIM), 1, 2), w["w_abs"], w["w_uv"])
    hm_s, c_s, n_s, m_s = _sample_mlstm_call(
        mq_s, mk_s, mv_s, mo_s, gates_s, w["g_mlstm"], state_C[0].astype(F32),
        state_n[0].astype(F32).reshape(nb_s, MLSTM_WIDTH), state_m[0].astype(F32))
    y_s = _merge_call(h_s, a_s.reshape(nb_s, MLA_WIDTH), hm_s, p_sample.reshape(nb_s, PLE_DIM),
                      w, nb_s)

    return (y_p.reshape(nb_p, seq, D_MODEL), y_s.reshape(nb_s, 1, D_MODEL),
            ckv_p.reshape(1, nb_p, seq, KV_LORA), jnp.swapaxes(krt_p, 1, 2)[None],
            jnp.swapaxes(cxt_p[:, :, :M_DIM, :], 2, 3)[None], cxt_p[None, :, :, M_DIM, :],
            m_p[:, 0, 0].reshape(1, nb_p, M_HEADS),
            ckv_s.reshape(1, nb_s, 1, KV_LORA), jnp.swapaxes(krt_s, 1, 2).reshape(1, nb_s, 1, ROPE_DIM),
            c_s[None], n_s.reshape(1, nb_s, M_HEADS, M_DIM), m_s[None])
```

```python
import functools
import math

import jax
import jax.numpy as jnp
from jax import lax
from jax.experimental import pallas as pl
from jax.experimental.pallas import tpu as pltpu

F32 = jnp.float32
BF16 = jnp.bfloat16

D_MODEL = 1024
SEQ = 8192
DEC_BATCH = 128
PAST_LEN = 8192
PAGE_SIZE = 128
N_PAGES = PAST_LEN // PAGE_SIZE
MLA_HEADS = 8
Q_LORA = 384
KV_LORA = 256
NOPE_DIM = 64
ROPE_DIM = 32
ROPE_HALF = ROPE_DIM // 2
V_DIM = 64
ROPE_THETA = 10000.0
M_HEADS = 4
M_DIM = 128
CHUNK = 128
MLSTM_WIDTH = M_HEADS * M_DIM
D_FF = 2816
PLE_DIM = 256
EPS = 1e-6

LANES = 128
SUBLANES = 8

HEAD_PAD = LANES
MLA_PAD_WIDTH = MLA_HEADS * HEAD_PAD
MLA_WIDTH = MLA_HEADS * V_DIM
V_ROWS = 80
GATE_I_LANE = ROPE_DIM
GATE_F_LANE = ROPE_DIM + M_HEADS
QK_SCALE = (NOPE_DIM + ROPE_DIM) ** -0.5 * math.log2(math.e)

VMEM_LIMIT = 56 * 1024 * 1024


def _dot(a, b):
    return jnp.dot(a, b, preferred_element_type=F32)


def _dot_nt(a, b):
    return lax.dot_general(a, b, (((1,), (1,)), ((), ())), preferred_element_type=F32)


def _rms(x, g):
    ms = jnp.sum(x * x, axis=-1, keepdims=True) * (1.0 / x.shape[-1])
    return x * lax.rsqrt(ms + EPS) * g


def _resident(shape):
    return pl.BlockSpec(shape, lambda *_: (0,) * len(shape), pipeline_mode=pl.Buffered(1))


def _params(n_axes):
    return pltpu.CompilerParams(dimension_semantics=("arbitrary",) * n_axes,
                                vmem_limit_bytes=VMEM_LIMIT)


MXU_TILE = 256
FF_CHUNK_BOUNDS = (0, 6 * MXU_TILE, D_FF)
assert D_FF % MXU_TILE == 0


def _swiglu_half(xn, wg_ref, wu_ref, wd_ref):
    out = None
    for lo, hi in zip(FF_CHUNK_BOUNDS[:-1], FF_CHUNK_BOUNDS[1:]):
        cols = slice(lo, hi)
        gate = _dot(xn, wg_ref[:, cols])
        up = _dot(xn, wu_ref[:, cols])
        act = (jax.nn.silu(gate) * up).astype(BF16)
        part = _dot(act, wd_ref[cols, :])
        out = part if out is None else out + part
    return 0.5 * out


def _ffn_body(x_ref, g_ref, wg_ref, wu_ref, wd_ref, o_ref):
    x = x_ref[...]
    xn = _rms(x, g_ref[...]).astype(BF16)
    o_ref[...] = x + _swiglu_half(xn, wg_ref, wu_ref, wd_ref)


def _ffn_call(x, g, wg, wu, wd, tm):
    t = x.shape[0]
    row = lambda i: (i, 0)
    return pl.pallas_call(
        _ffn_body,
        grid=(t // tm,),
        in_specs=[pl.BlockSpec((tm, D_MODEL), row), _resident((1, D_MODEL)),
                  _resident((D_MODEL, D_FF)), _resident((D_MODEL, D_FF)), _resident((D_FF, D_MODEL))],
        out_specs=pl.BlockSpec((tm, D_MODEL), row),
        out_shape=jax.ShapeDtypeStruct((t, D_MODEL), F32),
        compiler_params=_params(1),
        name="ffn1",
    )(x, g, wg, wu, wd)


def _rope128(x, cos_tab, sin_tab, x2_start):
    lane = lax.broadcasted_iota(jnp.int32, x.shape, 1)
    partner = jnp.where(lane < x2_start, pltpu.roll(x, LANES - ROPE_HALF, 1),
                        pltpu.roll(x, ROPE_HALF, 1))
    return x * cos_tab + partner * sin_tab


CELL_V_ROWS = M_DIM + 16


def _proj_body(cell_feature_major, h_ref, gmix_ref, wqs_ref, wkv_ref, gq_ref, wuq_ref,
               gkv_ref, wuk_ref, wuvt_ref, bias_ref, tab_ref, *refs):
    wmk_ref, wqvot_ref = refs[:2]
    if cell_feature_major:
        q_out, k_out, vt_out, ckv_out, kr_out, krt_out, mk_out, mqt_out, mvt_out, mot_out, gt_out = refs[2:]
    else:
        q_out, k_out, vt_out, ckv_out, kr_out, krt_out, mq_out, mk_out, mv_out, mo_out, gate_out = refs[2:]
    u = _rms(h_ref[...], gmix_ref[...]).astype(BF16)
    ckv = _rms(_dot_nt(u, wkv_ref[...]), gkv_ref[...])
    ckv_out[...] = ckv
    ckv_b = ckv.astype(BF16)
    zqs = _dot_nt(u, wqs_ref[...])
    zs = zqs[:, Q_LORA:Q_LORA + LANES]
    n_tok = zs.shape[0]
    cs = jnp.concatenate([tab_ref[...], jnp.zeros((LANES - ROPE_DIM, n_tok), F32)], axis=0).T
    lane = lax.broadcasted_iota(jnp.int32, zs.shape, 1)
    shifted = lambda by: pltpu.roll(cs, by, 1)
    cos_k = jnp.where(lane < ROPE_HALF, cs, jnp.where(lane < ROPE_DIM, shifted(ROPE_HALF), 0.0))
    sin_k = jnp.where(lane < ROPE_HALF, -shifted(LANES - ROPE_HALF), jnp.where(lane < ROPE_DIM, cs, 0.0))
    kr = _rope128(zs, cos_k, sin_k, ROPE_HALF)
    kr_out[...] = kr[:, :ROPE_DIM]
    krt_out[0] = kr.T[:ROPE_DIM, :]
    kn = _dot(ckv_b, wuk_ref[...])
    kr_head = pltpu.roll(kr, NOPE_DIM, 1)
    for hd in range(MLA_HEADS):
        lanes = slice(hd * HEAD_PAD, (hd + 1) * HEAD_PAD)
        k_out[:, lanes] = (kn[:, lanes] + kr_head).astype(BF16)
    x1 = (lane >= NOPE_DIM) & (lane < NOPE_DIM + ROPE_HALF)
    x2 = (lane >= NOPE_DIM + ROPE_HALF) & (lane < NOPE_DIM + ROPE_DIM)
    cos_q = jnp.where(lane < NOPE_DIM, 1.0, jnp.where(x1, shifted(NOPE_DIM), jnp.where(
        x2, shifted(NOPE_DIM + ROPE_HALF), 0.0)))
    sin_q = jnp.where(x1, -shifted(NOPE_DIM - ROPE_HALF), jnp.where(x2, shifted(NOPE_DIM), 0.0))
    vt = _dot_nt(wuvt_ref[...], ckv_b)
    vrow = lax.broadcasted_iota(jnp.int32, vt.shape, 0)
    vt_out[0, 0] = jnp.where(vrow % V_ROWS == V_DIM, 1.0, vt).astype(BF16)
    qn = _rms(zqs[:, :Q_LORA], gq_ref[...]).astype(BF16)
    q = _dot(qn, wuq_ref[...])
    cos_qs = cos_q * QK_SCALE
    sin_qs = sin_q * QK_SCALE
    for hd in range(MLA_HEADS):
        lanes = slice(hd * HEAD_PAD, (hd + 1) * HEAD_PAD)
        q_out[:, lanes] = _rope128(q[:, lanes], cos_qs, sin_qs, NOPE_DIM + ROPE_HALF).astype(BF16)
    zb = zs + bias_ref[...]
    log_sig = jnp.minimum(zb, 0.0) - jnp.log1p(jnp.exp(-jnp.abs(zb)))
    is_f = (lane >= GATE_F_LANE) & (lane < GATE_F_LANE + M_HEADS)
    gates = jnp.where(is_f, log_sig, zb)
    mk_out[...] = (_dot_nt(u, wmk_ref[...]) * (M_DIM ** -0.5)).astype(BF16)
    if cell_feature_major:
        zt = _dot_nt(wqvot_ref[...], u)
        mqt_out[0, 0] = zt[0:MLSTM_WIDTH].astype(BF16)
        unit_rows = (lax.broadcasted_iota(jnp.int32, (CELL_V_ROWS - M_DIM, n_tok), 0) == 0)
        for hd in range(M_HEADS):
            rows = slice(MLSTM_WIDTH + hd * M_DIM, MLSTM_WIDTH + (hd + 1) * M_DIM)
            mvt_out[0, 0, hd * CELL_V_ROWS:hd * CELL_V_ROWS + M_DIM, :] = zt[rows].astype(BF16)
            mvt_out[0, 0, hd * CELL_V_ROWS + M_DIM:(hd + 1) * CELL_V_ROWS, :] = (
                unit_rows.astype(F32).astype(BF16))
        mot_out[0, 0] = zt[2 * MLSTM_WIDTH:3 * MLSTM_WIDTH]
        gt_out[0, 0] = gates.T[GATE_I_LANE:GATE_I_LANE + 2 * M_HEADS, :]
    else:
        zm = _dot_nt(u, wqvot_ref[...])
        mq_out[...] = zm[:, 0:MLSTM_WIDTH].astype(BF16)
        mv_out[...] = zm[:, MLSTM_WIDTH:2 * MLSTM_WIDTH].astype(BF16)
        mo_out[...] = zm[:, 2 * MLSTM_WIDTH:3 * MLSTM_WIDTH]
        gate_out[...] = gates


def _proj_call(h, w, rope_tab, tm, cell_feature_major):
    t = h.shape[0]
    n_tab = rope_tab.shape[1] // tm
    n_seq = t // tm // n_tab
    row = lambda i: (i, 0)
    tab = lambda i: (0, i % n_tab)
    tok = lambda width, dtype: (jax.ShapeDtypeStruct((t, width), dtype),
                                pl.BlockSpec((tm, width), row))
    slab = lambda rows, dtype: (jax.ShapeDtypeStruct((n_seq, n_tab, rows, tm), dtype),
                                pl.BlockSpec((1, 1, rows, tm), lambda i: (i // n_tab, i % n_tab, 0, 0)))
    outs = [tok(MLA_PAD_WIDTH, BF16), tok(MLA_PAD_WIDTH, BF16), slab(MLA_HEADS * V_ROWS, BF16),
            tok(KV_LORA, F32), tok(ROPE_DIM, F32),
            (jax.ShapeDtypeStruct((n_seq, ROPE_DIM, n_tab * tm), F32),
             pl.BlockSpec((1, ROPE_DIM, tm), lambda i: (i // n_tab, 0, i % n_tab)))]
    if cell_feature_major:
        outs += [tok(MLSTM_WIDTH, BF16), slab(MLSTM_WIDTH, BF16), slab(M_HEADS * CELL_V_ROWS, BF16),
                 slab(MLSTM_WIDTH, F32), slab(2 * M_HEADS, F32)]
    else:
        outs += [tok(MLSTM_WIDTH, BF16)] * 3 + [tok(MLSTM_WIDTH, F32), tok(LANES, F32)]
    return pl.pallas_call(
        functools.partial(_proj_body, cell_feature_major),
        grid=(t // tm,),
        in_specs=[pl.BlockSpec((tm, D_MODEL), row), _resident((1, D_MODEL)),
                  _resident((Q_LORA + LANES, D_MODEL)), _resident((KV_LORA, D_MODEL)),
                  _resident((1, Q_LORA)), _resident((Q_LORA, MLA_PAD_WIDTH)),
                  _resident((1, KV_LORA)), _resident((KV_LORA, MLA_PAD_WIDTH)),
                  _resident((MLA_HEADS * V_ROWS, KV_LORA)),
                  _resident((1, LANES)),
                  pl.BlockSpec((ROPE_DIM, tm), tab),
                  _resident((MLSTM_WIDTH, D_MODEL)), _resident((3 * MLSTM_WIDTH, D_MODEL))],
        out_specs=[o[1] for o in outs],
        out_shape=[o[0] for o in outs],
        compiler_params=_params(1),
        name="proj",
    )(h, w["g_mix"], w["w_qs_t"], w["w_kv_t"], w["g_q"], w["w_uq"], w["g_kv"],
      w["w_uk"], w["w_uv_t"], w["gate_bias"], rope_tab, w["w_mk_t"], w["w_qvo_t"])


ATT_BLOCK = 512
ATT_HEADS = 8
ATT_LANES = ATT_HEADS * HEAD_PAD
ATT_V_ROWS = ATT_HEADS * V_ROWS
ATT_LOOKAHEAD = 3


def _flash_body(q_ref, k_ref, vt_ref, o_ref):
    qi = pl.program_id(2)
    blk = ATT_BLOCK
    head_lanes = [slice(hd * HEAD_PAD, (hd + 1) * HEAD_PAD) for hd in range(ATT_HEADS)]
    qs = [q_ref[0, :, lanes] for lanes in head_lanes]

    def step(j, carry, diagonal):
        start = pl.multiple_of(j * blk, blk)
        scores = [None] * ATT_HEADS
        if diagonal:
            key = lax.broadcasted_iota(jnp.int32, (blk, blk), 0)
            qry = lax.broadcasted_iota(jnp.int32, (blk, blk), 1)
            visible = key <= qry
        out = []
        for hd in range(ATT_HEADS):
            m, acc = carry[hd]
            for nxt in range(hd if hd else 0, min(hd + ATT_LOOKAHEAD, ATT_HEADS - 1) + 1):
                if scores[nxt] is None:
                    scores[nxt] = _dot_nt(k_ref[0, pl.ds(start, blk), head_lanes[nxt]], qs[nxt])
            s = jnp.where(visible, scores[hd], -jnp.inf) if diagonal else scores[hd]
            m_new = jnp.maximum(m, jnp.max(s, axis=0, keepdims=True))
            p = jnp.exp2(s - m_new).astype(BF16)
            vt = vt_ref[0, j, hd * V_ROWS:(hd + 1) * V_ROWS, :]
            acc = jnp.exp2(m - m_new) * acc + _dot(vt, p)
            out.append((m_new, acc))
        return tuple(out)

    init = tuple((jnp.full((1, blk), -jnp.inf, F32), jnp.zeros((V_ROWS, blk), F32))
                 for _ in range(ATT_HEADS))
    carry = lax.fori_loop(0, qi, lambda j, c: step(j, c, False), init)
    final = step(qi, carry, True)
    for pair in range(ATT_HEADS // 2):
        o_t = jnp.concatenate([final[hd][1][:V_DIM] / final[hd][1][V_DIM:V_DIM + 1]
                               for hd in (2 * pair, 2 * pair + 1)], axis=0)
        o_ref[0, :, pair * LANES:(pair + 1) * LANES] = o_t.T.astype(BF16)


def _flash_call(q, k, vt):
    b, s, _ = q.shape
    assert vt.shape == (b, s // ATT_BLOCK, MLA_HEADS * V_ROWS, ATT_BLOCK)
    qmap = lambda bi, gi, qi: (bi, qi, gi)
    return pl.pallas_call(
        _flash_body,
        grid=(b, MLA_HEADS // ATT_HEADS, s // ATT_BLOCK),
        in_specs=[pl.BlockSpec((1, ATT_BLOCK, ATT_LANES), qmap),
                  pl.BlockSpec((1, s, ATT_LANES), lambda bi, gi, qi: (bi, 0, gi),
                               pipeline_mode=pl.Buffered(1)),
                  pl.BlockSpec((1, s // ATT_BLOCK, ATT_V_ROWS, ATT_BLOCK),
                               lambda bi, gi, qi: (bi, 0, gi, 0), pipeline_mode=pl.Buffered(1))],
        out_specs=pl.BlockSpec((1, ATT_BLOCK, ATT_HEADS * V_DIM), qmap),
        out_shape=jax.ShapeDtypeStruct((b, s, MLA_WIDTH), BF16),
        compiler_params=_params(3),
        name="prompt_attn",
    )(q, k, vt)


def _lane_scan(x, op, fill):
    lane = lax.broadcasted_iota(jnp.int32, x.shape, 1)
    shift = 1
    while shift < x.shape[1]:
        x = op(x, jnp.where(lane >= shift, pltpu.roll(x, shift, 1), fill))
        shift *= 2
    return x


def _mlstm_body(k_ref, qt_ref, vt_ref, ot_ref, gt_ref, gm_ref, hm_out, cxt_out, m_out, cxt_s, m_s):
    ci = pl.program_id(0)
    n_seq = k_ref.shape[0]
    L = CHUNK

    @pl.when(ci == 0)
    def _():
        cxt_s[...] = jnp.zeros_like(cxt_s)
        m_s[...] = jnp.zeros_like(m_s)

    s_idx = lax.broadcasted_iota(jnp.int32, (L, L), 0)
    t_idx = lax.broadcasted_iota(jnp.int32, (L, L), 1)
    causal = s_idx <= t_idx

    streams = []
    for sq in range(n_seq):
        rows = gt_ref[sq, 0]
        b_rows = pltpu.roll(_lane_scan(rows, jnp.add, 0.0), M_HEADS, 0)
        head_row = lax.broadcasted_iota(jnp.int32, rows.shape, 0) < M_HEADS
        a_rows = jnp.where(head_row, rows - b_rows, 0.0)
        run_max = _lane_scan(a_rows, jnp.maximum, -jnp.inf)
        top = jnp.broadcast_to(jnp.max(a_rows, axis=1, keepdims=True), rows.shape)
        b_last = pltpu.roll(jnp.broadcast_to(jnp.sum(rows, axis=1, keepdims=True), rows.shape),
                            M_HEADS, 0)
        src = jnp.exp(a_rows - top)
        a_cols = jnp.concatenate([a_rows, jnp.zeros((L - SUBLANES, L), F32)], axis=0).T
        for hd in range(M_HEADS):
            feat = slice(hd * M_DIM, (hd + 1) * M_DIM)
            idx = sq * M_HEADS + hd
            k = k_ref[sq, :, feat]
            qt = qt_ref[sq, 0, feat, :]
            vt = vt_ref[sq, 0, hd * CELL_V_ROWS:(hd + 1) * CELL_V_ROWS, :]
            m_prev = m_s[idx][0:1, :]
            big_m = jnp.maximum(m_prev, run_max[hd:hd + 1, :])
            decay = jnp.where(causal, jnp.exp(a_cols[:, hd:hd + 1] - big_m), 0.0)
            cxt_prev = cxt_s[idx]
            weighted_vt = (vt.astype(F32) * src[hd:hd + 1, :]).astype(BF16)
            streams.append(dict(
                sq=sq, feat=feat, idx=idx, m_prev=m_prev, cxt_prev=cxt_prev, vt=vt,
                w_inter=jnp.exp(m_prev - big_m),
                floor=jnp.exp(-(b_rows[hd:hd + 1, :] + big_m)),
                top=top[hd:hd + 1, :], b_last=b_last[hd:hd + 1, :],
                sqk=_dot(k, qt) * decay,
                qe=_dot(cxt_prev.astype(BF16), qt),
                kv=_dot(weighted_vt, k)))

    for st in streams:
        ue = _dot(st["vt"], st["sqk"].astype(BF16))
        num = ue[:M_DIM] + st["qe"][:M_DIM] * st["w_inter"]
        den = ue[M_DIM:M_DIM + 1] + st["qe"][M_DIM:M_DIM + 1] * st["w_inter"]
        inv = 1.0 / jnp.maximum(jnp.abs(den), st["floor"])
        t = jax.nn.sigmoid(ot_ref[st["sq"], 0, st["feat"], :]) * num
        ms = jnp.sum(t * t, axis=0, keepdims=True) * (1.0 / M_DIM)
        scale = inv * lax.rsqrt(ms * inv * inv + EPS)
        hm_out[st["sq"], :, st["feat"]] = (t * scale * gm_ref[st["feat"], :]).T.astype(BF16)
        m_last = jnp.maximum(st["m_prev"], st["top"])
        keep = jnp.exp(st["m_prev"] - m_last)
        gain = jnp.exp(st["top"] - m_last)
        cxt_s[st["idx"]] = keep * st["cxt_prev"] + gain * st["kv"]
        m_s[st["idx"]] = jnp.broadcast_to(st["b_last"] + m_last, (SUBLANES, LANES))

    @pl.when(ci == pl.num_programs(0) - 1)
    def _():
        cxt_out[...] = cxt_s[...]
        m_out[...] = m_s[...]


def _mlstm_call(mk, mqt, mvt, mot, gt, gm_lanes):
    b, s, _ = mk.shape
    tile = mqt.shape[-1]
    per_tile = tile // CHUNK
    n_streams = b * M_HEADS
    slab = lambda rows: pl.BlockSpec((b, 1, rows, CHUNK),
                                     lambda ci: (0, ci // per_tile, 0, ci % per_tile))
    tok = pl.BlockSpec((b, CHUNK, MLSTM_WIDTH), lambda ci: (0, ci, 0))
    whole3 = lambda ci: (0, 0, 0)
    return pl.pallas_call(
        _mlstm_body,
        grid=(s // CHUNK,),
        in_specs=[tok, slab(MLSTM_WIDTH), slab(M_HEADS * CELL_V_ROWS), slab(MLSTM_WIDTH),
                  slab(2 * M_HEADS), _resident((MLSTM_WIDTH, LANES))],
        out_specs=[tok,
                   pl.BlockSpec((n_streams, CELL_V_ROWS, M_DIM), whole3),
                   pl.BlockSpec((n_streams, SUBLANES, LANES), whole3)],
        out_shape=[jax.ShapeDtypeStruct((b, s, MLSTM_WIDTH), BF16),
                   jax.ShapeDtypeStruct((n_streams, CELL_V_ROWS, M_DIM), F32),
                   jax.ShapeDtypeStruct((n_streams, SUBLANES, LANES), F32)],
        scratch_shapes=[pltpu.VMEM((n_streams, CELL_V_ROWS, M_DIM), F32),
                        pltpu.VMEM((n_streams, SUBLANES, LANES), F32)],
        compiler_params=_params(1),
        name="prompt_mlstm",
    )(mk, mqt, mvt, mot, gt, gm_lanes)


def _merge_body(h_ref, a_ref, hm_ref, p_ref, ga_ref, woa_ref, wom_ref, gff_ref, wg_ref, wu_ref,
                wd_ref, gple_ref, wpg_ref, wpp_ref, gfin_ref, y_ref):
    a = a_ref[...].astype(F32)
    an = _rms(a, ga_ref[...]).astype(BF16)
    h = h_ref[...] + _dot(an, woa_ref[...]) + _dot(hm_ref[...], wom_ref[...])
    h = h + _swiglu_half(_rms(h, gff_ref[...]).astype(BF16), wg_ref, wu_ref, wd_ref)
    gate = jax.nn.sigmoid(_dot(_rms(h, gple_ref[...]).astype(BF16), wpg_ref[...]))
    h = h + gate * _dot(p_ref[...].astype(BF16), wpp_ref[...])
    y_ref[...] = _rms(h, gfin_ref[...])


def _merge_call(h, a, hm, p, w, tm):
    t = h.shape[0]
    row = lambda i: (i, 0)
    return pl.pallas_call(
        _merge_body,
        grid=(t // tm,),
        in_specs=[pl.BlockSpec((tm, D_MODEL), row), pl.BlockSpec((tm, MLA_WIDTH), row),
                  pl.BlockSpec((tm, MLSTM_WIDTH), row), pl.BlockSpec((tm, PLE_DIM), row),
                  _resident((1, MLA_WIDTH)), _resident((MLA_WIDTH, D_MODEL)),
                  _resident((MLSTM_WIDTH, D_MODEL)), _resident((1, D_MODEL)),
                  _resident((D_MODEL, D_FF)), _resident((D_MODEL, D_FF)), _resident((D_FF, D_MODEL)),
                  _resident((1, D_MODEL)), _resident((D_MODEL, D_MODEL)),
                  _resident((PLE_DIM, D_MODEL)), _resident((1, D_MODEL))],
        out_specs=pl.BlockSpec((tm, D_MODEL), row),
        out_shape=jax.ShapeDtypeStruct((t, D_MODEL), F32),
        compiler_params=_params(1),
        name="merge",
    )(h, a, hm, p, w["g_attn"], w["w_out_a"], w["w_out_m"], w["g_ff2"], w["w_ff2_gate"],
      w["w_ff2_up"], w["w_ff2_down"], w["g_ple"], w["w_ple_gate"], w["w_ple_proj"], w["g_final"])


N_SLOTS = 2
SAMPLE_CHUNKS = 4
SAMPLE_CHUNK_PAGES = N_PAGES // SAMPLE_CHUNKS
SAMPLE_CHUNK_KEYS = SAMPLE_CHUNK_PAGES * PAGE_SIZE


def _sample_attn_body(pt_ref, q_ref, ckv_ref, kr_ref, cache_c, cache_r, wabs_ref, wuv_ref,
                      o_ref, cbuf, rbuf, sem_c, sem_r):
    b = pl.program_id(0)
    nb = pl.num_programs(0)
    slot = b % N_SLOTS

    def copies(bi, sl):
        out = []
        for j in range(N_PAGES):
            page = pt_ref[bi, j]
            out.append(pltpu.make_async_copy(cache_c.at[page], cbuf.at[sl, j], sem_c.at[sl]))
            out.append(pltpu.make_async_copy(
                cache_r.at[page], rbuf.at[sl, :, pl.ds(j * PAGE_SIZE, PAGE_SIZE)], sem_r.at[sl]))
        return out

    @pl.when(b == 0)
    def _():
        for bi in range(min(N_SLOTS, DEC_BATCH)):
            for cp in copies(bi, bi):
                cp.start()

    q_row = q_ref[0]
    sub = lax.broadcasted_iota(jnp.int32, (MLA_HEADS, MLA_PAD_WIDTH), 0)
    lane = lax.broadcasted_iota(jnp.int32, (MLA_HEADS, MLA_PAD_WIDTH), 1)
    own = (lane // HEAD_PAD) == sub
    q_bd = jnp.where(own, jnp.broadcast_to(q_row.astype(F32), own.shape), 0.0).astype(BF16)
    q_ext = _dot(q_bd, wabs_ref[...])
    q_abs = q_ext[:, :KV_LORA].astype(BF16)
    q_rope = q_ext[:, KV_LORA:KV_LORA + ROPE_DIM].astype(BF16)

    c_new = ckv_ref[0].astype(BF16).astype(F32)
    r_new = kr_ref[0].astype(BF16).astype(F32)
    s_new = (jnp.sum(q_abs.astype(F32) * c_new, axis=1, keepdims=True)
             + jnp.sum(q_rope.astype(F32) * r_new, axis=1, keepdims=True))

    for cp in copies(b, slot):
        cp.wait()

    def chunk_keys(i):
        pages = slice(i * SAMPLE_CHUNK_PAGES, (i + 1) * SAMPLE_CHUNK_PAGES)
        keys = slice(i * SAMPLE_CHUNK_KEYS, (i + 1) * SAMPLE_CHUNK_KEYS)
        kc = cbuf[slot, pages].reshape(SAMPLE_CHUNK_KEYS, KV_LORA).astype(BF16)
        return kc, _dot_nt(q_abs, kc) + _dot(q_rope, rbuf[slot, :, keys].astype(BF16))

    m = s_new
    l = jnp.ones_like(s_new)
    acc = jnp.broadcast_to(c_new, (MLA_HEADS, KV_LORA))
    kc, s = chunk_keys(0)
    for i in range(SAMPLE_CHUNKS):
        nxt = chunk_keys(i + 1) if i + 1 < SAMPLE_CHUNKS else None
        m_new = jnp.maximum(m, jnp.max(s, axis=1, keepdims=True))
        alpha = jnp.exp2(m - m_new)
        p = jnp.exp2(s - m_new)
        l = alpha * l + jnp.sum(p, axis=1, keepdims=True)
        acc = alpha * acc + _dot(p.astype(BF16), kc)
        m = m_new
        if nxt is not None:
            kc, s = nxt
    o_lat = (acc / l).astype(BF16)
    res = _dot(o_lat, wuv_ref[...])
    own_v = (lax.broadcasted_iota(jnp.int32, res.shape, 1) // V_DIM
             == lax.broadcasted_iota(jnp.int32, res.shape, 0))
    o_ref[0] = jnp.sum(jnp.where(own_v, res, 0.0), axis=0, keepdims=True).astype(BF16)

    @pl.when(b + N_SLOTS < nb)
    def _():
        for cp in copies(b + N_SLOTS, slot):
            cp.start()


def _sample_attn_call(page_table, q, ckv, kr, cache_c, cache_r, w_abs, w_uv):
    nb = q.shape[0]
    tok = lambda bi, pt: (bi, 0, 0)
    whole = lambda shape: pl.BlockSpec(shape, lambda bi, pt: (0,) * len(shape),
                                       pipeline_mode=pl.Buffered(1))
    grid_spec = pltpu.PrefetchScalarGridSpec(
        num_scalar_prefetch=1,
        grid=(nb,),
        in_specs=[pl.BlockSpec((1, 1, MLA_PAD_WIDTH), tok), pl.BlockSpec((1, 1, KV_LORA), tok),
                  pl.BlockSpec((1, 1, ROPE_DIM), tok),
                  pl.BlockSpec(memory_space=pl.ANY), pl.BlockSpec(memory_space=pl.ANY),
                  whole((MLA_PAD_WIDTH, KV_LORA + LANES)), whole((KV_LORA, MLA_WIDTH))],
        out_specs=pl.BlockSpec((1, 1, MLA_WIDTH), tok),
        scratch_shapes=[pltpu.VMEM((N_SLOTS, N_PAGES, PAGE_SIZE, KV_LORA), F32),
                        pltpu.VMEM((N_SLOTS, ROPE_DIM, PAST_LEN), F32),
                        pltpu.SemaphoreType.DMA((N_SLOTS,)), pltpu.SemaphoreType.DMA((N_SLOTS,))])
    return pl.pallas_call(
        _sample_attn_body,
        grid_spec=grid_spec,
        out_shape=jax.ShapeDtypeStruct((nb, 1, MLA_WIDTH), BF16),
        compiler_params=_params(1),
        name="sample_attn",
    )(page_table, q, ckv, kr, cache_c, cache_r, w_abs, w_uv)


SAMPLE_ROWS = 32


def _sample_mlstm_body(q_ref, k_ref, v_ref, o_ref, g_ref, gm_ref, c_ref, n_ref, m_ref,
                       hm_out, c_out, n_out, m_out):
    hd = pl.program_id(1)
    g = g_ref[...]
    lane = lax.broadcasted_iota(jnp.int32, g.shape, 1)
    ig = jnp.sum(jnp.where(lane == GATE_I_LANE + hd, g, 0.0), axis=1, keepdims=True)
    lf = jnp.sum(jnp.where(lane == GATE_F_LANE + hd, g, 0.0), axis=1, keepdims=True)
    m_all = m_ref[...]
    head_lane = lax.broadcasted_iota(jnp.int32, m_all.shape, 1)
    m0 = jnp.sum(jnp.where(head_lane == hd, m_all, 0.0), axis=1, keepdims=True)
    q = q_ref[...].astype(F32)
    k = k_ref[...].astype(F32)
    v = v_ref[...].astype(F32)
    n0 = n_ref[...]
    m_new = jnp.maximum(lf + m0, ig)
    keep = jnp.exp(lf + m0 - m_new)
    w_src = jnp.exp(ig - m_new)
    sqk = jnp.sum(q * k, axis=1, keepdims=True) * w_src
    wk = w_src * k
    qc = jnp.zeros_like(v)
    for d in range(M_DIM):
        c_d = c_ref[:, 0, d, :]
        qc = qc + q[:, d:d + 1] * c_d
        c_out[:, 0, d, :] = keep * c_d + wk[:, d:d + 1] * v
    num = sqk * v + qc * keep
    den = sqk + jnp.sum(q * n0, axis=1, keepdims=True) * keep
    den = jnp.maximum(jnp.abs(den), jnp.exp(-m_new))
    hh = jax.nn.sigmoid(o_ref[...]) * (num / den)
    hm_out[...] = _rms(hh, gm_ref[...]).astype(BF16)
    n_out[...] = keep * n0 + wk

    @pl.when(hd == 0)
    def _():
        m_out[...] = jnp.broadcast_to(m_new, m_all.shape)

    @pl.when(hd > 0)
    def _():
        m_out[...] = jnp.where(head_lane == hd, m_new, m_out[...])


def _sample_mlstm_call(mq, mk, mv, mo, gates, gm, c0, n0, m0):
    nb = mq.shape[0]
    rows = SAMPLE_ROWS
    tok = lambda bi, hi: (bi, hi)
    return pl.pallas_call(
        _sample_mlstm_body,
        grid=(nb // rows, M_HEADS),
        in_specs=[pl.BlockSpec((rows, M_DIM), tok)] * 4
                 + [pl.BlockSpec((rows, LANES), lambda bi, hi: (bi, 0)),
                    pl.BlockSpec((1, M_DIM), lambda bi, hi: (0, hi)),
                    pl.BlockSpec((rows, 1, M_DIM, M_DIM), lambda bi, hi: (bi, hi, 0, 0)),
                    pl.BlockSpec((rows, M_DIM), tok),
                    pl.BlockSpec((rows, M_HEADS), lambda bi, hi: (bi, 0))],
        out_specs=[pl.BlockSpec((rows, M_DIM), tok),
                   pl.BlockSpec((rows, 1, M_DIM, M_DIM), lambda bi, hi: (bi, hi, 0, 0)),
                   pl.BlockSpec((rows, M_DIM), tok),
                   pl.BlockSpec((rows, M_HEADS), lambda bi, hi: (bi, 0))],
        out_shape=[jax.ShapeDtypeStruct((nb, MLSTM_WIDTH), BF16),
                   jax.ShapeDtypeStruct((nb, M_HEADS, M_DIM, M_DIM), F32),
                   jax.ShapeDtypeStruct((nb, MLSTM_WIDTH), F32),
                   jax.ShapeDtypeStruct((nb, M_HEADS), F32)],
        compiler_params=_params(2),
        name="sample_mlstm",
    )(mq, mk, mv, mo, gates, gm, c0, n0, m0)


def _pad_heads(w, head_dim):
    rows = w.shape[0]
    w = w.reshape(rows, MLA_HEADS, head_dim)
    w = jnp.pad(w, ((0, 0), (0, 0), (0, HEAD_PAD - head_dim)))
    return w.reshape(rows, MLA_PAD_WIDTH)


def _rope_tables(pos):
    inv = ROPE_THETA ** (-jnp.arange(ROPE_HALF, dtype=F32) / ROPE_HALF)
    ang = inv[:, None] * pos.astype(F32)[None, :]
    return jnp.concatenate([jnp.cos(ang), jnp.sin(ang)], axis=0)


def _prep_weights(g_ff1, w_ff1_gate, w_ff1_up, w_ff1_down, g_mix, w_in, g_q, w_uq, g_kv, w_uk,
                  w_uv, b_gate_i, b_gate_f, g_attn_out, g_mlstm_out, w_out, g_ff2, w_ff2_gate,
                  w_ff2_up, w_ff2_down, g_ple, w_ple_gate, w_ple_proj, g_final):
    bf = lambda a: a.astype(BF16)
    row = lambda a: a.reshape(1, -1).astype(F32)
    off_kv, off_kr = Q_LORA, Q_LORA + KV_LORA
    off_m = off_kr + ROPE_DIM
    off_i = off_m + 4 * MLSTM_WIDTH
    small_pad = LANES - ROPE_DIM - 2 * M_HEADS
    w_t = bf(w_in.T)
    w_qs_t = jnp.concatenate([w_t[:off_kv], w_t[off_kr:off_m], w_t[off_i:off_i + 2 * M_HEADS],
                              jnp.zeros((small_pad, D_MODEL), BF16)], axis=0)
    w_qvo_t = jnp.concatenate([w_t[off_m:off_m + MLSTM_WIDTH], w_t[off_m + 2 * MLSTM_WIDTH:off_i]],
                              axis=0)
    gate_bias = jnp.concatenate([jnp.zeros((ROPE_DIM,), F32), b_gate_i, b_gate_f,
                                 jnp.zeros((small_pad,), F32)])
    src = jnp.arange(LANES)[:, None]
    dst = jnp.arange(MLA_PAD_WIDTH)[None, :]
    w_abs = jnp.pad(w_uk.reshape(KV_LORA, MLA_HEADS, NOPE_DIM).transpose(1, 2, 0),
                    ((0, 0), (0, HEAD_PAD - NOPE_DIM), (0, 0))).reshape(MLA_PAD_WIDTH, KV_LORA)
    rope_sel = (dst.T % HEAD_PAD) == (src.T + NOPE_DIM)
    rope_sel = rope_sel & (src.T < ROPE_DIM)
    w_abs = jnp.concatenate([w_abs, rope_sel.astype(F32)], axis=1)
    return dict(
        g_ff1=row(g_ff1), w_ff1_gate=bf(w_ff1_gate), w_ff1_up=bf(w_ff1_up), w_ff1_down=bf(w_ff1_down),
        g_mix=row(g_mix), w_qs_t=w_qs_t, w_kv_t=w_t[off_kv:off_kr],
        w_mk_t=w_t[off_m + MLSTM_WIDTH:off_m + 2 * MLSTM_WIDTH], w_qvo_t=w_qvo_t,
        g_q=row(g_q), w_uq=bf(_pad_heads(w_uq, NOPE_DIM + ROPE_DIM)),
        g_kv=row(g_kv), w_uk=bf(_pad_heads(w_uk, NOPE_DIM)), w_uv=bf(w_uv),
        w_uv_t=bf(jnp.pad(w_uv.T.reshape(MLA_HEADS, V_DIM, KV_LORA),
                          ((0, 0), (0, V_ROWS - V_DIM), (0, 0))).reshape(MLA_HEADS * V_ROWS, KV_LORA)),
        gate_bias=row(gate_bias), w_abs=bf(w_abs),
        g_attn=row(g_attn_out), w_out_a=bf(w_out[:MLA_WIDTH]),
        w_out_m=bf(w_out[MLA_WIDTH:]), g_mlstm=row(g_mlstm_out),
        g_mlstm_lanes=jnp.broadcast_to(g_mlstm_out.reshape(MLSTM_WIDTH, 1).astype(F32),
                                       (MLSTM_WIDTH, LANES)),
        g_ff2=row(g_ff2), w_ff2_gate=bf(w_ff2_gate), w_ff2_up=bf(w_ff2_up), w_ff2_down=bf(w_ff2_down),
        g_ple=row(g_ple), w_ple_gate=bf(w_ple_gate), w_ple_proj=bf(w_ple_proj), g_final=row(g_final))


PROMPT_TILE = 512


def kernel(x_prompt, x_sample, p_prompt, p_sample, cache_ckv, cache_krope, state_C, state_n, state_m, page_table, g_ff1, w_ff1_gate, w_ff1_up, w_ff1_down, g_mix, w_in, g_q, w_uq, g_kv, w_uk, w_uv, b_gate_i, b_gate_f, g_attn_out, g_mlstm_out, w_out, g_ff2, w_ff2_gate, w_ff2_up, w_ff2_down, g_ple, w_ple_gate, w_ple_proj, g_final):
    assert w_in.shape[0] == 1, "single-layer trunk"
    w = _prep_weights(g_ff1[0], w_ff1_gate[0], w_ff1_up[0], w_ff1_down[0], g_mix[0], w_in[0],
                      g_q[0], w_uq[0], g_kv[0], w_uk[0], w_uv[0], b_gate_i[0], b_gate_f[0],
                      g_attn_out[0], g_mlstm_out[0], w_out[0], g_ff2[0], w_ff2_gate[0],
                      w_ff2_up[0], w_ff2_down[0], g_ple[0], w_ple_gate[0], w_ple_proj[0], g_final)
    nb_p, seq, _ = x_prompt.shape
    nb_s = x_sample.shape[0]
    t_p = nb_p * seq

    tab_p = _rope_tables(jnp.arange(seq))
    h_p = _ffn_call(x_prompt.reshape(t_p, D_MODEL), w["g_ff1"], w["w_ff1_gate"], w["w_ff1_up"],
                    w["w_ff1_down"], PROMPT_TILE)
    q_p, k_p, vt_p, ckv_p, _, krt_p, mk_p, mqt_p, mvt_p, mot_p, gt_p = _proj_call(
        h_p, w, tab_p, PROMPT_TILE, cell_feature_major=True)
    seq3 = lambda a: a.reshape(nb_p, seq, a.shape[-1])
    a_p = _flash_call(seq3(q_p), seq3(k_p), vt_p)
    hm_p, cxt_p, m_p = _mlstm_call(seq3(mk_p), mqt_p, mvt_p, mot_p, gt_p, w["g_mlstm_lanes"])
    cxt_p = cxt_p.reshape(nb_p, M_HEADS, CELL_V_ROWS, M_DIM)
    y_p = _merge_call(h_p, a_p.reshape(t_p, MLA_WIDTH), hm_p.reshape(t_p, MLSTM_WIDTH),
                      p_prompt.reshape(t_p, PLE_DIM), w, PROMPT_TILE)

    tab_s = _rope_tables(jnp.full((nb_s,), PAST_LEN, jnp.int32))
    h_s = _ffn_call(x_sample.reshape(nb_s, D_MODEL), w["g_ff1"], w["w_ff1_gate"], w["w_ff1_up"],
                    w["w_ff1_down"], nb_s)
    q_s, _, _, ckv_s, kr_s, krt_s, mq_s, mk_s, mv_s, mo_s, gates_s = _proj_call(
        h_s, w, tab_s, nb_s, cell_feature_major=False)
    n_phys = cache_ckv.shape[1]
    a_s = _sample_attn_call(
        page_table, q_s.reshape(nb_s, 1, MLA_PAD_WIDTH), ckv_s.reshape(nb_s, 1, KV_LORA),
        kr_s.reshape(nb_s, 1, ROPE_DIM), cache_ckv.reshape(n_phys, PAGE_SIZE, KV_LORA),
        jnp.swapaxes(cache_krope.reshape(n_phys, PAGE_SIZE, ROPE_DIM), 1, 2), w["w_abs"], w["w_uv"])
    hm_s, c_s, n_s, m_s = _sample_mlstm_call(
        mq_s, mk_s, mv_s, mo_s, gates_s, w["g_mlstm"], state_C[0].astype(F32),
        state_n[0].astype(F32).reshape(nb_s, MLSTM_WIDTH), state_m[0].astype(F32))
    y_s = _merge_call(h_s, a_s.reshape(nb_s, MLA_WIDTH), hm_s, p_sample.reshape(nb_s, PLE_DIM),
                      w, nb_s)

    return (y_p.reshape(nb_p, seq, D_MODEL), y_s.reshape(nb_s, 1, D_MODEL),
            ckv_p.reshape(1, nb_p, seq, KV_LORA), jnp.swapaxes(krt_p, 1, 2)[None],
            jnp.swapaxes(cxt_p[:, :, :M_DIM, :], 2, 3)[None], cxt_p[None, :, :, M_DIM, :],
            m_p[:, 0, 0].reshape(1, nb_p, M_HEADS),
            ckv_s.reshape(1, nb_s, 1, KV_LORA), jnp.swapaxes(krt_s, 1, 2).reshape(1, nb_s, 1, ROPE_DIM),
            c_s[None], n_s.reshape(1, nb_s, M_HEADS, M_DIM), m_s[None])
```

```python
import functools
import math

import jax
import jax.numpy as jnp
from jax import lax
from jax.experimental import pallas as pl
from jax.experimental.pallas import tpu as pltpu

F32 = jnp.float32
BF16 = jnp.bfloat16

D_MODEL = 1024
SEQ = 8192
DEC_BATCH = 128
PAST_LEN = 8192
PAGE_SIZE = 128
N_PAGES = PAST_LEN // PAGE_SIZE
MLA_HEADS = 8
Q_LORA = 384
KV_LORA = 256
NOPE_DIM = 64
ROPE_DIM = 32
ROPE_HALF = ROPE_DIM // 2
V_DIM = 64
ROPE_THETA = 10000.0
M_HEADS = 4
M_DIM = 128
CHUNK = 128
MLSTM_WIDTH = M_HEADS * M_DIM
D_FF = 2816
PLE_DIM = 256
EPS = 1e-6

LANES = 128
SUBLANES = 8

HEAD_PAD = LANES
MLA_PAD_WIDTH = MLA_HEADS * HEAD_PAD
MLA_WIDTH = MLA_HEADS * V_DIM
V_ROWS = 80
GATE_I_LANE = ROPE_DIM
GATE_F_LANE = ROPE_DIM + M_HEADS
QK_SCALE = (NOPE_DIM + ROPE_DIM) ** -0.5 * math.log2(math.e)

VMEM_LIMIT = 56 * 1024 * 1024


def _dot(a, b):
    return jnp.dot(a, b, preferred_element_type=F32)


def _dot_nt(a, b):
    return lax.dot_general(a, b, (((1,), (1,)), ((), ())), preferred_element_type=F32)


def _rms(x, g):
    ms = jnp.sum(x * x, axis=-1, keepdims=True) * (1.0 / x.shape[-1])
    return x * lax.rsqrt(ms + EPS) * g


def _resident(shape):
    return pl.BlockSpec(shape, lambda *_: (0,) * len(shape), pipeline_mode=pl.Buffered(1))


def _params(n_axes):
    return pltpu.CompilerParams(dimension_semantics=("arbitrary",) * n_axes,
                                vmem_limit_bytes=VMEM_LIMIT)


MXU_TILE = 256
FF_CHUNK_BOUNDS = (0, 6 * MXU_TILE, D_FF)
assert D_FF % MXU_TILE == 0


def _swiglu_half(xn, wg_ref, wu_ref, wd_ref):
    out = None
    for lo, hi in zip(FF_CHUNK_BOUNDS[:-1], FF_CHUNK_BOUNDS[1:]):
        cols = slice(lo, hi)
        gate = _dot(xn, wg_ref[:, cols])
        up = _dot(xn, wu_ref[:, cols])
        act = (jax.nn.silu(gate) * up).astype(BF16)
        part = _dot(act, wd_ref[cols, :])
        out = part if out is None else out + part
    return 0.5 * out


def _ffn_body(x_ref, g_ref, wg_ref, wu_ref, wd_ref, o_ref):
    x = x_ref[...]
    xn = _rms(x, g_ref[...]).astype(BF16)
    o_ref[...] = x + _swiglu_half(xn, wg_ref, wu_ref, wd_ref)


def _ffn_call(x, g, wg, wu, wd, tm):
    t = x.shape[0]
    row = lambda i: (i, 0)
    return pl.pallas_call(
        _ffn_body,
        grid=(t // tm,),
        in_specs=[pl.BlockSpec((tm, D_MODEL), row), _resident((1, D_MODEL)),
                  _resident((D_MODEL, D_FF)), _resident((D_MODEL, D_FF)), _resident((D_FF, D_MODEL))],
        out_specs=pl.BlockSpec((tm, D_MODEL), row),
        out_shape=jax.ShapeDtypeStruct((t, D_MODEL), F32),
        compiler_params=_params(1),
        name="ffn1",
    )(x, g, wg, wu, wd)


def _rope128(x, cos_tab, sin_tab, x2_start):
    lane = lax.broadcasted_iota(jnp.int32, x.shape, 1)
    partner = jnp.where(lane < x2_start, pltpu.roll(x, LANES - ROPE_HALF, 1),
                        pltpu.roll(x, ROPE_HALF, 1))
    return x * cos_tab + partner * sin_tab


CELL_V_ROWS = M_DIM + 16


def _proj_body(cell_feature_major, h_ref, gmix_ref, wqs_ref, wkv_ref, gq_ref, wuq_ref,
               gkv_ref, wuk_ref, wuvt_ref, bias_ref, tab_ref, *refs):
    wmk_ref, wqvot_ref = refs[:2]
    if cell_feature_major:
        q_out, k_out, vt_out, ckv_out, kr_out, krt_out, mk_out, mqt_out, mvt_out, mot_out, gt_out = refs[2:]
    else:
        q_out, k_out, vt_out, ckv_out, kr_out, krt_out, mq_out, mk_out, mv_out, mo_out, gate_out = refs[2:]
    u = _rms(h_ref[...], gmix_ref[...]).astype(BF16)
    ckv = _rms(_dot_nt(u, wkv_ref[...]), gkv_ref[...])
    ckv_out[...] = ckv
    ckv_b = ckv.astype(BF16)
    zqs = _dot_nt(u, wqs_ref[...])
    zs = zqs[:, Q_LORA:Q_LORA + LANES]
    n_tok = zs.shape[0]
    cs = jnp.concatenate([tab_ref[...], jnp.zeros((LANES - ROPE_DIM, n_tok), F32)], axis=0).T
    lane = lax.broadcasted_iota(jnp.int32, zs.shape, 1)
    shifted = lambda by: pltpu.roll(cs, by, 1)
    cos_k = jnp.where(lane < ROPE_HALF, cs, jnp.where(lane < ROPE_DIM, shifted(ROPE_HALF), 0.0))
    sin_k = jnp.where(lane < ROPE_HALF, -shifted(LANES - ROPE_HALF), jnp.where(lane < ROPE_DIM, cs, 0.0))
    kr = _rope128(zs, cos_k, sin_k, ROPE_HALF)
    kr_out[...] = kr[:, :ROPE_DIM]
    krt_out[0] = kr.T[:ROPE_DIM, :]
    kn = _dot(ckv_b, wuk_ref[...])
    kr_head = pltpu.roll(kr, NOPE_DIM, 1)
    for hd in range(MLA_HEADS):
        lanes = slice(hd * HEAD_PAD, (hd + 1) * HEAD_PAD)
        k_out[:, lanes] = (kn[:, lanes] + kr_head).astype(BF16)
    x1 = (lane >= NOPE_DIM) & (lane < NOPE_DIM + ROPE_HALF)
    x2 = (lane >= NOPE_DIM + ROPE_HALF) & (lane < NOPE_DIM + ROPE_DIM)
    cos_q = jnp.where(lane < NOPE_DIM, 1.0, jnp.where(x1, shifted(NOPE_DIM), jnp.where(
        x2, shifted(NOPE_DIM + ROPE_HALF), 0.0)))
    sin_q = jnp.where(x1, -shifted(NOPE_DIM - ROPE_HALF), jnp.where(x2, shifted(NOPE_DIM), 0.0))
    vt = _dot_nt(wuvt_ref[...], ckv_b)
    vrow = lax.broadcasted_iota(jnp.int32, vt.shape, 0)
    vt_out[0, 0] = jnp.where(vrow % V_ROWS == V_DIM, 1.0, vt).astype(BF16)
    qn = _rms(zqs[:, :Q_LORA], gq_ref[...]).astype(BF16)
    q = _dot(qn, wuq_ref[...])
    cos_qs = cos_q * QK_SCALE
    sin_qs = sin_q * QK_SCALE
    for hd in range(MLA_HEADS):
        lanes = slice(hd * HEAD_PAD, (hd + 1) * HEAD_PAD)
        q_out[:, lanes] = _rope128(q[:, lanes], cos_qs, sin_qs, NOPE_DIM + ROPE_HALF).astype(BF16)
    zb = zs + bias_ref[...]
    log_sig = jnp.minimum(zb, 0.0) - jnp.log1p(jnp.exp(-jnp.abs(zb)))
    is_f = (lane >= GATE_F_LANE) & (lane < GATE_F_LANE + M_HEADS)
    gates = jnp.where(is_f, log_sig, zb)
    mk_out[...] = (_dot_nt(u, wmk_ref[...]) * (M_DIM ** -0.5)).astype(BF16)
    if cell_feature_major:
        zt = _dot_nt(wqvot_ref[...], u)
        mqt_out[0, 0] = zt[0:MLSTM_WIDTH].astype(BF16)
        unit_rows = (lax.broadcasted_iota(jnp.int32, (CELL_V_ROWS - M_DIM, n_tok), 0) == 0)
        for hd in range(M_HEADS):
            rows = slice(MLSTM_WIDTH + hd * M_DIM, MLSTM_WIDTH + (hd + 1) * M_DIM)
            mvt_out[0, 0, hd * CELL_V_ROWS:hd * CELL_V_ROWS + M_DIM, :] = zt[rows].astype(BF16)
            mvt_out[0, 0, hd * CELL_V_ROWS + M_DIM:(hd + 1) * CELL_V_ROWS, :] = (
                unit_rows.astype(F32).astype(BF16))
        mot_out[0, 0] = zt[2 * MLSTM_WIDTH:3 * MLSTM_WIDTH]
        gt_out[0, 0] = gates.T[GATE_I_LANE:GATE_I_LANE + 2 * M_HEADS, :]
    else:
        zm = _dot_nt(u, wqvot_ref[...])
        mq_out[...] = zm[:, 0:MLSTM_WIDTH].astype(BF16)
        mv_out[...] = zm[:, MLSTM_WIDTH:2 * MLSTM_WIDTH].astype(BF16)
        mo_out[...] = zm[:, 2 * MLSTM_WIDTH:3 * MLSTM_WIDTH]
        gate_out[...] = gates


def _proj_call(h, w, rope_tab, tm, cell_feature_major):
    t = h.shape[0]
    n_tab = rope_tab.shape[1] // tm
    n_seq = t // tm // n_tab
    row = lambda i: (i, 0)
    tab = lambda i: (0, i % n_tab)
    tok = lambda width, dtype: (jax.ShapeDtypeStruct((t, width), dtype),
                                pl.BlockSpec((tm, width), row))
    slab = lambda rows, dtype: (jax.ShapeDtypeStruct((n_seq, n_tab, rows, tm), dtype),
                                pl.BlockSpec((1, 1, rows, tm), lambda i: (i // n_tab, i % n_tab, 0, 0)))
    outs = [tok(MLA_PAD_WIDTH, BF16), tok(MLA_PAD_WIDTH, BF16), slab(MLA_HEADS * V_ROWS, BF16),
            tok(KV_LORA, F32), tok(ROPE_DIM, F32),
            (jax.ShapeDtypeStruct((n_seq, ROPE_DIM, n_tab * tm), F32),
             pl.BlockSpec((1, ROPE_DIM, tm), lambda i: (i // n_tab, 0, i % n_tab)))]
    if cell_feature_major:
        outs += [tok(MLSTM_WIDTH, BF16), slab(MLSTM_WIDTH, BF16), slab(M_HEADS * CELL_V_ROWS, BF16),
                 slab(MLSTM_WIDTH, F32), slab(2 * M_HEADS, F32)]
    else:
        outs += [tok(MLSTM_WIDTH, BF16)] * 3 + [tok(MLSTM_WIDTH, F32), tok(LANES, F32)]
    return pl.pallas_call(
        functools.partial(_proj_body, cell_feature_major),
        grid=(t // tm,),
        in_specs=[pl.BlockSpec((tm, D_MODEL), row), _resident((1, D_MODEL)),
                  _resident((Q_LORA + LANES, D_MODEL)), _resident((KV_LORA, D_MODEL)),
                  _resident((1, Q_LORA)), _resident((Q_LORA, MLA_PAD_WIDTH)),
                  _resident((1, KV_LORA)), _resident((KV_LORA, MLA_PAD_WIDTH)),
                  _resident((MLA_HEADS * V_ROWS, KV_LORA)),
                  _resident((1, LANES)),
                  pl.BlockSpec((ROPE_DIM, tm), tab),
                  _resident((MLSTM_WIDTH, D_MODEL)), _resident((3 * MLSTM_WIDTH, D_MODEL))],
        out_specs=[o[1] for o in outs],
        out_shape=[o[0] for o in outs],
        compiler_params=_params(1),
        name="proj",
    )(h, w["g_mix"], w["w_qs_t"], w["w_kv_t"], w["g_q"], w["w_uq"], w["g_kv"],
      w["w_uk"], w["w_uv_t"], w["gate_bias"], rope_tab, w["w_mk_t"], w["w_qvo_t"])


ATT_BLOCK = 512
ATT_HEADS = 8
ATT_LANES = ATT_HEADS * HEAD_PAD
ATT_V_ROWS = ATT_HEADS * V_ROWS
ATT_LOOKAHEAD = 3


def _flash_body(q_ref, k_ref, vt_ref, o_ref):
    qi = pl.program_id(2)
    blk = ATT_BLOCK
    head_lanes = [slice(hd * HEAD_PAD, (hd + 1) * HEAD_PAD) for hd in range(ATT_HEADS)]
    qs = [q_ref[0, :, lanes] for lanes in head_lanes]

    def step(j, carry, diagonal):
        start = pl.multiple_of(j * blk, blk)
        scores = [None] * ATT_HEADS
        if diagonal:
            key = lax.broadcasted_iota(jnp.int32, (blk, blk), 0)
            qry = lax.broadcasted_iota(jnp.int32, (blk, blk), 1)
            visible = key <= qry
        out = []
        for hd in range(ATT_HEADS):
            m, acc = carry[hd]
            for nxt in range(hd if hd else 0, min(hd + ATT_LOOKAHEAD, ATT_HEADS - 1) + 1):
                if scores[nxt] is None:
                    scores[nxt] = _dot_nt(k_ref[0, pl.ds(start, blk), head_lanes[nxt]], qs[nxt])
            s = jnp.where(visible, scores[hd], -jnp.inf) if diagonal else scores[hd]
            m_new = jnp.maximum(m, jnp.max(s, axis=0, keepdims=True))
            p = jnp.exp2(s - m_new).astype(BF16)
            vt = vt_ref[0, j, hd * V_ROWS:(hd + 1) * V_ROWS, :]
            acc = jnp.exp2(m - m_new) * acc + _dot(vt, p)
            out.append((m_new, acc))
        return tuple(out)

    init = tuple((jnp.full((1, blk), -jnp.inf, F32), jnp.zeros((V_ROWS, blk), F32))
                 for _ in range(ATT_HEADS))
    carry = lax.fori_loop(0, qi, lambda j, c: step(j, c, False), init)
    final = step(qi, carry, True)
    for pair in range(ATT_HEADS // 2):
        o_t = jnp.concatenate([final[hd][1][:V_DIM] / final[hd][1][V_DIM:V_DIM + 1]
                               for hd in (2 * pair, 2 * pair + 1)], axis=0)
        o_ref[0, :, pair * LANES:(pair + 1) * LANES] = o_t.T.astype(BF16)


def _flash_call(q, k, vt):
    b, s, _ = q.shape
    assert vt.shape == (b, s // ATT_BLOCK, MLA_HEADS * V_ROWS, ATT_BLOCK)
    qmap = lambda bi, gi, qi: (bi, qi, gi)
    return pl.pallas_call(
        _flash_body,
        grid=(b, MLA_HEADS // ATT_HEADS, s // ATT_BLOCK),
        in_specs=[pl.BlockSpec((1, ATT_BLOCK, ATT_LANES), qmap),
                  pl.BlockSpec((1, s, ATT_LANES), lambda bi, gi, qi: (bi, 0, gi),
                               pipeline_mode=pl.Buffered(1)),
                  pl.BlockSpec((1, s // ATT_BLOCK, ATT_V_ROWS, ATT_BLOCK),
                               lambda bi, gi, qi: (bi, 0, gi, 0), pipeline_mode=pl.Buffered(1))],
        out_specs=pl.BlockSpec((1, ATT_BLOCK, ATT_HEADS * V_DIM), qmap),
        out_shape=jax.ShapeDtypeStruct((b, s, MLA_WIDTH), BF16),
        compiler_params=_params(3),
        name="prompt_attn",
    )(q, k, vt)


def _mlstm_body(k_ref, qt_ref, vt_ref, ot_ref, gt_ref, gm_ref, hm_out, cxt_out, m_out, cxt_s, m_s):
    ci = pl.program_id(0)
    n_seq = k_ref.shape[0]
    L = CHUNK

    @pl.when(ci == 0)
    def _():
        cxt_s[...] = jnp.zeros_like(cxt_s)
        m_s[...] = jnp.zeros_like(m_s)

    s_idx = lax.broadcasted_iota(jnp.int32, (L, L), 0)
    t_idx = lax.broadcasted_iota(jnp.int32, (L, L), 1)
    causal = s_idx <= t_idx

    gate_rows = [gt_ref[sq, 0] for sq in range(n_seq)]
    prefix = [jnp.dot(rows, causal.astype(F32), precision=lax.Precision.HIGHEST,
                      preferred_element_type=F32) for rows in gate_rows]

    streams = []
    for sq in range(n_seq):
        for hd in range(M_HEADS):
            feat = slice(hd * M_DIM, (hd + 1) * M_DIM)
            idx = sq * M_HEADS + hd
            k = k_ref[sq, :, feat]
            qt = qt_ref[sq, 0, feat, :]
            cxt_prev = cxt_s[idx]
            streams.append(dict(
                sq=sq, hd=hd, feat=feat, idx=idx, k=k, cxt_prev=cxt_prev,
                vt=vt_ref[sq, 0, hd * CELL_V_ROWS:(hd + 1) * CELL_V_ROWS, :],
                m_prev=m_s[idx][0:1, :],
                qk=_dot(k, qt),
                qe=_dot(cxt_prev.astype(BF16), qt)))

    gates = []
    for sq in range(n_seq):
        rows = gate_rows[sq]
        b_rows = pltpu.roll(prefix[sq], M_HEADS, 0)
        head_row = lax.broadcasted_iota(jnp.int32, rows.shape, 0) < M_HEADS
        a_rows = jnp.where(head_row, rows - b_rows, 0.0)
        top = jnp.broadcast_to(jnp.max(a_rows, axis=1, keepdims=True), rows.shape)
        b_last = pltpu.roll(jnp.broadcast_to(jnp.sum(rows, axis=1, keepdims=True), rows.shape),
                            M_HEADS, 0)
        a_cols = jnp.concatenate([a_rows, jnp.zeros((L - SUBLANES, L), F32)], axis=0).T
        gates.append(dict(b_rows=b_rows, a_cols=a_cols, top=top, b_last=b_last,
                          src=jnp.exp(a_rows - top)))

    for st in streams:
        g, hd = gates[st["sq"]], st["hd"]
        weighted_vt = (st["vt"].astype(F32) * g["src"][hd:hd + 1, :]).astype(BF16)
        st["kv"] = _dot(weighted_vt, st["k"])

    for st in streams:
        g, hd = gates[st["sq"]], st["hd"]
        a_col = g["a_cols"][:, hd:hd + 1]
        run_max = jnp.max(jnp.where(causal, a_col, -jnp.inf), axis=0, keepdims=True)
        big_m = jnp.maximum(st["m_prev"], run_max)
        st["sqk"] = st["qk"] * jnp.where(causal, jnp.exp(a_col - big_m), 0.0)
        st["w_inter"] = jnp.exp(st["m_prev"] - big_m)
        st["floor"] = jnp.exp(-(g["b_rows"][hd:hd + 1, :] + big_m))
        st["top"] = g["top"][hd:hd + 1, :]
        st["b_last"] = g["b_last"][hd:hd + 1, :]

    for st in streams:
        ue = _dot(st["vt"], st["sqk"].astype(BF16))
        num = ue[:M_DIM] + st["qe"][:M_DIM] * st["w_inter"]
        den = ue[M_DIM:M_DIM + 1] + st["qe"][M_DIM:M_DIM + 1] * st["w_inter"]
        inv = 1.0 / jnp.maximum(jnp.abs(den), st["floor"])
        t = jax.nn.sigmoid(ot_ref[st["sq"], 0, st["feat"], :]) * num
        ms = jnp.sum(t * t, axis=0, keepdims=True) * (1.0 / M_DIM)
        scale = inv * lax.rsqrt(ms * inv * inv + EPS)
        hm_out[st["sq"], :, st["feat"]] = (t * scale * gm_ref[st["feat"], :]).T.astype(BF16)
        m_last = jnp.maximum(st["m_prev"], st["top"])
        keep = jnp.exp(st["m_prev"] - m_last)
        gain = jnp.exp(st["top"] - m_last)
        cxt_s[st["idx"]] = keep * st["cxt_prev"] + gain * st["kv"]
        m_s[st["idx"]] = jnp.broadcast_to(st["b_last"] + m_last, (SUBLANES, LANES))

    @pl.when(ci == pl.num_programs(0) - 1)
    def _():
        cxt_out[...] = cxt_s[...]
        m_out[...] = m_s[...]


def _mlstm_call(mk, mqt, mvt, mot, gt, gm_lanes):
    b, s, _ = mk.shape
    tile = mqt.shape[-1]
    per_tile = tile // CHUNK
    n_streams = b * M_HEADS
    slab = lambda rows: pl.BlockSpec((b, 1, rows, CHUNK),
                                     lambda ci: (0, ci // per_tile, 0, ci % per_tile))
    tok = pl.BlockSpec((b, CHUNK, MLSTM_WIDTH), lambda ci: (0, ci, 0))
    whole3 = lambda ci: (0, 0, 0)
    return pl.pallas_call(
        _mlstm_body,
        grid=(s // CHUNK,),
        in_specs=[tok, slab(MLSTM_WIDTH), slab(M_HEADS * CELL_V_ROWS), slab(MLSTM_WIDTH),
                  slab(2 * M_HEADS), _resident((MLSTM_WIDTH, LANES))],
        out_specs=[tok,
                   pl.BlockSpec((n_streams, CELL_V_ROWS, M_DIM), whole3),
                   pl.BlockSpec((n_streams, SUBLANES, LANES), whole3)],
        out_shape=[jax.ShapeDtypeStruct((b, s, MLSTM_WIDTH), BF16),
                   jax.ShapeDtypeStruct((n_streams, CELL_V_ROWS, M_DIM), F32),
                   jax.ShapeDtypeStruct((n_streams, SUBLANES, LANES), F32)],
        scratch_shapes=[pltpu.VMEM((n_streams, CELL_V_ROWS, M_DIM), F32),
                        pltpu.VMEM((n_streams, SUBLANES, LANES), F32)],
        compiler_params=_params(1),
        name="prompt_mlstm",
    )(mk, mqt, mvt, mot, gt, gm_lanes)


def _merge_body(h_ref, a_ref, hm_ref, p_ref, ga_ref, woa_ref, wom_ref, gff_ref, wg_ref, wu_ref,
                wd_ref, gple_ref, wpg_ref, wpp_ref, gfin_ref, y_ref):
    a = a_ref[...].astype(F32)
    an = _rms(a, ga_ref[...]).astype(BF16)
    h = h_ref[...] + _dot(an, woa_ref[...]) + _dot(hm_ref[...], wom_ref[...])
    h = h + _swiglu_half(_rms(h, gff_ref[...]).astype(BF16), wg_ref, wu_ref, wd_ref)
    gate = jax.nn.sigmoid(_dot(_rms(h, gple_ref[...]).astype(BF16), wpg_ref[...]))
    h = h + gate * _dot(p_ref[...].astype(BF16), wpp_ref[...])
    y_ref[...] = _rms(h, gfin_ref[...])


def _merge_call(h, a, hm, p, w, tm):
    t = h.shape[0]
    row = lambda i: (i, 0)
    return pl.pallas_call(
        _merge_body,
        grid=(t // tm,),
        in_specs=[pl.BlockSpec((tm, D_MODEL), row), pl.BlockSpec((tm, MLA_WIDTH), row),
                  pl.BlockSpec((tm, MLSTM_WIDTH), row), pl.BlockSpec((tm, PLE_DIM), row),
                  _resident((1, MLA_WIDTH)), _resident((MLA_WIDTH, D_MODEL)),
                  _resident((MLSTM_WIDTH, D_MODEL)), _resident((1, D_MODEL)),
                  _resident((D_MODEL, D_FF)), _resident((D_MODEL, D_FF)), _resident((D_FF, D_MODEL)),
                  _resident((1, D_MODEL)), _resident((D_MODEL, D_MODEL)),
                  _resident((PLE_DIM, D_MODEL)), _resident((1, D_MODEL))],
        out_specs=pl.BlockSpec((tm, D_MODEL), row),
        out_shape=jax.ShapeDtypeStruct((t, D_MODEL), F32),
        compiler_params=_params(1),
        name="merge",
    )(h, a, hm, p, w["g_attn"], w["w_out_a"], w["w_out_m"], w["g_ff2"], w["w_ff2_gate"],
      w["w_ff2_up"], w["w_ff2_down"], w["g_ple"], w["w_ple_gate"], w["w_ple_proj"], w["g_final"])


N_SLOTS = 2
SAMPLE_CHUNKS = 4
SAMPLE_CHUNK_PAGES = N_PAGES // SAMPLE_CHUNKS
SAMPLE_CHUNK_KEYS = SAMPLE_CHUNK_PAGES * PAGE_SIZE


def _sample_attn_body(pt_ref, q_ref, ckv_ref, kr_ref, cache_c, cache_r, wabs_ref, wuv_ref,
                      o_ref, cbuf, rbuf, sem_c, sem_r):
    b = pl.program_id(0)
    nb = pl.num_programs(0)
    slot = b % N_SLOTS

    def copies(bi, sl):
        out = []
        for j in range(N_PAGES):
            page = pt_ref[bi, j]
            out.append(pltpu.make_async_copy(cache_c.at[page], cbuf.at[sl, j], sem_c.at[sl]))
            out.append(pltpu.make_async_copy(
                cache_r.at[page], rbuf.at[sl, :, pl.ds(j * PAGE_SIZE, PAGE_SIZE)], sem_r.at[sl]))
        return out

    @pl.when(b == 0)
    def _():
        for bi in range(min(N_SLOTS, DEC_BATCH)):
            for cp in copies(bi, bi):
                cp.start()

    q_row = q_ref[0]
    sub = lax.broadcasted_iota(jnp.int32, (MLA_HEADS, MLA_PAD_WIDTH), 0)
    lane = lax.broadcasted_iota(jnp.int32, (MLA_HEADS, MLA_PAD_WIDTH), 1)
    own = (lane // HEAD_PAD) == sub
    q_bd = jnp.where(own, jnp.broadcast_to(q_row.astype(F32), own.shape), 0.0).astype(BF16)
    q_ext = _dot(q_bd, wabs_ref[...])
    q_abs = q_ext[:, :KV_LORA].astype(BF16)
    q_rope = q_ext[:, KV_LORA:KV_LORA + ROPE_DIM].astype(BF16)

    c_new = ckv_ref[0].astype(BF16).astype(F32)
    r_new = kr_ref[0].astype(BF16).astype(F32)
    s_new = (jnp.sum(q_abs.astype(F32) * c_new, axis=1, keepdims=True)
             + jnp.sum(q_rope.astype(F32) * r_new, axis=1, keepdims=True))

    for cp in copies(b, slot):
        cp.wait()

    def chunk_keys(i):
        pages = slice(i * SAMPLE_CHUNK_PAGES, (i + 1) * SAMPLE_CHUNK_PAGES)
        keys = slice(i * SAMPLE_CHUNK_KEYS, (i + 1) * SAMPLE_CHUNK_KEYS)
        kc = cbuf[slot, pages].reshape(SAMPLE_CHUNK_KEYS, KV_LORA).astype(BF16)
        return kc, _dot_nt(q_abs, kc) + _dot(q_rope, rbuf[slot, :, keys].astype(BF16))

    m = s_new
    l = jnp.ones_like(s_new)
    acc = jnp.broadcast_to(c_new, (MLA_HEADS, KV_LORA))
    kc, s = chunk_keys(0)
    for i in range(SAMPLE_CHUNKS):
        nxt = chunk_keys(i + 1) if i + 1 < SAMPLE_CHUNKS else None
        m_new = jnp.maximum(m, jnp.max(s, axis=1, keepdims=True))
        alpha = jnp.exp2(m - m_new)
        p = jnp.exp2(s - m_new)
        l = alpha * l + jnp.sum(p, axis=1, keepdims=True)
        acc = alpha * acc + _dot(p.astype(BF16), kc)
        m = m_new
        if nxt is not None:
            kc, s = nxt
    o_lat = (acc / l).astype(BF16)
    res = _dot(o_lat, wuv_ref[...])
    own_v = (lax.broadcasted_iota(jnp.int32, res.shape, 1) // V_DIM
             == lax.broadcasted_iota(jnp.int32, res.shape, 0))
    o_ref[0] = jnp.sum(jnp.where(own_v, res, 0.0), axis=0, keepdims=True).astype(BF16)

    @pl.when(b + N_SLOTS < nb)
    def _():
        for cp in copies(b + N_SLOTS, slot):
            cp.start()


def _sample_attn_call(page_table, q, ckv, kr, cache_c, cache_r, w_abs, w_uv):
    nb = q.shape[0]
    tok = lambda bi, pt: (bi, 0, 0)
    whole = lambda shape: pl.BlockSpec(shape, lambda bi, pt: (0,) * len(shape),
                                       pipeline_mode=pl.Buffered(1))
    grid_spec = pltpu.PrefetchScalarGridSpec(
        num_scalar_prefetch=1,
        grid=(nb,),
        in_specs=[pl.BlockSpec((1, 1, MLA_PAD_WIDTH), tok), pl.BlockSpec((1, 1, KV_LORA), tok),
                  pl.BlockSpec((1, 1, ROPE_DIM), tok),
                  pl.BlockSpec(memory_space=pl.ANY), pl.BlockSpec(memory_space=pl.ANY),
                  whole((MLA_PAD_WIDTH, KV_LORA + LANES)), whole((KV_LORA, MLA_WIDTH))],
        out_specs=pl.BlockSpec((1, 1, MLA_WIDTH), tok),
        scratch_shapes=[pltpu.VMEM((N_SLOTS, N_PAGES, PAGE_SIZE, KV_LORA), F32),
                        pltpu.VMEM((N_SLOTS, ROPE_DIM, PAST_LEN), F32),
                        pltpu.SemaphoreType.DMA((N_SLOTS,)), pltpu.SemaphoreType.DMA((N_SLOTS,))])
    return pl.pallas_call(
        _sample_attn_body,
        grid_spec=grid_spec,
        out_shape=jax.ShapeDtypeStruct((nb, 1, MLA_WIDTH), BF16),
        compiler_params=_params(1),
        name="sample_attn",
    )(page_table, q, ckv, kr, cache_c, cache_r, w_abs, w_uv)


SAMPLE_ROWS = 32


def _sample_mlstm_body(q_ref, k_ref, v_ref, o_ref, g_ref, gm_ref, c_ref, n_ref, m_ref,
                       hm_out, c_out, n_out, m_out):
    hd = pl.program_id(1)
    g = g_ref[...]
    lane = lax.broadcasted_iota(jnp.int32, g.shape, 1)
    ig = jnp.sum(jnp.where(lane == GATE_I_LANE + hd, g, 0.0), axis=1, keepdims=True)
    lf = jnp.sum(jnp.where(lane == GATE_F_LANE + hd, g, 0.0), axis=1, keepdims=True)
    m_all = m_ref[...]
    head_lane = lax.broadcasted_iota(jnp.int32, m_all.shape, 1)
    m0 = jnp.sum(jnp.where(head_lane == hd, m_all, 0.0), axis=1, keepdims=True)
    q = q_ref[...].astype(F32)
    k = k_ref[...].astype(F32)
    v = v_ref[...].astype(F32)
    n0 = n_ref[...]
    m_new = jnp.maximum(lf + m0, ig)
    keep = jnp.exp(lf + m0 - m_new)
    w_src = jnp.exp(ig - m_new)
    sqk = jnp.sum(q * k, axis=1, keepdims=True) * w_src
    wk = w_src * k
    qc = jnp.zeros_like(v)
    for d in range(M_DIM):
        c_d = c_ref[:, 0, d, :]
        qc = qc + q[:, d:d + 1] * c_d
        c_out[:, 0, d, :] = keep * c_d + wk[:, d:d + 1] * v
    num = sqk * v + qc * keep
    den = sqk + jnp.sum(q * n0, axis=1, keepdims=True) * keep
    den = jnp.maximum(jnp.abs(den), jnp.exp(-m_new))
    hh = jax.nn.sigmoid(o_ref[...]) * (num / den)
    hm_out[...] = _rms(hh, gm_ref[...]).astype(BF16)
    n_out[...] = keep * n0 + wk

    @pl.when(hd == 0)
    def _():
        m_out[...] = jnp.broadcast_to(m_new, m_all.shape)

    @pl.when(hd > 0)
    def _():
        m_out[...] = jnp.where(head_lane == hd, m_new, m_out[...])


def _sample_mlstm_call(mq, mk, mv, mo, gates, gm, c0, n0, m0):
    nb = mq.shape[0]
    rows = SAMPLE_ROWS
    tok = lambda bi, hi: (bi, hi)
    return pl.pallas_call(
        _sample_mlstm_body,
        grid=(nb // rows, M_HEADS),
        in_specs=[pl.BlockSpec((rows, M_DIM), tok)] * 4
                 + [pl.BlockSpec((rows, LANES), lambda bi, hi: (bi, 0)),
                    pl.BlockSpec((1, M_DIM), lambda bi, hi: (0, hi)),
                    pl.BlockSpec((rows, 1, M_DIM, M_DIM), lambda bi, hi: (bi, hi, 0, 0)),
                    pl.BlockSpec((rows, M_DIM), tok),
                    pl.BlockSpec((rows, M_HEADS), lambda bi, hi: (bi, 0))],
        out_specs=[pl.BlockSpec((rows, M_DIM), tok),
                   pl.BlockSpec((rows, 1, M_DIM, M_DIM), lambda bi, hi: (bi, hi, 0, 0)),
                   pl.BlockSpec((rows, M_DIM), tok),
                   pl.BlockSpec((rows, M_HEADS), lambda bi, hi: (bi, 0))],
        out_shape=[jax.ShapeDtypeStruct((nb, MLSTM_WIDTH), BF16),
                   jax.ShapeDtypeStruct((nb, M_HEADS, M_DIM, M_DIM), F32),
                   jax.ShapeDtypeStruct((nb, MLSTM_WIDTH), F32),
                   jax.ShapeDtypeStruct((nb, M_HEADS), F32)],
        compiler_params=_params(2),
        name="sample_mlstm",
    )(mq, mk, mv, mo, gates, gm, c0, n0, m0)


def _pad_heads(w, head_dim):
    rows = w.shape[0]
    w = w.reshape(rows, MLA_HEADS, head_dim)
    w = jnp.pad(w, ((0, 0), (0, 0), (0, HEAD_PAD - head_dim)))
    return w.reshape(rows, MLA_PAD_WIDTH)


def _rope_tables(pos):
    inv = ROPE_THETA ** (-jnp.arange(ROPE_HALF, dtype=F32) / ROPE_HALF)
    ang = inv[:, None] * pos.astype(F32)[None, :]
    return jnp.concatenate([jnp.cos(ang), jnp.sin(ang)], axis=0)


def _prep_weights(g_ff1, w_ff1_gate, w_ff1_up, w_ff1_down, g_mix, w_in, g_q, w_uq, g_kv, w_uk,
                  w_uv, b_gate_i, b_gate_f, g_attn_out, g_mlstm_out, w_out, g_ff2, w_ff2_gate,
                  w_ff2_up, w_ff2_down, g_ple, w_ple_gate, w_ple_proj, g_final):
    bf = lambda a: a.astype(BF16)
    row = lambda a: a.reshape(1, -1).astype(F32)
    off_kv, off_kr = Q_LORA, Q_LORA + KV_LORA
    off_m = off_kr + ROPE_DIM
    off_i = off_m + 4 * MLSTM_WIDTH
    small_pad = LANES - ROPE_DIM - 2 * M_HEADS
    w_t = bf(w_in.T)
    w_qs_t = jnp.concatenate([w_t[:off_kv], w_t[off_kr:off_m], w_t[off_i:off_i + 2 * M_HEADS],
                              jnp.zeros((small_pad, D_MODEL), BF16)], axis=0)
    w_qvo_t = jnp.concatenate([w_t[off_m:off_m + MLSTM_WIDTH], w_t[off_m + 2 * MLSTM_WIDTH:off_i]],
                              axis=0)
    gate_bias = jnp.concatenate([jnp.zeros((ROPE_DIM,), F32), b_gate_i, b_gate_f,
                                 jnp.zeros((small_pad,), F32)])
    src = jnp.arange(LANES)[:, None]
    dst = jnp.arange(MLA_PAD_WIDTH)[None, :]
    w_abs = jnp.pad(w_uk.reshape(KV_LORA, MLA_HEADS, NOPE_DIM).transpose(1, 2, 0),
                    ((0, 0), (0, HEAD_PAD - NOPE_DIM), (0, 0))).reshape(MLA_PAD_WIDTH, KV_LORA)
    rope_sel = (dst.T % HEAD_PAD) == (src.T + NOPE_DIM)
    rope_sel = rope_sel & (src.T < ROPE_DIM)
    w_abs = jnp.concatenate([w_abs, rope_sel.astype(F32)], axis=1)
    return dict(
        g_ff1=row(g_ff1), w_ff1_gate=bf(w_ff1_gate), w_ff1_up=bf(w_ff1_up), w_ff1_down=bf(w_ff1_down),
        g_mix=row(g_mix), w_qs_t=w_qs_t, w_kv_t=w_t[off_kv:off_kr],
        w_mk_t=w_t[off_m + MLSTM_WIDTH:off_m + 2 * MLSTM_WIDTH], w_qvo_t=w_qvo_t,
        g_q=row(g_q), w_uq=bf(_pad_heads(w_uq, NOPE_DIM + ROPE_DIM)),
        g_kv=row(g_kv), w_uk=bf(_pad_heads(w_uk, NOPE_DIM)), w_uv=bf(w_uv),
        w_uv_t=bf(jnp.pad(w_uv.T.reshape(MLA_HEADS, V_DIM, KV_LORA),
                          ((0, 0), (0, V_ROWS - V_DIM), (0, 0))).reshape(MLA_HEADS * V_ROWS, KV_LORA)),
        gate_bias=row(gate_bias), w_abs=bf(w_abs),
        g_attn=row(g_attn_out), w_out_a=bf(w_out[:MLA_WIDTH]),
        w_out_m=bf(w_out[MLA_WIDTH:]), g_mlstm=row(g_mlstm_out),
        g_mlstm_lanes=jnp.broadcast_to(g_mlstm_out.reshape(MLSTM_WIDTH, 1).astype(F32),
                                       (MLSTM_WIDTH, LANES)),
        g_ff2=row(g_ff2), w_ff2_gate=bf(w_ff2_gate), w_ff2_up=bf(w_ff2_up), w_ff2_down=bf(w_ff2_down),
        g_ple=row(g_ple), w_ple_gate=bf(w_ple_gate), w_ple_proj=bf(w_ple_proj), g_final=row(g_final))


PROMPT_TILE = 512


def kernel(x_prompt, x_sample, p_prompt, p_sample, cache_ckv, cache_krope, state_C, state_n, state_m, page_table, g_ff1, w_ff1_gate, w_ff1_up, w_ff1_down, g_mix, w_in, g_q, w_uq, g_kv, w_uk, w_uv, b_gate_i, b_gate_f, g_attn_out, g_mlstm_out, w_out, g_ff2, w_ff2_gate, w_ff2_up, w_ff2_down, g_ple, w_ple_gate, w_ple_proj, g_final):
    assert w_in.shape[0] == 1, "single-layer trunk"
    w = _prep_weights(g_ff1[0], w_ff1_gate[0], w_ff1_up[0], w_ff1_down[0], g_mix[0], w_in[0],
                      g_q[0], w_uq[0], g_kv[0], w_uk[0], w_uv[0], b_gate_i[0], b_gate_f[0],
                      g_attn_out[0], g_mlstm_out[0], w_out[0], g_ff2[0], w_ff2_gate[0],
                      w_ff2_up[0], w_ff2_down[0], g_ple[0], w_ple_gate[0], w_ple_proj[0], g_final)
    nb_p, seq, _ = x_prompt.shape
    nb_s = x_sample.shape[0]
    t_p = nb_p * seq

    tab_p = _rope_tables(jnp.arange(seq))
    h_p = _ffn_call(x_prompt.reshape(t_p, D_MODEL), w["g_ff1"], w["w_ff1_gate"], w["w_ff1_up"],
                    w["w_ff1_down"], PROMPT_TILE)
    q_p, k_p, vt_p, ckv_p, _, krt_p, mk_p, mqt_p, mvt_p, mot_p, gt_p = _proj_call(
        h_p, w, tab_p, PROMPT_TILE, cell_feature_major=True)
    seq3 = lambda a: a.reshape(nb_p, seq, a.shape[-1])
    a_p = _flash_call(seq3(q_p), seq3(k_p), vt_p)
    hm_p, cxt_p, m_p = _mlstm_call(seq3(mk_p), mqt_p, mvt_p, mot_p, gt_p, w["g_mlstm_lanes"])
    cxt_p = cxt_p.reshape(nb_p, M_HEADS, CELL_V_ROWS, M_DIM)
    y_p = _merge_call(h_p, a_p.reshape(t_p, MLA_WIDTH), hm_p.reshape(t_p, MLSTM_WIDTH),
                      p_prompt.reshape(t_p, PLE_DIM), w, PROMPT_TILE)

    tab_s = _rope_tables(jnp.full((nb_s,), PAST_LEN, jnp.int32))
    h_s = _ffn_call(x_sample.reshape(nb_s, D_MODEL), w["g_ff1"], w["w_ff1_gate"], w["w_ff1_up"],
                    w["w_ff1_down"], nb_s)
    q_s, _, _, ckv_s, kr_s, krt_s, mq_s, mk_s, mv_s, mo_s, gates_s = _proj_call(
        h_s, w, tab_s, nb_s, cell_feature_major=False)
    n_phys = cache_ckv.shape[1]
    a_s = _sample_attn_call(
        page_table, q_s.reshape(nb_s, 1, MLA_PAD_WIDTH), ckv_s.reshape(nb_s, 1, KV_LORA),
        kr_s.reshape(nb_s, 1, ROPE_DIM), cache_ckv.reshape(n_phys, PAGE_SIZE, KV_LORA),
        jnp.swapaxes(cache_krope.reshape(n_phys, PAGE_SIZE, ROPE_DIM), 1, 2), w["w_abs"], w["w_uv"])
    hm_s, c_s, n_s, m_s = _sample_mlstm_call(
        mq_s, mk_s, mv_s, mo_s, gates_s, w["g_mlstm"], state_C[0].astype(F32),
        state_n[0].astype(F32).reshape(nb_s, MLSTM_WIDTH), state_m[0].astype(F32))
    y_s = _merge_call(h_s, a_s.reshape(nb_s, MLA_WIDTH), hm_s, p_sample.reshape(nb_s, PLE_DIM),
                      w, nb_s)

    return (y_p.reshape(nb_p, seq, D_MODEL), y_s.reshape(nb_s, 1, D_MODEL),
            ckv_p.reshape(1, nb_p, seq, KV_LORA), jnp.swapaxes(krt_p, 1, 2)[None],
            jnp.swapaxes(cxt_p[:, :, :M_DIM, :], 2, 3)[None], cxt_p[None, :, :, M_DIM, :],
            m_p[:, 0, 0].reshape(1, nb_p, M_HEADS),
            ckv_s.reshape(1, nb_s, 1, KV_LORA), jnp.swapaxes(krt_s, 1, 2).reshape(1, nb_s, 1, ROPE_DIM),
            c_s[None], n_s.reshape(1, nb_s, M_HEADS, M_DIM), m_s[None])
```

```python
import functools
import math

import jax
import jax.numpy as jnp
from jax import lax
from jax.experimental import pallas as pl
from jax.experimental.pallas import tpu as pltpu

F32 = jnp.float32
BF16 = jnp.bfloat16

D_MODEL = 1024
SEQ = 8192
DEC_BATCH = 128
PAST_LEN = 8192
PAGE_SIZE = 128
N_PAGES = PAST_LEN // PAGE_SIZE
MLA_HEADS = 8
Q_LORA = 384
KV_LORA = 256
NOPE_DIM = 64
ROPE_DIM = 32
ROPE_HALF = ROPE_DIM // 2
V_DIM = 64
ROPE_THETA = 10000.0
M_HEADS = 4
M_DIM = 128
CHUNK = 128
MLSTM_WIDTH = M_HEADS * M_DIM
D_FF = 2816
PLE_DIM = 256
EPS = 1e-6

LANES = 128
SUBLANES = 8

HEAD_PAD = LANES
MLA_PAD_WIDTH = MLA_HEADS * HEAD_PAD
MLA_WIDTH = MLA_HEADS * V_DIM
V_ROWS = 80
GATE_I_LANE = ROPE_DIM
GATE_F_LANE = ROPE_DIM + M_HEADS
QK_SCALE = (NOPE_DIM + ROPE_DIM) ** -0.5 * math.log2(math.e)

VMEM_LIMIT = 56 * 1024 * 1024


def _dot(a, b):
    return jnp.dot(a, b, preferred_element_type=F32)


def _dot_nt(a, b):
    return lax.dot_general(a, b, (((1,), (1,)), ((), ())), preferred_element_type=F32)


def _rms(x, g):
    ms = jnp.sum(x * x, axis=-1, keepdims=True) * (1.0 / x.shape[-1])
    return x * lax.rsqrt(ms + EPS) * g


def _resident(shape):
    return pl.BlockSpec(shape, lambda *_: (0,) * len(shape), pipeline_mode=pl.Buffered(1))


def _params(n_axes):
    return pltpu.CompilerParams(dimension_semantics=("arbitrary",) * n_axes,
                                vmem_limit_bytes=VMEM_LIMIT)


MXU_TILE = 256
FF_CHUNK_BOUNDS = (0, 6 * MXU_TILE, D_FF)
assert D_FF % MXU_TILE == 0


def _swiglu_half(xn, wg_ref, wu_ref, wd_ref):
    out = None
    for lo, hi in zip(FF_CHUNK_BOUNDS[:-1], FF_CHUNK_BOUNDS[1:]):
        cols = slice(lo, hi)
        gate = _dot(xn, wg_ref[:, cols])
        up = _dot(xn, wu_ref[:, cols])
        act = (jax.nn.silu(gate) * up).astype(BF16)
        part = _dot(act, wd_ref[cols, :])
        out = part if out is None else out + part
    return 0.5 * out


def _ffn_body(x_ref, g_ref, wg_ref, wu_ref, wd_ref, o_ref):
    x = x_ref[...]
    xn = _rms(x, g_ref[...]).astype(BF16)
    o_ref[...] = x + _swiglu_half(xn, wg_ref, wu_ref, wd_ref)


def _ffn_call(x, g, wg, wu, wd, tm):
    t = x.shape[0]
    row = lambda i: (i, 0)
    return pl.pallas_call(
        _ffn_body,
        grid=(t // tm,),
        in_specs=[pl.BlockSpec((tm, D_MODEL), row), _resident((1, D_MODEL)),
                  _resident((D_MODEL, D_FF)), _resident((D_MODEL, D_FF)), _resident((D_FF, D_MODEL))],
        out_specs=pl.BlockSpec((tm, D_MODEL), row),
        out_shape=jax.ShapeDtypeStruct((t, D_MODEL), F32),
        compiler_params=_params(1),
        name="ffn1",
    )(x, g, wg, wu, wd)


def _rope128(x, cos_tab, sin_tab, x2_start):
    lane = lax.broadcasted_iota(jnp.int32, x.shape, 1)
    partner = jnp.where(lane < x2_start, pltpu.roll(x, LANES - ROPE_HALF, 1),
                        pltpu.roll(x, ROPE_HALF, 1))
    return x * cos_tab + partner * sin_tab


CELL_V_ROWS = M_DIM + 16


def _proj_body(cell_feature_major, h_ref, gmix_ref, wqs_ref, wkv_ref, gq_ref, wuq_ref,
               gkv_ref, wuk_ref, wuvt_ref, bias_ref, tab_ref, *refs):
    wmk_ref, wqvot_ref = refs[:2]
    if cell_feature_major:
        q_out, k_out, vt_out, ckv_out, kr_out, krt_out, mk_out, mqt_out, mvt_out, mot_out, gt_out = refs[2:]
    else:
        q_out, k_out, vt_out, ckv_out, kr_out, krt_out, mq_out, mk_out, mv_out, mo_out, gate_out = refs[2:]
    u = _rms(h_ref[...], gmix_ref[...]).astype(BF16)
    ckv = _rms(_dot_nt(u, wkv_ref[...]), gkv_ref[...])
    ckv_out[...] = ckv
    ckv_b = ckv.astype(BF16)
    zqs = _dot_nt(u, wqs_ref[...])
    zs = zqs[:, Q_LORA:Q_LORA + LANES]
    n_tok = zs.shape[0]
    cs = jnp.concatenate([tab_ref[...], jnp.zeros((LANES - ROPE_DIM, n_tok), F32)], axis=0).T
    lane = lax.broadcasted_iota(jnp.int32, zs.shape, 1)
    shifted = lambda by: pltpu.roll(cs, by, 1)
    cos_k = jnp.where(lane < ROPE_HALF, cs, jnp.where(lane < ROPE_DIM, shifted(ROPE_HALF), 0.0))
    sin_k = jnp.where(lane < ROPE_HALF, -shifted(LANES - ROPE_HALF), jnp.where(lane < ROPE_DIM, cs, 0.0))
    kr = _rope128(zs, cos_k, sin_k, ROPE_HALF)
    kr_out[...] = kr[:, :ROPE_DIM]
    krt_out[0] = kr.T[:ROPE_DIM, :]
    kn = _dot(ckv_b, wuk_ref[...])
    kr_head = pltpu.roll(kr, NOPE_DIM, 1)
    for hd in range(MLA_HEADS):
        lanes = slice(hd * HEAD_PAD, (hd + 1) * HEAD_PAD)
        k_out[:, lanes] = (kn[:, lanes] + kr_head).astype(BF16)
    x1 = (lane >= NOPE_DIM) & (lane < NOPE_DIM + ROPE_HALF)
    x2 = (lane >= NOPE_DIM + ROPE_HALF) & (lane < NOPE_DIM + ROPE_DIM)
    cos_q = jnp.where(lane < NOPE_DIM, 1.0, jnp.where(x1, shifted(NOPE_DIM), jnp.where(
        x2, shifted(NOPE_DIM + ROPE_HALF), 0.0)))
    sin_q = jnp.where(x1, -shifted(NOPE_DIM - ROPE_HALF), jnp.where(x2, shifted(NOPE_DIM), 0.0))
    vt = _dot_nt(wuvt_ref[...], ckv_b)
    vrow = lax.broadcasted_iota(jnp.int32, vt.shape, 0)
    vt_out[0, 0] = jnp.where(vrow % V_ROWS == V_DIM, 1.0, vt).astype(BF16)
    qn = _rms(zqs[:, :Q_LORA], gq_ref[...]).astype(BF16)
    q = _dot(qn, wuq_ref[...])
    cos_qs = cos_q * QK_SCALE
    sin_qs = sin_q * QK_SCALE
    for hd in range(MLA_HEADS):
        lanes = slice(hd * HEAD_PAD, (hd + 1) * HEAD_PAD)
        q_out[:, lanes] = _rope128(q[:, lanes], cos_qs, sin_qs, NOPE_DIM + ROPE_HALF).astype(BF16)
    zb = zs + bias_ref[...]
    log_sig = jnp.minimum(zb, 0.0) - jnp.log1p(jnp.exp(-jnp.abs(zb)))
    is_f = (lane >= GATE_F_LANE) & (lane < GATE_F_LANE + M_HEADS)
    gates = jnp.where(is_f, log_sig, zb)
    mk_out[...] = (_dot_nt(u, wmk_ref[...]) * (M_DIM ** -0.5)).astype(BF16)
    if cell_feature_major:
        zt = _dot_nt(wqvot_ref[...], u)
        mqt_out[0, 0] = zt[0:MLSTM_WIDTH].astype(BF16)
        unit_rows = (lax.broadcasted_iota(jnp.int32, (CELL_V_ROWS - M_DIM, n_tok), 0) == 0)
        for hd in range(M_HEADS):
            rows = slice(MLSTM_WIDTH + hd * M_DIM, MLSTM_WIDTH + (hd + 1) * M_DIM)
            mvt_out[0, 0, hd * CELL_V_ROWS:hd * CELL_V_ROWS + M_DIM, :] = zt[rows].astype(BF16)
            mvt_out[0, 0, hd * CELL_V_ROWS + M_DIM:(hd + 1) * CELL_V_ROWS, :] = (
                unit_rows.astype(F32).astype(BF16))
        mot_out[0, 0] = zt[2 * MLSTM_WIDTH:3 * MLSTM_WIDTH]
        gt_out[0, 0] = gates.T[GATE_I_LANE:GATE_I_LANE + 2 * M_HEADS, :]
    else:
        zm = _dot_nt(u, wqvot_ref[...])
        mq_out[...] = zm[:, 0:MLSTM_WIDTH].astype(BF16)
        mv_out[...] = zm[:, MLSTM_WIDTH:2 * MLSTM_WIDTH].astype(BF16)
        mo_out[...] = zm[:, 2 * MLSTM_WIDTH:3 * MLSTM_WIDTH]
        gate_out[...] = gates


def _proj_call(h, w, rope_tab, tm, cell_feature_major):
    t = h.shape[0]
    n_tab = rope_tab.shape[1] // tm
    n_seq = t // tm // n_tab
    row = lambda i: (i, 0)
    tab = lambda i: (0, i % n_tab)
    tok = lambda width, dtype: (jax.ShapeDtypeStruct((t, width), dtype),
                                pl.BlockSpec((tm, width), row))
    slab = lambda rows, dtype: (jax.ShapeDtypeStruct((n_seq, n_tab, rows, tm), dtype),
                                pl.BlockSpec((1, 1, rows, tm), lambda i: (i // n_tab, i % n_tab, 0, 0)))
    outs = [tok(MLA_PAD_WIDTH, BF16), tok(MLA_PAD_WIDTH, BF16), slab(MLA_HEADS * V_ROWS, BF16),
            tok(KV_LORA, F32), tok(ROPE_DIM, F32),
            (jax.ShapeDtypeStruct((n_seq, ROPE_DIM, n_tab * tm), F32),
             pl.BlockSpec((1, ROPE_DIM, tm), lambda i: (i // n_tab, 0, i % n_tab)))]
    if cell_feature_major:
        outs += [tok(MLSTM_WIDTH, BF16), slab(MLSTM_WIDTH, BF16), slab(M_HEADS * CELL_V_ROWS, BF16),
                 slab(MLSTM_WIDTH, F32), slab(2 * M_HEADS, F32)]
    else:
        outs += [tok(MLSTM_WIDTH, BF16)] * 3 + [tok(MLSTM_WIDTH, F32), tok(LANES, F32)]
    return pl.pallas_call(
        functools.partial(_proj_body, cell_feature_major),
        grid=(t // tm,),
        in_specs=[pl.BlockSpec((tm, D_MODEL), row), _resident((1, D_MODEL)),
                  _resident((Q_LORA + LANES, D_MODEL)), _resident((KV_LORA, D_MODEL)),
                  _resident((1, Q_LORA)), _resident((Q_LORA, MLA_PAD_WIDTH)),
                  _resident((1, KV_LORA)), _resident((KV_LORA, MLA_PAD_WIDTH)),
                  _resident((MLA_HEADS * V_ROWS, KV_LORA)),
                  _resident((1, LANES)),
                  pl.BlockSpec((ROPE_DIM, tm), tab),
                  _resident((MLSTM_WIDTH, D_MODEL)), _resident((3 * MLSTM_WIDTH, D_MODEL))],
        out_specs=[o[1] for o in outs],
        out_shape=[o[0] for o in outs],
        compiler_params=_params(1),
        name="proj",
    )(h, w["g_mix"], w["w_qs_t"], w["w_kv_t"], w["g_q"], w["w_uq"], w["g_kv"],
      w["w_uk"], w["w_uv_t"], w["gate_bias"], rope_tab, w["w_mk_t"], w["w_qvo_t"])


ATT_BLOCK = 512
ATT_HEADS = 8
ATT_LANES = ATT_HEADS * HEAD_PAD
ATT_V_ROWS = ATT_HEADS * V_ROWS
ATT_LOOKAHEAD = 3


def _flash_body(q_ref, k_ref, vt_ref, o_ref):
    qi = pl.program_id(2)
    blk = ATT_BLOCK
    head_lanes = [slice(hd * HEAD_PAD, (hd + 1) * HEAD_PAD) for hd in range(ATT_HEADS)]
    qs = [q_ref[0, :, lanes] for lanes in head_lanes]

    def step(j, carry, diagonal):
        start = pl.multiple_of(j * blk, blk)
        scores = [None] * ATT_HEADS
        if diagonal:
            key = lax.broadcasted_iota(jnp.int32, (blk, blk), 0)
            qry = lax.broadcasted_iota(jnp.int32, (blk, blk), 1)
            visible = key <= qry
        out = []
        for hd in range(ATT_HEADS):
            m, acc = carry[hd]
            for nxt in range(hd if hd else 0, min(hd + ATT_LOOKAHEAD, ATT_HEADS - 1) + 1):
                if scores[nxt] is None:
                    scores[nxt] = _dot_nt(k_ref[0, pl.ds(start, blk), head_lanes[nxt]], qs[nxt])
            s = jnp.where(visible, scores[hd], -jnp.inf) if diagonal else scores[hd]
            m_new = jnp.maximum(m, jnp.max(s, axis=0, keepdims=True))
            p = jnp.exp2(s - m_new).astype(BF16)
            vt = vt_ref[0, j, hd * V_ROWS:(hd + 1) * V_ROWS, :]
            acc = jnp.exp2(m - m_new) * acc + _dot(vt, p)
            out.append((m_new, acc))
        return tuple(out)

    init = tuple((jnp.full((1, blk), -jnp.inf, F32), jnp.zeros((V_ROWS, blk), F32))
                 for _ in range(ATT_HEADS))
    carry = lax.fori_loop(0, qi, lambda j, c: step(j, c, False), init)
    final = step(qi, carry, True)
    for pair in range(ATT_HEADS // 2):
        o_t = jnp.concatenate([final[hd][1][:V_DIM] / final[hd][1][V_DIM:V_DIM + 1]
                               for hd in (2 * pair, 2 * pair + 1)], axis=0)
        o_ref[0, :, pair * LANES:(pair + 1) * LANES] = o_t.T.astype(BF16)


def _flash_call(q, k, vt):
    b, s, _ = q.shape
    assert vt.shape == (b, s // ATT_BLOCK, MLA_HEADS * V_ROWS, ATT_BLOCK)
    qmap = lambda bi, gi, qi: (bi, qi, gi)
    return pl.pallas_call(
        _flash_body,
        grid=(b, MLA_HEADS // ATT_HEADS, s // ATT_BLOCK),
        in_specs=[pl.BlockSpec((1, ATT_BLOCK, ATT_LANES), qmap),
                  pl.BlockSpec((1, s, ATT_LANES), lambda bi, gi, qi: (bi, 0, gi),
                               pipeline_mode=pl.Buffered(1)),
                  pl.BlockSpec((1, s // ATT_BLOCK, ATT_V_ROWS, ATT_BLOCK),
                               lambda bi, gi, qi: (bi, 0, gi, 0), pipeline_mode=pl.Buffered(1))],
        out_specs=pl.BlockSpec((1, ATT_BLOCK, ATT_HEADS * V_DIM), qmap),
        out_shape=jax.ShapeDtypeStruct((b, s, MLA_WIDTH), BF16),
        compiler_params=_params(3),
        name="prompt_attn",
    )(q, k, vt)


def _mlstm_body(k_ref, qt_ref, vt_ref, ot_ref, gt_ref, gm_ref, hm_out, cxt_out, m_out, cxt_s, m_s):
    ci = pl.program_id(0)
    n_seq = k_ref.shape[0]
    L = CHUNK

    @pl.when(ci == 0)
    def _():
        cxt_s[...] = jnp.zeros_like(cxt_s)
        m_s[...] = jnp.zeros_like(m_s)

    s_idx = lax.broadcasted_iota(jnp.int32, (L, L), 0)
    t_idx = lax.broadcasted_iota(jnp.int32, (L, L), 1)
    causal = s_idx <= t_idx

    gate_rows = [gt_ref[sq, 0] for sq in range(n_seq)]
    prefix = [jnp.dot(rows, causal.astype(F32), precision=lax.Precision.HIGHEST,
                      preferred_element_type=F32) for rows in gate_rows]

    streams = []
    for sq in range(n_seq):
        for hd in range(M_HEADS):
            feat = slice(hd * M_DIM, (hd + 1) * M_DIM)
            idx = sq * M_HEADS + hd
            k = k_ref[sq, :, feat]
            qt = qt_ref[sq, 0, feat, :]
            cxt_prev = cxt_s[idx]
            streams.append(dict(
                sq=sq, hd=hd, feat=feat, idx=idx, k=k, cxt_prev=cxt_prev,
                vt=vt_ref[sq, 0, hd * CELL_V_ROWS:(hd + 1) * CELL_V_ROWS, :],
                m_prev=m_s[idx][0:1, :],
                qk=_dot(k, qt),
                qe=_dot(cxt_prev.astype(BF16), qt)))

    gates = []
    for sq in range(n_seq):
        rows = gate_rows[sq]
        b_rows = pltpu.roll(prefix[sq], M_HEADS, 0)
        head_row = lax.broadcasted_iota(jnp.int32, rows.shape, 0) < M_HEADS
        a_rows = jnp.where(head_row, rows - b_rows, 0.0)
        top = jnp.broadcast_to(jnp.max(a_rows, axis=1, keepdims=True), rows.shape)
        b_last = pltpu.roll(jnp.broadcast_to(jnp.sum(rows, axis=1, keepdims=True), rows.shape),
                            M_HEADS, 0)
        a_cols = jnp.concatenate([a_rows, jnp.zeros((L - SUBLANES, L), F32)], axis=0).T
        gates.append(dict(b_rows=b_rows, a_cols=a_cols, top=top, b_last=b_last,
                          src=jnp.exp(a_rows - top)))

    for st in streams:
        g, hd = gates[st["sq"]], st["hd"]
        weighted_vt = (st["vt"].astype(F32) * g["src"][hd:hd + 1, :]).astype(BF16)
        st["kv"] = _dot(weighted_vt, st["k"])

    for st in streams:
        g, hd = gates[st["sq"]], st["hd"]
        a_col = g["a_cols"][:, hd:hd + 1]
        run_max = jnp.max(jnp.where(causal, a_col, -jnp.inf), axis=0, keepdims=True)
        big_m = jnp.maximum(st["m_prev"], run_max)
        st["sqk"] = st["qk"] * jnp.where(causal, jnp.exp(a_col - big_m), 0.0)
        st["w_inter"] = jnp.exp(st["m_prev"] - big_m)
        st["floor"] = jnp.exp(-(g["b_rows"][hd:hd + 1, :] + big_m))
        st["top"] = g["top"][hd:hd + 1, :]
        st["b_last"] = g["b_last"][hd:hd + 1, :]

    for st in streams:
        ue = _dot(st["vt"], st["sqk"].astype(BF16))
        num = ue[:M_DIM] + st["qe"][:M_DIM] * st["w_inter"]
        den = ue[M_DIM:M_DIM + 1] + st["qe"][M_DIM:M_DIM + 1] * st["w_inter"]
        inv = 1.0 / jnp.maximum(jnp.abs(den), st["floor"])
        t = jax.nn.sigmoid(ot_ref[st["sq"], 0, st["feat"], :]) * num
        ms = jnp.sum(t * t, axis=0, keepdims=True) * (1.0 / M_DIM)
        scale = inv * lax.rsqrt(ms * inv * inv + EPS)
        hm_out[st["sq"], :, st["feat"]] = (t * scale * gm_ref[st["feat"], :]).T.astype(BF16)
        m_last = jnp.maximum(st["m_prev"], st["top"])
        keep = jnp.exp(st["m_prev"] - m_last)
        gain = jnp.exp(st["top"] - m_last)
        cxt_s[st["idx"]] = keep * st["cxt_prev"] + gain * st["kv"]
        m_s[st["idx"]] = jnp.broadcast_to(st["b_last"] + m_last, (SUBLANES, LANES))

    @pl.when(ci == pl.num_programs(0) - 1)
    def _():
        cxt_out[...] = cxt_s[...]
        m_out[...] = m_s[...]


def _mlstm_call(mk, mqt, mvt, mot, gt, gm_lanes):
    b, s, _ = mk.shape
    tile = mqt.shape[-1]
    per_tile = tile // CHUNK
    n_streams = b * M_HEADS
    slab = lambda rows: pl.BlockSpec((b, 1, rows, CHUNK),
                                     lambda ci: (0, ci // per_tile, 0, ci % per_tile))
    tok = pl.BlockSpec((b, CHUNK, MLSTM_WIDTH), lambda ci: (0, ci, 0))
    whole3 = lambda ci: (0, 0, 0)
    return pl.pallas_call(
        _mlstm_body,
        grid=(s // CHUNK,),
        in_specs=[tok, slab(MLSTM_WIDTH), slab(M_HEADS * CELL_V_ROWS), slab(MLSTM_WIDTH),
                  slab(2 * M_HEADS), _resident((MLSTM_WIDTH, LANES))],
        out_specs=[tok,
                   pl.BlockSpec((n_streams, CELL_V_ROWS, M_DIM), whole3),
                   pl.BlockSpec((n_streams, SUBLANES, LANES), whole3)],
        out_shape=[jax.ShapeDtypeStruct((b, s, MLSTM_WIDTH), BF16),
                   jax.ShapeDtypeStruct((n_streams, CELL_V_ROWS, M_DIM), F32),
                   jax.ShapeDtypeStruct((n_streams, SUBLANES, LANES), F32)],
        scratch_shapes=[pltpu.VMEM((n_streams, CELL_V_ROWS, M_DIM), F32),
                        pltpu.VMEM((n_streams, SUBLANES, LANES), F32)],
        compiler_params=_params(1),
        name="prompt_mlstm",
    )(mk, mqt, mvt, mot, gt, gm_lanes)


def _merge_body(h_ref, a_ref, hm_ref, p_ref, ga_ref, woa_ref, wom_ref, gff_ref, wg_ref, wu_ref,
                wd_ref, gple_ref, wpg_ref, wpp_ref, gfin_ref, y_ref):
    a = a_ref[...].astype(F32)
    an = _rms(a, ga_ref[...]).astype(BF16)
    h = h_ref[...] + _dot(an, woa_ref[...]) + _dot(hm_ref[...], wom_ref[...])
    h = h + _swiglu_half(_rms(h, gff_ref[...]).astype(BF16), wg_ref, wu_ref, wd_ref)
    gate = jax.nn.sigmoid(_dot(_rms(h, gple_ref[...]).astype(BF16), wpg_ref[...]))
    h = h + gate * _dot(p_ref[...].astype(BF16), wpp_ref[...])
    y_ref[...] = _rms(h, gfin_ref[...])


def _merge_call(h, a, hm, p, w, tm):
    t = h.shape[0]
    row = lambda i: (i, 0)
    return pl.pallas_call(
        _merge_body,
        grid=(t // tm,),
        in_specs=[pl.BlockSpec((tm, D_MODEL), row), pl.BlockSpec((tm, MLA_WIDTH), row),
                  pl.BlockSpec((tm, MLSTM_WIDTH), row), pl.BlockSpec((tm, PLE_DIM), row),
                  _resident((1, MLA_WIDTH)), _resident((MLA_WIDTH, D_MODEL)),
                  _resident((MLSTM_WIDTH, D_MODEL)), _resident((1, D_MODEL)),
                  _resident((D_MODEL, D_FF)), _resident((D_MODEL, D_FF)), _resident((D_FF, D_MODEL)),
                  _resident((1, D_MODEL)), _resident((D_MODEL, D_MODEL)),
                  _resident((PLE_DIM, D_MODEL)), _resident((1, D_MODEL))],
        out_specs=pl.BlockSpec((tm, D_MODEL), row),
        out_shape=jax.ShapeDtypeStruct((t, D_MODEL), F32),
        compiler_params=_params(1),
        name="merge",
    )(h, a, hm, p, w["g_attn"], w["w_out_a"], w["w_out_m"], w["g_ff2"], w["w_ff2_gate"],
      w["w_ff2_up"], w["w_ff2_down"], w["g_ple"], w["w_ple_gate"], w["w_ple_proj"], w["g_final"])


N_SLOTS = 2
SAMPLE_CHUNKS = 4
SAMPLE_CHUNK_PAGES = N_PAGES // SAMPLE_CHUNKS
SAMPLE_CHUNK_KEYS = SAMPLE_CHUNK_PAGES * PAGE_SIZE


def _absorb_body(q_ref, wabs_ref, qx_ref):
    for hd in range(MLA_HEADS):
        lanes = slice(hd * HEAD_PAD, (hd + 1) * HEAD_PAD)
        qx_ref[hd, :, 0, :] = _dot(q_ref[:, lanes], wabs_ref[hd])


def _absorb_call(q, w_abs):
    nb = q.shape[0]
    return pl.pallas_call(
        _absorb_body,
        out_shape=jax.ShapeDtypeStruct((MLA_HEADS, nb, 1, KV_LORA + LANES), F32),
        name="sample_absorb",
    )(q, w_abs)


def _sample_attn_body(pt_ref, qx_ref, ckv_ref, kr_ref, cache_c, cache_r, wuv_ref,
                      o_ref, cbuf, rbuf, sem_c, sem_r):
    b = pl.program_id(0)
    nb = pl.num_programs(0)
    slot = b % N_SLOTS

    def copies(bi, sl):
        out = []
        for j in range(N_PAGES):
            page = pt_ref[bi, j]
            out.append(pltpu.make_async_copy(cache_c.at[page], cbuf.at[sl, j], sem_c.at[sl]))
            out.append(pltpu.make_async_copy(
                cache_r.at[page], rbuf.at[sl, :, pl.ds(j * PAGE_SIZE, PAGE_SIZE)], sem_r.at[sl]))
        return out

    @pl.when(b == 0)
    def _():
        for bi in range(min(N_SLOTS, DEC_BATCH)):
            for cp in copies(bi, bi):
                cp.start()

    q_ext = qx_ref[:, 0, 0, :]
    q_abs = q_ext[:, :KV_LORA].astype(BF16)
    q_rope = q_ext[:, KV_LORA:KV_LORA + ROPE_DIM].astype(BF16)

    c_new = ckv_ref[0].astype(BF16).astype(F32)
    r_new = kr_ref[0].astype(BF16).astype(F32)
    s_new = (jnp.sum(q_abs.astype(F32) * c_new, axis=1, keepdims=True)
             + jnp.sum(q_rope.astype(F32) * r_new, axis=1, keepdims=True))

    for cp in copies(b, slot):
        cp.wait()

    def chunk_keys(i):
        pages = slice(i * SAMPLE_CHUNK_PAGES, (i + 1) * SAMPLE_CHUNK_PAGES)
        keys = slice(i * SAMPLE_CHUNK_KEYS, (i + 1) * SAMPLE_CHUNK_KEYS)
        kc = cbuf[slot, pages].reshape(SAMPLE_CHUNK_KEYS, KV_LORA).astype(BF16)
        return kc, _dot_nt(q_abs, kc) + _dot(q_rope, rbuf[slot, :, keys].astype(BF16))

    m = s_new
    l = jnp.ones_like(s_new)
    acc = jnp.broadcast_to(c_new, (MLA_HEADS, KV_LORA))
    kc, s = chunk_keys(0)
    for i in range(SAMPLE_CHUNKS):
        nxt = chunk_keys(i + 1) if i + 1 < SAMPLE_CHUNKS else None
        m_new = jnp.maximum(m, jnp.max(s, axis=1, keepdims=True))
        alpha = jnp.exp2(m - m_new)
        p = jnp.exp2(s - m_new)
        l = alpha * l + jnp.sum(p, axis=1, keepdims=True)
        acc = alpha * acc + _dot(p.astype(BF16), kc)
        m = m_new
        if nxt is not None:
            kc, s = nxt
    o_lat = (acc / l).astype(BF16)
    res = _dot(o_lat, wuv_ref[...])
    own_v = (lax.broadcasted_iota(jnp.int32, res.shape, 1) // V_DIM
             == lax.broadcasted_iota(jnp.int32, res.shape, 0))
    o_ref[0] = jnp.sum(jnp.where(own_v, res, 0.0), axis=0, keepdims=True).astype(BF16)

    @pl.when(b + N_SLOTS < nb)
    def _():
        for cp in copies(b + N_SLOTS, slot):
            cp.start()


def _sample_attn_call(page_table, q_ext, ckv, kr, cache_c, cache_r, w_uv):
    nb = q_ext.shape[1]
    tok = lambda bi, pt: (bi, 0, 0)
    whole = lambda shape: pl.BlockSpec(shape, lambda bi, pt: (0,) * len(shape),
                                       pipeline_mode=pl.Buffered(1))
    grid_spec = pltpu.PrefetchScalarGridSpec(
        num_scalar_prefetch=1,
        grid=(nb,),
        in_specs=[pl.BlockSpec((MLA_HEADS, 1, 1, KV_LORA + LANES), lambda bi, pt: (0, bi, 0, 0)),
                  pl.BlockSpec((1, 1, KV_LORA), tok),
                  pl.BlockSpec((1, 1, ROPE_DIM), tok),
                  pl.BlockSpec(memory_space=pl.ANY), pl.BlockSpec(memory_space=pl.ANY),
                  whole((KV_LORA, MLA_WIDTH))],
        out_specs=pl.BlockSpec((1, 1, MLA_WIDTH), tok),
        scratch_shapes=[pltpu.VMEM((N_SLOTS, N_PAGES, PAGE_SIZE, KV_LORA), F32),
                        pltpu.VMEM((N_SLOTS, ROPE_DIM, PAST_LEN), F32),
                        pltpu.SemaphoreType.DMA((N_SLOTS,)), pltpu.SemaphoreType.DMA((N_SLOTS,))])
    return pl.pallas_call(
        _sample_attn_body,
        grid_spec=grid_spec,
        out_shape=jax.ShapeDtypeStruct((nb, 1, MLA_WIDTH), BF16),
        compiler_params=_params(1),
        name="sample_attn",
    )(page_table, q_ext, ckv, kr, cache_c, cache_r, w_uv)


SAMPLE_ROWS = 32


def _sample_mlstm_body(q_ref, k_ref, v_ref, o_ref, g_ref, gm_ref, c_ref, n_ref, m_ref,
                       hm_out, c_out, n_out, m_out):
    hd = pl.program_id(1)
    g = g_ref[...]
    lane = lax.broadcasted_iota(jnp.int32, g.shape, 1)
    ig = jnp.sum(jnp.where(lane == GATE_I_LANE + hd, g, 0.0), axis=1, keepdims=True)
    lf = jnp.sum(jnp.where(lane == GATE_F_LANE + hd, g, 0.0), axis=1, keepdims=True)
    m_all = m_ref[...]
    head_lane = lax.broadcasted_iota(jnp.int32, m_all.shape, 1)
    m0 = jnp.sum(jnp.where(head_lane == hd, m_all, 0.0), axis=1, keepdims=True)
    q = q_ref[...].astype(F32)
    k = k_ref[...].astype(F32)
    v = v_ref[...].astype(F32)
    n0 = n_ref[...]
    m_new = jnp.maximum(lf + m0, ig)
    keep = jnp.exp(lf + m0 - m_new)
    w_src = jnp.exp(ig - m_new)
    sqk = jnp.sum(q * k, axis=1, keepdims=True) * w_src
    wk = w_src * k
    qc = jnp.zeros_like(v)
    for d in range(M_DIM):
        c_d = c_ref[:, 0, d, :]
        qc = qc + q[:, d:d + 1] * c_d
        c_out[:, 0, d, :] = keep * c_d + wk[:, d:d + 1] * v
    num = sqk * v + qc * keep
    den = sqk + jnp.sum(q * n0, axis=1, keepdims=True) * keep
    den = jnp.maximum(jnp.abs(den), jnp.exp(-m_new))
    hh = jax.nn.sigmoid(o_ref[...]) * (num / den)
    hm_out[...] = _rms(hh, gm_ref[...]).astype(BF16)
    n_out[...] = keep * n0 + wk

    @pl.when(hd == 0)
    def _():
        m_out[...] = jnp.broadcast_to(m_new, m_all.shape)

    @pl.when(hd > 0)
    def _():
        m_out[...] = jnp.where(head_lane == hd, m_new, m_out[...])


def _sample_mlstm_call(mq, mk, mv, mo, gates, gm, c0, n0, m0):
    nb = mq.shape[0]
    rows = SAMPLE_ROWS
    tok = lambda bi, hi: (bi, hi)
    return pl.pallas_call(
        _sample_mlstm_body,
        grid=(nb // rows, M_HEADS),
        in_specs=[pl.BlockSpec((rows, M_DIM), tok)] * 4
                 + [pl.BlockSpec((rows, LANES), lambda bi, hi: (bi, 0)),
                    pl.BlockSpec((1, M_DIM), lambda bi, hi: (0, hi)),
                    pl.BlockSpec((rows, 1, M_DIM, M_DIM), lambda bi, hi: (bi, hi, 0, 0)),
                    pl.BlockSpec((rows, M_DIM), tok),
                    pl.BlockSpec((rows, M_HEADS), lambda bi, hi: (bi, 0))],
        out_specs=[pl.BlockSpec((rows, M_DIM), tok),
                   pl.BlockSpec((rows, 1, M_DIM, M_DIM), lambda bi, hi: (bi, hi, 0, 0)),
                   pl.BlockSpec((rows, M_DIM), tok),
                   pl.BlockSpec((rows, M_HEADS), lambda bi, hi: (bi, 0))],
        out_shape=[jax.ShapeDtypeStruct((nb, MLSTM_WIDTH), BF16),
                   jax.ShapeDtypeStruct((nb, M_HEADS, M_DIM, M_DIM), F32),
                   jax.ShapeDtypeStruct((nb, MLSTM_WIDTH), F32),
                   jax.ShapeDtypeStruct((nb, M_HEADS), F32)],
        compiler_params=_params(2),
        name="sample_mlstm",
    )(mq, mk, mv, mo, gates, gm, c0, n0, m0)


def _pad_heads(w, head_dim):
    rows = w.shape[0]
    w = w.reshape(rows, MLA_HEADS, head_dim)
    w = jnp.pad(w, ((0, 0), (0, 0), (0, HEAD_PAD - head_dim)))
    return w.reshape(rows, MLA_PAD_WIDTH)


def _rope_tables(pos):
    inv = ROPE_THETA ** (-jnp.arange(ROPE_HALF, dtype=F32) / ROPE_HALF)
    ang = inv[:, None] * pos.astype(F32)[None, :]
    return jnp.concatenate([jnp.cos(ang), jnp.sin(ang)], axis=0)


def _prep_weights(g_ff1, w_ff1_gate, w_ff1_up, w_ff1_down, g_mix, w_in, g_q, w_uq, g_kv, w_uk,
                  w_uv, b_gate_i, b_gate_f, g_attn_out, g_mlstm_out, w_out, g_ff2, w_ff2_gate,
                  w_ff2_up, w_ff2_down, g_ple, w_ple_gate, w_ple_proj, g_final):
    bf = lambda a: a.astype(BF16)
    row = lambda a: a.reshape(1, -1).astype(F32)
    off_kv, off_kr = Q_LORA, Q_LORA + KV_LORA
    off_m = off_kr + ROPE_DIM
    off_i = off_m + 4 * MLSTM_WIDTH
    small_pad = LANES - ROPE_DIM - 2 * M_HEADS
    w_t = bf(w_in.T)
    w_qs_t = jnp.concatenate([w_t[:off_kv], w_t[off_kr:off_m], w_t[off_i:off_i + 2 * M_HEADS],
                              jnp.zeros((small_pad, D_MODEL), BF16)], axis=0)
    w_qvo_t = jnp.concatenate([w_t[off_m:off_m + MLSTM_WIDTH], w_t[off_m + 2 * MLSTM_WIDTH:off_i]],
                              axis=0)
    gate_bias = jnp.concatenate([jnp.zeros((ROPE_DIM,), F32), b_gate_i, b_gate_f,
                                 jnp.zeros((small_pad,), F32)])
    src = jnp.arange(LANES)[:, None]
    dst = jnp.arange(MLA_PAD_WIDTH)[None, :]
    w_abs = jnp.pad(w_uk.reshape(KV_LORA, MLA_HEADS, NOPE_DIM).transpose(1, 2, 0),
                    ((0, 0), (0, HEAD_PAD - NOPE_DIM), (0, 0))).reshape(MLA_PAD_WIDTH, KV_LORA)
    rope_sel = (dst.T % HEAD_PAD) == (src.T + NOPE_DIM)
    rope_sel = rope_sel & (src.T < ROPE_DIM)
    w_abs = jnp.concatenate([w_abs, rope_sel.astype(F32)], axis=1)
    return dict(
        g_ff1=row(g_ff1), w_ff1_gate=bf(w_ff1_gate), w_ff1_up=bf(w_ff1_up), w_ff1_down=bf(w_ff1_down),
        g_mix=row(g_mix), w_qs_t=w_qs_t, w_kv_t=w_t[off_kv:off_kr],
        w_mk_t=w_t[off_m + MLSTM_WIDTH:off_m + 2 * MLSTM_WIDTH], w_qvo_t=w_qvo_t,
        g_q=row(g_q), w_uq=bf(_pad_heads(w_uq, NOPE_DIM + ROPE_DIM)),
        g_kv=row(g_kv), w_uk=bf(_pad_heads(w_uk, NOPE_DIM)), w_uv=bf(w_uv),
        w_uv_t=bf(jnp.pad(w_uv.T.reshape(MLA_HEADS, V_DIM, KV_LORA),
                          ((0, 0), (0, V_ROWS - V_DIM), (0, 0))).reshape(MLA_HEADS * V_ROWS, KV_LORA)),
        gate_bias=row(gate_bias), w_abs=bf(w_abs).reshape(MLA_HEADS, HEAD_PAD, KV_LORA + LANES),
        g_attn=row(g_attn_out), w_out_a=bf(w_out[:MLA_WIDTH]),
        w_out_m=bf(w_out[MLA_WIDTH:]), g_mlstm=row(g_mlstm_out),
        g_mlstm_lanes=jnp.broadcast_to(g_mlstm_out.reshape(MLSTM_WIDTH, 1).astype(F32),
                                       (MLSTM_WIDTH, LANES)),
        g_ff2=row(g_ff2), w_ff2_gate=bf(w_ff2_gate), w_ff2_up=bf(w_ff2_up), w_ff2_down=bf(w_ff2_down),
        g_ple=row(g_ple), w_ple_gate=bf(w_ple_gate), w_ple_proj=bf(w_ple_proj), g_final=row(g_final))


PROMPT_TILE = 512


def kernel(x_prompt, x_sample, p_prompt, p_sample, cache_ckv, cache_krope, state_C, state_n, state_m, page_table, g_ff1, w_ff1_gate, w_ff1_up, w_ff1_down, g_mix, w_in, g_q, w_uq, g_kv, w_uk, w_uv, b_gate_i, b_gate_f, g_attn_out, g_mlstm_out, w_out, g_ff2, w_ff2_gate, w_ff2_up, w_ff2_down, g_ple, w_ple_gate, w_ple_proj, g_final):
    assert w_in.shape[0] == 1, "single-layer trunk"
    w = _prep_weights(g_ff1[0], w_ff1_gate[0], w_ff1_up[0], w_ff1_down[0], g_mix[0], w_in[0],
                      g_q[0], w_uq[0], g_kv[0], w_uk[0], w_uv[0], b_gate_i[0], b_gate_f[0],
                      g_attn_out[0], g_mlstm_out[0], w_out[0], g_ff2[0], w_ff2_gate[0],
                      w_ff2_up[0], w_ff2_down[0], g_ple[0], w_ple_gate[0], w_ple_proj[0], g_final)
    nb_p, seq, _ = x_prompt.shape
    nb_s = x_sample.shape[0]
    t_p = nb_p * seq

    tab_p = _rope_tables(jnp.arange(seq))
    h_p = _ffn_call(x_prompt.reshape(t_p, D_MODEL), w["g_ff1"], w["w_ff1_gate"], w["w_ff1_up"],
                    w["w_ff1_down"], PROMPT_TILE)
    q_p, k_p, vt_p, ckv_p, _, krt_p, mk_p, mqt_p, mvt_p, mot_p, gt_p = _proj_call(
        h_p, w, tab_p, PROMPT_TILE, cell_feature_major=True)
    seq3 = lambda a: a.reshape(nb_p, seq, a.shape[-1])
    a_p = _flash_call(seq3(q_p), seq3(k_p), vt_p)
    hm_p, cxt_p, m_p = _mlstm_call(seq3(mk_p), mqt_p, mvt_p, mot_p, gt_p, w["g_mlstm_lanes"])
    cxt_p = cxt_p.reshape(nb_p, M_HEADS, CELL_V_ROWS, M_DIM)
    y_p = _merge_call(h_p, a_p.reshape(t_p, MLA_WIDTH), hm_p.reshape(t_p, MLSTM_WIDTH),
                      p_prompt.reshape(t_p, PLE_DIM), w, PROMPT_TILE)

    tab_s = _rope_tables(jnp.full((nb_s,), PAST_LEN, jnp.int32))
    h_s = _ffn_call(x_sample.reshape(nb_s, D_MODEL), w["g_ff1"], w["w_ff1_gate"], w["w_ff1_up"],
                    w["w_ff1_down"], nb_s)
    q_s, _, _, ckv_s, kr_s, krt_s, mq_s, mk_s, mv_s, mo_s, gates_s = _proj_call(
        h_s, w, tab_s, nb_s, cell_feature_major=False)
    n_phys = cache_ckv.shape[1]
    a_s = _sample_attn_call(
        page_table, _absorb_call(q_s, w["w_abs"]), ckv_s.reshape(nb_s, 1, KV_LORA),
        kr_s.reshape(nb_s, 1, ROPE_DIM), cache_ckv.reshape(n_phys, PAGE_SIZE, KV_LORA),
        jnp.swapaxes(cache_krope.reshape(n_phys, PAGE_SIZE, ROPE_DIM), 1, 2), w["w_uv"])
    hm_s, c_s, n_s, m_s = _sample_mlstm_call(
        mq_s, mk_s, mv_s, mo_s, gates_s, w["g_mlstm"], state_C[0].astype(F32),
        state_n[0].astype(F32).reshape(nb_s, MLSTM_WIDTH), state_m[0].astype(F32))
    y_s = _merge_call(h_s, a_s.reshape(nb_s, MLA_WIDTH), hm_s, p_sample.reshape(nb_s, PLE_DIM),
                      w, nb_s)

    return (y_p.reshape(nb_p, seq, D_MODEL), y_s.reshape(nb_s, 1, D_MODEL),
            ckv_p.reshape(1, nb_p, seq, KV_LORA), jnp.swapaxes(krt_p, 1, 2)[None],
            jnp.swapaxes(cxt_p[:, :, :M_DIM, :], 2, 3)[None], cxt_p[None, :, :, M_DIM, :],
            m_p[:, 0, 0].reshape(1, nb_p, M_HEADS),
            ckv_s.reshape(1, nb_s, 1, KV_LORA), jnp.swapaxes(krt_s, 1, 2).reshape(1, nb_s, 1, ROPE_DIM),
            c_s[None], n_s.reshape(1, nb_s, M_HEADS, M_DIM), m_s[None])
```

```python
import functools
import math

import jax
import jax.numpy as jnp
from jax import lax
from jax.experimental import pallas as pl
from jax.experimental.pallas import tpu as pltpu

F32 = jnp.float32
BF16 = jnp.bfloat16

D_MODEL = 1024
SEQ = 8192
DEC_BATCH = 128
PAST_LEN = 8192
PAGE_SIZE = 128
N_PAGES = PAST_LEN // PAGE_SIZE
MLA_HEADS = 8
Q_LORA = 384
KV_LORA = 256
NOPE_DIM = 64
ROPE_DIM = 32
ROPE_HALF = ROPE_DIM // 2
V_DIM = 64
ROPE_THETA = 10000.0
M_HEADS = 4
M_DIM = 128
CHUNK = 128
MLSTM_WIDTH = M_HEADS * M_DIM
D_FF = 2816
PLE_DIM = 256
EPS = 1e-6

LANES = 128
SUBLANES = 8

HEAD_PAD = LANES
MLA_PAD_WIDTH = MLA_HEADS * HEAD_PAD
MLA_WIDTH = MLA_HEADS * V_DIM
V_ROWS = 80
GATE_I_LANE = ROPE_DIM
GATE_F_LANE = ROPE_DIM + M_HEADS
QK_SCALE = (NOPE_DIM + ROPE_DIM) ** -0.5 * math.log2(math.e)

VMEM_LIMIT = 56 * 1024 * 1024


def _dot(a, b):
    return jnp.dot(a, b, preferred_element_type=F32)


def _dot_nt(a, b):
    return lax.dot_general(a, b, (((1,), (1,)), ((), ())), preferred_element_type=F32)


def _rms(x, g):
    ms = jnp.sum(x * x, axis=-1, keepdims=True) * (1.0 / x.shape[-1])
    return x * lax.rsqrt(ms + EPS) * g


def _resident(shape):
    return pl.BlockSpec(shape, lambda *_: (0,) * len(shape), pipeline_mode=pl.Buffered(1))


def _params(n_axes):
    return pltpu.CompilerParams(dimension_semantics=("arbitrary",) * n_axes,
                                vmem_limit_bytes=VMEM_LIMIT)


MXU_TILE = 256
FF_CHUNK_BOUNDS = (0, 6 * MXU_TILE, D_FF)
assert D_FF % MXU_TILE == 0


def _swiglu_half(xn, wg_ref, wu_ref, wd_ref):
    out = None
    for lo, hi in zip(FF_CHUNK_BOUNDS[:-1], FF_CHUNK_BOUNDS[1:]):
        cols = slice(lo, hi)
        gate = _dot(xn, wg_ref[:, cols])
        up = _dot(xn, wu_ref[:, cols])
        act = (jax.nn.silu(gate) * up).astype(BF16)
        part = _dot(act, wd_ref[cols, :])
        out = part if out is None else out + part
    return 0.5 * out


def _ffn_body(x_ref, g_ref, wg_ref, wu_ref, wd_ref, o_ref):
    x = x_ref[...]
    xn = _rms(x, g_ref[...]).astype(BF16)
    o_ref[...] = x + _swiglu_half(xn, wg_ref, wu_ref, wd_ref)


def _ffn_call(x, g, wg, wu, wd, tm):
    t = x.shape[0]
    row = lambda i: (i, 0)
    return pl.pallas_call(
        _ffn_body,
        grid=(t // tm,),
        in_specs=[pl.BlockSpec((tm, D_MODEL), row), _resident((1, D_MODEL)),
                  _resident((D_MODEL, D_FF)), _resident((D_MODEL, D_FF)), _resident((D_FF, D_MODEL))],
        out_specs=pl.BlockSpec((tm, D_MODEL), row),
        out_shape=jax.ShapeDtypeStruct((t, D_MODEL), F32),
        compiler_params=_params(1),
        name="ffn1",
    )(x, g, wg, wu, wd)


def _rope128(x, cos_tab, sin_tab, x2_start):
    lane = lax.broadcasted_iota(jnp.int32, x.shape, 1)
    partner = jnp.where(lane < x2_start, pltpu.roll(x, LANES - ROPE_HALF, 1),
                        pltpu.roll(x, ROPE_HALF, 1))
    return x * cos_tab + partner * sin_tab


CELL_V_ROWS = M_DIM + 16


def _proj_body(cell_feature_major, h_ref, gmix_ref, wqs_ref, wkv_ref, gq_ref, wuq_ref,
               gkv_ref, wuk_ref, wuvt_ref, bias_ref, tab_ref, *refs):
    wmk_ref, wqvot_ref = refs[:2]
    if cell_feature_major:
        q_out, k_out, vt_out, ckv_out, kr_out, krt_out, mk_out, mqt_out, mvt_out, mot_out, gt_out = refs[2:]
    else:
        q_out, k_out, vt_out, ckv_out, kr_out, krt_out, mq_out, mk_out, mv_out, mo_out, gate_out = refs[2:]
    u = _rms(h_ref[...], gmix_ref[...]).astype(BF16)
    ckv = _rms(_dot_nt(u, wkv_ref[...]), gkv_ref[...])
    ckv_out[...] = ckv
    ckv_b = ckv.astype(BF16)
    zqs = _dot_nt(u, wqs_ref[...])
    zs = zqs[:, Q_LORA:Q_LORA + LANES]
    n_tok = zs.shape[0]
    cs = jnp.concatenate([tab_ref[...], jnp.zeros((LANES - ROPE_DIM, n_tok), F32)], axis=0).T
    lane = lax.broadcasted_iota(jnp.int32, zs.shape, 1)
    shifted = lambda by: pltpu.roll(cs, by, 1)
    cos_k = jnp.where(lane < ROPE_HALF, cs, jnp.where(lane < ROPE_DIM, shifted(ROPE_HALF), 0.0))
    sin_k = jnp.where(lane < ROPE_HALF, -shifted(LANES - ROPE_HALF), jnp.where(lane < ROPE_DIM, cs, 0.0))
    kr = _rope128(zs, cos_k, sin_k, ROPE_HALF)
    kr_out[...] = kr[:, :ROPE_DIM]
    krt_out[0] = kr.T[:ROPE_DIM, :]
    kn = _dot(ckv_b, wuk_ref[...])
    kr_head = pltpu.roll(kr, NOPE_DIM, 1)
    for hd in range(MLA_HEADS):
        lanes = slice(hd * HEAD_PAD, (hd + 1) * HEAD_PAD)
        k_out[:, lanes] = (kn[:, lanes] + kr_head).astype(BF16)
    x1 = (lane >= NOPE_DIM) & (lane < NOPE_DIM + ROPE_HALF)
    x2 = (lane >= NOPE_DIM + ROPE_HALF) & (lane < NOPE_DIM + ROPE_DIM)
    cos_q = jnp.where(lane < NOPE_DIM, 1.0, jnp.where(x1, shifted(NOPE_DIM), jnp.where(
        x2, shifted(NOPE_DIM + ROPE_HALF), 0.0)))
    sin_q = jnp.where(x1, -shifted(NOPE_DIM - ROPE_HALF), jnp.where(x2, shifted(NOPE_DIM), 0.0))
    vt = _dot_nt(wuvt_ref[...], ckv_b)
    vrow = lax.broadcasted_iota(jnp.int32, vt.shape, 0)
    vt_out[0, 0] = jnp.where(vrow % V_ROWS == V_DIM, 1.0, vt).astype(BF16)
    qn = _rms(zqs[:, :Q_LORA], gq_ref[...]).astype(BF16)
    q = _dot(qn, wuq_ref[...])
    cos_qs = cos_q * QK_SCALE
    sin_qs = sin_q * QK_SCALE
    for hd in range(MLA_HEADS):
        lanes = slice(hd * HEAD_PAD, (hd + 1) * HEAD_PAD)
        q_out[:, lanes] = _rope128(q[:, lanes], cos_qs, sin_qs, NOPE_DIM + ROPE_HALF).astype(BF16)
    zb = zs + bias_ref[...]
    log_sig = jnp.minimum(zb, 0.0) - jnp.log1p(jnp.exp(-jnp.abs(zb)))
    is_f = (lane >= GATE_F_LANE) & (lane < GATE_F_LANE + M_HEADS)
    gates = jnp.where(is_f, log_sig, zb)
    mk_out[...] = (_dot_nt(u, wmk_ref[...]) * (M_DIM ** -0.5)).astype(BF16)
    if cell_feature_major:
        zt = _dot_nt(wqvot_ref[...], u)
        mqt_out[0, 0] = zt[0:MLSTM_WIDTH].astype(BF16)
        unit_rows = (lax.broadcasted_iota(jnp.int32, (CELL_V_ROWS - M_DIM, n_tok), 0) == 0)
        for hd in range(M_HEADS):
            rows = slice(MLSTM_WIDTH + hd * M_DIM, MLSTM_WIDTH + (hd + 1) * M_DIM)
            mvt_out[0, 0, hd * CELL_V_ROWS:hd * CELL_V_ROWS + M_DIM, :] = zt[rows].astype(BF16)
            mvt_out[0, 0, hd * CELL_V_ROWS + M_DIM:(hd + 1) * CELL_V_ROWS, :] = (
                unit_rows.astype(F32).astype(BF16))
        mot_out[0, 0] = zt[2 * MLSTM_WIDTH:3 * MLSTM_WIDTH]
        gt_out[0, 0] = gates.T[GATE_I_LANE:GATE_I_LANE + 2 * M_HEADS, :]
    else:
        zm = _dot_nt(u, wqvot_ref[...])
        mq_out[...] = zm[:, 0:MLSTM_WIDTH].astype(BF16)
        mv_out[...] = zm[:, MLSTM_WIDTH:2 * MLSTM_WIDTH].astype(BF16)
        mo_out[...] = zm[:, 2 * MLSTM_WIDTH:3 * MLSTM_WIDTH]
        gate_out[...] = gates


def _proj_call(h, w, rope_tab, tm, cell_feature_major):
    t = h.shape[0]
    n_tab = rope_tab.shape[1] // tm
    n_seq = t // tm // n_tab
    row = lambda i: (i, 0)
    tab = lambda i: (0, i % n_tab)
    tok = lambda width, dtype: (jax.ShapeDtypeStruct((t, width), dtype),
                                pl.BlockSpec((tm, width), row))
    slab = lambda rows, dtype: (jax.ShapeDtypeStruct((n_seq, n_tab, rows, tm), dtype),
                                pl.BlockSpec((1, 1, rows, tm), lambda i: (i // n_tab, i % n_tab, 0, 0)))
    outs = [tok(MLA_PAD_WIDTH, BF16), tok(MLA_PAD_WIDTH, BF16), slab(MLA_HEADS * V_ROWS, BF16),
            tok(KV_LORA, F32), tok(ROPE_DIM, F32),
            (jax.ShapeDtypeStruct((n_seq, ROPE_DIM, n_tab * tm), F32),
             pl.BlockSpec((1, ROPE_DIM, tm), lambda i: (i // n_tab, 0, i % n_tab)))]
    if cell_feature_major:
        outs += [tok(MLSTM_WIDTH, BF16), slab(MLSTM_WIDTH, BF16), slab(M_HEADS * CELL_V_ROWS, BF16),
                 slab(MLSTM_WIDTH, F32), slab(2 * M_HEADS, F32)]
    else:
        outs += [tok(MLSTM_WIDTH, BF16)] * 3 + [tok(MLSTM_WIDTH, F32), tok(LANES, F32)]
    return pl.pallas_call(
        functools.partial(_proj_body, cell_feature_major),
        grid=(t // tm,),
        in_specs=[pl.BlockSpec((tm, D_MODEL), row), _resident((1, D_MODEL)),
                  _resident((Q_LORA + LANES, D_MODEL)), _resident((KV_LORA, D_MODEL)),
                  _resident((1, Q_LORA)), _resident((Q_LORA, MLA_PAD_WIDTH)),
                  _resident((1, KV_LORA)), _resident((KV_LORA, MLA_PAD_WIDTH)),
                  _resident((MLA_HEADS * V_ROWS, KV_LORA)),
                  _resident((1, LANES)),
                  pl.BlockSpec((ROPE_DIM, tm), tab),
                  _resident((MLSTM_WIDTH, D_MODEL)), _resident((3 * MLSTM_WIDTH, D_MODEL))],
        out_specs=[o[1] for o in outs],
        out_shape=[o[0] for o in outs],
        compiler_params=_params(1),
        name="proj",
    )(h, w["g_mix"], w["w_qs_t"], w["w_kv_t"], w["g_q"], w["w_uq"], w["g_kv"],
      w["w_uk"], w["w_uv_t"], w["gate_bias"], rope_tab, w["w_mk_t"], w["w_qvo_t"])


ATT_BLOCK = 512
ATT_HEADS = 8
ATT_LANES = ATT_HEADS * HEAD_PAD
ATT_V_ROWS = ATT_HEADS * V_ROWS
ATT_LOOKAHEAD = 3


def _flash_body(q_ref, k_ref, vt_ref, o_ref):
    qi = pl.program_id(2)
    blk = ATT_BLOCK
    head_lanes = [slice(hd * HEAD_PAD, (hd + 1) * HEAD_PAD) for hd in range(ATT_HEADS)]
    qs = [q_ref[0, :, lanes] for lanes in head_lanes]

    def step(j, carry, diagonal):
        start = pl.multiple_of(j * blk, blk)
        scores = [None] * ATT_HEADS
        if diagonal:
            key = lax.broadcasted_iota(jnp.int32, (blk, blk), 0)
            qry = lax.broadcasted_iota(jnp.int32, (blk, blk), 1)
            visible = key <= qry
        out = []
        for hd in range(ATT_HEADS):
            m, acc = carry[hd]
            for nxt in range(hd if hd else 0, min(hd + ATT_LOOKAHEAD, ATT_HEADS - 1) + 1):
                if scores[nxt] is None:
                    scores[nxt] = _dot_nt(k_ref[0, pl.ds(start, blk), head_lanes[nxt]], qs[nxt])
            s = jnp.where(visible, scores[hd], -jnp.inf) if diagonal else scores[hd]
            m_new = jnp.maximum(m, jnp.max(s, axis=0, keepdims=True))
            p = jnp.exp2(s - m_new).astype(BF16)
            vt = vt_ref[0, j, hd * V_ROWS:(hd + 1) * V_ROWS, :]
            acc = jnp.exp2(m - m_new) * acc + _dot(vt, p)
            out.append((m_new, acc))
        return tuple(out)

    init = tuple((jnp.full((1, blk), -jnp.inf, F32), jnp.zeros((V_ROWS, blk), F32))
                 for _ in range(ATT_HEADS))
    carry = lax.fori_loop(0, qi, lambda j, c: step(j, c, False), init)
    final = step(qi, carry, True)
    for pair in range(ATT_HEADS // 2):
        o_t = jnp.concatenate([final[hd][1][:V_DIM] / final[hd][1][V_DIM:V_DIM + 1]
                               for hd in (2 * pair, 2 * pair + 1)], axis=0)
        o_ref[0, :, pair * LANES:(pair + 1) * LANES] = o_t.T.astype(BF16)


def _flash_call(q, k, vt):
    b, s, _ = q.shape
    assert vt.shape == (b, s // ATT_BLOCK, MLA_HEADS * V_ROWS, ATT_BLOCK)
    qmap = lambda bi, gi, qi: (bi, qi, gi)
    return pl.pallas_call(
        _flash_body,
        grid=(b, MLA_HEADS // ATT_HEADS, s // ATT_BLOCK),
        in_specs=[pl.BlockSpec((1, ATT_BLOCK, ATT_LANES), qmap),
                  pl.BlockSpec((1, s, ATT_LANES), lambda bi, gi, qi: (bi, 0, gi),
                               pipeline_mode=pl.Buffered(1)),
                  pl.BlockSpec((1, s // ATT_BLOCK, ATT_V_ROWS, ATT_BLOCK),
                               lambda bi, gi, qi: (bi, 0, gi, 0), pipeline_mode=pl.Buffered(1))],
        out_specs=pl.BlockSpec((1, ATT_BLOCK, ATT_HEADS * V_DIM), qmap),
        out_shape=jax.ShapeDtypeStruct((b, s, MLA_WIDTH), BF16),
        compiler_params=_params(3),
        name="prompt_attn",
    )(q, k, vt)


def _mlstm_body(k_ref, qt_ref, vt_ref, ot_ref, gt_ref, gm_ref, hm_out, cxt_out, m_out, cxt_s, m_s):
    ci = pl.program_id(0)
    n_seq = k_ref.shape[0]
    L = CHUNK

    @pl.when(ci == 0)
    def _():
        cxt_s[...] = jnp.zeros_like(cxt_s)
        m_s[...] = jnp.zeros_like(m_s)

    s_idx = lax.broadcasted_iota(jnp.int32, (L, L), 0)
    t_idx = lax.broadcasted_iota(jnp.int32, (L, L), 1)
    causal = s_idx <= t_idx

    gate_rows = [gt_ref[sq, 0] for sq in range(n_seq)]
    prefix = [jnp.dot(rows, causal.astype(F32), precision=lax.Precision.HIGHEST,
                      preferred_element_type=F32) for rows in gate_rows]

    streams = []
    for sq in range(n_seq):
        for hd in range(M_HEADS):
            feat = slice(hd * M_DIM, (hd + 1) * M_DIM)
            idx = sq * M_HEADS + hd
            k = k_ref[sq, :, feat]
            qt = qt_ref[sq, 0, feat, :]
            cxt_prev = cxt_s[idx]
            streams.append(dict(
                sq=sq, hd=hd, feat=feat, idx=idx, k=k, cxt_prev=cxt_prev,
                vt=vt_ref[sq, 0, hd * CELL_V_ROWS:(hd + 1) * CELL_V_ROWS, :],
                m_prev=m_s[idx][0:1, :],
                qk=_dot(k, qt),
                qe=_dot(cxt_prev.astype(BF16), qt)))

    gates = []
    for sq in range(n_seq):
        rows = gate_rows[sq]
        b_rows = pltpu.roll(prefix[sq], M_HEADS, 0)
        head_row = lax.broadcasted_iota(jnp.int32, rows.shape, 0) < M_HEADS
        a_rows = jnp.where(head_row, rows - b_rows, 0.0)
        top = jnp.broadcast_to(jnp.max(a_rows, axis=1, keepdims=True), rows.shape)
        b_last = pltpu.roll(jnp.broadcast_to(jnp.sum(rows, axis=1, keepdims=True), rows.shape),
                            M_HEADS, 0)
        a_cols = jnp.concatenate([a_rows, jnp.zeros((L - SUBLANES, L), F32)], axis=0).T
        gates.append(dict(b_rows=b_rows, a_cols=a_cols, top=top, b_last=b_last,
                          src=jnp.exp(a_rows - top)))

    for st in streams:
        g, hd = gates[st["sq"]], st["hd"]
        weighted_vt = (st["vt"].astype(F32) * g["src"][hd:hd + 1, :]).astype(BF16)
        st["kv"] = _dot(weighted_vt, st["k"])

    for st in streams:
        g, hd = gates[st["sq"]], st["hd"]
        a_col = g["a_cols"][:, hd:hd + 1]
        run_max = jnp.max(jnp.where(causal, a_col, -jnp.inf), axis=0, keepdims=True)
        big_m = jnp.maximum(st["m_prev"], run_max)
        st["sqk"] = st["qk"] * jnp.where(causal, jnp.exp(a_col - big_m), 0.0)
        st["w_inter"] = jnp.exp(st["m_prev"] - big_m)
        st["floor"] = jnp.exp(-(g["b_rows"][hd:hd + 1, :] + big_m))
        st["top"] = g["top"][hd:hd + 1, :]
        st["b_last"] = g["b_last"][hd:hd + 1, :]

    for st in streams:
        ue = _dot(st["vt"], st["sqk"].astype(BF16))
        num = ue[:M_DIM] + st["qe"][:M_DIM] * st["w_inter"]
        den = ue[M_DIM:M_DIM + 1] + st["qe"][M_DIM:M_DIM + 1] * st["w_inter"]
        inv = 1.0 / jnp.maximum(jnp.abs(den), st["floor"])
        t = jax.nn.sigmoid(ot_ref[st["sq"], 0, st["feat"], :]) * num
        ms = jnp.sum(t * t, axis=0, keepdims=True) * (1.0 / M_DIM)
        scale = inv * lax.rsqrt(ms * inv * inv + EPS)
        hm_out[st["sq"], :, st["feat"]] = (t * scale * gm_ref[st["feat"], :]).T.astype(BF16)
        m_last = jnp.maximum(st["m_prev"], st["top"])
        keep = jnp.exp(st["m_prev"] - m_last)
        gain = jnp.exp(st["top"] - m_last)
        cxt_s[st["idx"]] = keep * st["cxt_prev"] + gain * st["kv"]
        m_s[st["idx"]] = jnp.broadcast_to(st["b_last"] + m_last, (SUBLANES, LANES))

    @pl.when(ci == pl.num_programs(0) - 1)
    def _():
        cxt_out[...] = cxt_s[...]
        m_out[...] = m_s[...]


def _mlstm_call(mk, mqt, mvt, mot, gt, gm_lanes):
    b, s, _ = mk.shape
    tile = mqt.shape[-1]
    per_tile = tile // CHUNK
    n_streams = b * M_HEADS
    slab = lambda rows: pl.BlockSpec((b, 1, rows, CHUNK),
                                     lambda ci: (0, ci // per_tile, 0, ci % per_tile))
    tok = pl.BlockSpec((b, CHUNK, MLSTM_WIDTH), lambda ci: (0, ci, 0))
    whole3 = lambda ci: (0, 0, 0)
    return pl.pallas_call(
        _mlstm_body,
        grid=(s // CHUNK,),
        in_specs=[tok, slab(MLSTM_WIDTH), slab(M_HEADS * CELL_V_ROWS), slab(MLSTM_WIDTH),
                  slab(2 * M_HEADS), _resident((MLSTM_WIDTH, LANES))],
        out_specs=[tok,
                   pl.BlockSpec((n_streams, CELL_V_ROWS, M_DIM), whole3),
                   pl.BlockSpec((n_streams, SUBLANES, LANES), whole3)],
        out_shape=[jax.ShapeDtypeStruct((b, s, MLSTM_WIDTH), BF16),
                   jax.ShapeDtypeStruct((n_streams, CELL_V_ROWS, M_DIM), F32),
                   jax.ShapeDtypeStruct((n_streams, SUBLANES, LANES), F32)],
        scratch_shapes=[pltpu.VMEM((n_streams, CELL_V_ROWS, M_DIM), F32),
                        pltpu.VMEM((n_streams, SUBLANES, LANES), F32)],
        compiler_params=_params(1),
        name="prompt_mlstm",
    )(mk, mqt, mvt, mot, gt, gm_lanes)


def _merge_body(h_ref, a_ref, hm_ref, p_ref, ga_ref, woa_ref, wom_ref, gff_ref, wg_ref, wu_ref,
                wd_ref, gple_ref, wpg_ref, wpp_ref, gfin_ref, y_ref):
    a = a_ref[...].astype(F32)
    an = _rms(a, ga_ref[...]).astype(BF16)
    h = h_ref[...] + _dot(an, woa_ref[...]) + _dot(hm_ref[...], wom_ref[...])
    h = h + _swiglu_half(_rms(h, gff_ref[...]).astype(BF16), wg_ref, wu_ref, wd_ref)
    gate = jax.nn.sigmoid(_dot(_rms(h, gple_ref[...]).astype(BF16), wpg_ref[...]))
    h = h + gate * _dot(p_ref[...].astype(BF16), wpp_ref[...])
    y_ref[...] = _rms(h, gfin_ref[...])


def _merge_call(h, a, hm, p, w, tm):
    t = h.shape[0]
    row = lambda i: (i, 0)
    return pl.pallas_call(
        _merge_body,
        grid=(t // tm,),
        in_specs=[pl.BlockSpec((tm, D_MODEL), row), pl.BlockSpec((tm, MLA_WIDTH), row),
                  pl.BlockSpec((tm, MLSTM_WIDTH), row), pl.BlockSpec((tm, PLE_DIM), row),
                  _resident((1, MLA_WIDTH)), _resident((MLA_WIDTH, D_MODEL)),
                  _resident((MLSTM_WIDTH, D_MODEL)), _resident((1, D_MODEL)),
                  _resident((D_MODEL, D_FF)), _resident((D_MODEL, D_FF)), _resident((D_FF, D_MODEL)),
                  _resident((1, D_MODEL)), _resident((D_MODEL, D_MODEL)),
                  _resident((PLE_DIM, D_MODEL)), _resident((1, D_MODEL))],
        out_specs=pl.BlockSpec((tm, D_MODEL), row),
        out_shape=jax.ShapeDtypeStruct((t, D_MODEL), F32),
        compiler_params=_params(1),
        name="merge",
    )(h, a, hm, p, w["g_attn"], w["w_out_a"], w["w_out_m"], w["g_ff2"], w["w_ff2_gate"],
      w["w_ff2_up"], w["w_ff2_down"], w["g_ple"], w["w_ple_gate"], w["w_ple_proj"], w["g_final"])


N_SLOTS = 2
SAMPLE_CHUNKS = 4
SAMPLE_CHUNK_PAGES = N_PAGES // SAMPLE_CHUNKS
SAMPLE_CHUNK_KEYS = SAMPLE_CHUNK_PAGES * PAGE_SIZE


def _absorb_body(q_ref, wabs_ref, qx_ref):
    for hd in range(MLA_HEADS):
        lanes = slice(hd * HEAD_PAD, (hd + 1) * HEAD_PAD)
        qx_ref[hd, :, 0, :] = _dot(q_ref[:, lanes], wabs_ref[hd])


def _absorb_call(q, w_abs):
    nb = q.shape[0]
    return pl.pallas_call(
        _absorb_body,
        out_shape=jax.ShapeDtypeStruct((MLA_HEADS, nb, 1, KV_LORA + LANES), F32),
        name="sample_absorb",
    )(q, w_abs)


def _sample_attn_body(pt_ref, qx_ref, ckv_ref, kr_ref, cache_c, cache_r, wuv_ref,
                      o_ref, cbuf, rbuf, sem_c, sem_r):
    b = pl.program_id(0)
    nb = pl.num_programs(0)
    slot = b % N_SLOTS

    def copies(bi, sl):
        out = []
        for j in range(N_PAGES):
            page = pt_ref[bi, j]
            out.append(pltpu.make_async_copy(cache_c.at[page], cbuf.at[sl, j], sem_c.at[sl]))
            out.append(pltpu.make_async_copy(
                cache_r.at[page], rbuf.at[sl, :, pl.ds(j * PAGE_SIZE, PAGE_SIZE)], sem_r.at[sl]))
        return out

    def start(bi, sl):
        for i, cp in enumerate(copies(bi, sl)):
            cp.start(priority=i % 2)

    @pl.when(b == 0)
    def _():
        for bi in range(min(N_SLOTS, DEC_BATCH)):
            start(bi, bi)

    q_ext = qx_ref[:, 0, 0, :]
    q_abs = q_ext[:, :KV_LORA].astype(BF16)
    q_rope = q_ext[:, KV_LORA:KV_LORA + ROPE_DIM].astype(BF16)

    c_new = ckv_ref[0].astype(BF16).astype(F32)
    r_new = kr_ref[0].astype(BF16).astype(F32)
    s_new = (jnp.sum(q_abs.astype(F32) * c_new, axis=1, keepdims=True)
             + jnp.sum(q_rope.astype(F32) * r_new, axis=1, keepdims=True))

    for cp in copies(b, slot):
        cp.wait()

    def chunk_keys(i):
        pages = slice(i * SAMPLE_CHUNK_PAGES, (i + 1) * SAMPLE_CHUNK_PAGES)
        keys = slice(i * SAMPLE_CHUNK_KEYS, (i + 1) * SAMPLE_CHUNK_KEYS)
        kc = cbuf[slot, pages].reshape(SAMPLE_CHUNK_KEYS, KV_LORA).astype(BF16)
        return kc, _dot_nt(q_abs, kc) + _dot(q_rope, rbuf[slot, :, keys].astype(BF16))

    m = s_new
    l = jnp.ones_like(s_new)
    acc = jnp.broadcast_to(c_new, (MLA_HEADS, KV_LORA))
    kc, s = chunk_keys(0)
    for i in range(SAMPLE_CHUNKS):
        nxt = chunk_keys(i + 1) if i + 1 < SAMPLE_CHUNKS else None
        m_new = jnp.maximum(m, jnp.max(s, axis=1, keepdims=True))
        alpha = jnp.exp2(m - m_new)
        p = jnp.exp2(s - m_new)
        l = alpha * l + jnp.sum(p, axis=1, keepdims=True)
        acc = alpha * acc + _dot(p.astype(BF16), kc)
        m = m_new
        if nxt is not None:
            kc, s = nxt
    o_lat = (acc / l).astype(BF16)
    res = _dot(o_lat, wuv_ref[...])
    own_v = (lax.broadcasted_iota(jnp.int32, res.shape, 1) // V_DIM
             == lax.broadcasted_iota(jnp.int32, res.shape, 0))
    o_ref[0] = jnp.sum(jnp.where(own_v, res, 0.0), axis=0, keepdims=True).astype(BF16)

    @pl.when(b + N_SLOTS < nb)
    def _():
        start(b + N_SLOTS, slot)


def _sample_attn_call(page_table, q_ext, ckv, kr, cache_c, cache_r, w_uv):
    nb = q_ext.shape[1]
    tok = lambda bi, pt: (bi, 0, 0)
    whole = lambda shape: pl.BlockSpec(shape, lambda bi, pt: (0,) * len(shape),
                                       pipeline_mode=pl.Buffered(1))
    grid_spec = pltpu.PrefetchScalarGridSpec(
        num_scalar_prefetch=1,
        grid=(nb,),
        in_specs=[pl.BlockSpec((MLA_HEADS, 1, 1, KV_LORA + LANES), lambda bi, pt: (0, bi, 0, 0)),
                  pl.BlockSpec((1, 1, KV_LORA), tok),
                  pl.BlockSpec((1, 1, ROPE_DIM), tok),
                  pl.BlockSpec(memory_space=pl.ANY), pl.BlockSpec(memory_space=pl.ANY),
                  whole((KV_LORA, MLA_WIDTH))],
        out_specs=pl.BlockSpec((1, 1, MLA_WIDTH), tok),
        scratch_shapes=[pltpu.VMEM((N_SLOTS, N_PAGES, PAGE_SIZE, KV_LORA), F32),
                        pltpu.VMEM((N_SLOTS, ROPE_DIM, PAST_LEN), F32),
                        pltpu.SemaphoreType.DMA((N_SLOTS,)), pltpu.SemaphoreType.DMA((N_SLOTS,))])
    return pl.pallas_call(
        _sample_attn_body,
        grid_spec=grid_spec,
        out_shape=jax.ShapeDtypeStruct((nb, 1, MLA_WIDTH), BF16),
        compiler_params=_params(1),
        name="sample_attn",
    )(page_table, q_ext, ckv, kr, cache_c, cache_r, w_uv)


SAMPLE_ROWS = 32


def _sample_mlstm_body(q_ref, k_ref, v_ref, o_ref, g_ref, gm_ref, c_ref, n_ref, m_ref,
                       hm_out, c_out, n_out, m_out):
    hd = pl.program_id(1)
    g = g_ref[...]
    lane = lax.broadcasted_iota(jnp.int32, g.shape, 1)
    ig = jnp.sum(jnp.where(lane == GATE_I_LANE + hd, g, 0.0), axis=1, keepdims=True)
    lf = jnp.sum(jnp.where(lane == GATE_F_LANE + hd, g, 0.0), axis=1, keepdims=True)
    m_all = m_ref[...]
    head_lane = lax.broadcasted_iota(jnp.int32, m_all.shape, 1)
    m0 = jnp.sum(jnp.where(head_lane == hd, m_all, 0.0), axis=1, keepdims=True)
    q = q_ref[...].astype(F32)
    k = k_ref[...].astype(F32)
    v = v_ref[...].astype(F32)
    n0 = n_ref[...]
    m_new = jnp.maximum(lf + m0, ig)
    keep = jnp.exp(lf + m0 - m_new)
    w_src = jnp.exp(ig - m_new)
    sqk = jnp.sum(q * k, axis=1, keepdims=True) * w_src
    wk = w_src * k
    qc = jnp.zeros_like(v)
    for d in range(M_DIM):
        c_d = c_ref[:, 0, d, :]
        qc = qc + q[:, d:d + 1] * c_d
        c_out[:, 0, d, :] = keep * c_d + wk[:, d:d + 1] * v
    num = sqk * v + qc * keep
    den = sqk + jnp.sum(q * n0, axis=1, keepdims=True) * keep
    den = jnp.maximum(jnp.abs(den), jnp.exp(-m_new))
    hh = jax.nn.sigmoid(o_ref[...]) * (num / den)
    hm_out[...] = _rms(hh, gm_ref[...]).astype(BF16)
    n_out[...] = keep * n0 + wk

    @pl.when(hd == 0)
    def _():
        m_out[...] = jnp.broadcast_to(m_new, m_all.shape)

    @pl.when(hd > 0)
    def _():
        m_out[...] = jnp.where(head_lane == hd, m_new, m_out[...])


def _sample_mlstm_call(mq, mk, mv, mo, gates, gm, c0, n0, m0):
    nb = mq.shape[0]
    rows = SAMPLE_ROWS
    tok = lambda bi, hi: (bi, hi)
    return pl.pallas_call(
        _sample_mlstm_body,
        grid=(nb // rows, M_HEADS),
        in_specs=[pl.BlockSpec((rows, M_DIM), tok)] * 4
                 + [pl.BlockSpec((rows, LANES), lambda bi, hi: (bi, 0)),
                    pl.BlockSpec((1, M_DIM), lambda bi, hi: (0, hi)),
                    pl.BlockSpec((rows, 1, M_DIM, M_DIM), lambda bi, hi: (bi, hi, 0, 0)),
                    pl.BlockSpec((rows, M_DIM), tok),
                    pl.BlockSpec((rows, M_HEADS), lambda bi, hi: (bi, 0))],
        out_specs=[pl.BlockSpec((rows, M_DIM), tok),
                   pl.BlockSpec((rows, 1, M_DIM, M_DIM), lambda bi, hi: (bi, hi, 0, 0)),
                   pl.BlockSpec((rows, M_DIM), tok),
                   pl.BlockSpec((rows, M_HEADS), lambda bi, hi: (bi, 0))],
        out_shape=[jax.ShapeDtypeStruct((nb, MLSTM_WIDTH), BF16),
                   jax.ShapeDtypeStruct((nb, M_HEADS, M_DIM, M_DIM), F32),
                   jax.ShapeDtypeStruct((nb, MLSTM_WIDTH), F32),
                   jax.ShapeDtypeStruct((nb, M_HEADS), F32)],
        compiler_params=_params(2),
        name="sample_mlstm",
    )(mq, mk, mv, mo, gates, gm, c0, n0, m0)


def _pad_heads(w, head_dim):
    rows = w.shape[0]
    w = w.reshape(rows, MLA_HEADS, head_dim)
    w = jnp.pad(w, ((0, 0), (0, 0), (0, HEAD_PAD - head_dim)))
    return w.reshape(rows, MLA_PAD_WIDTH)


def _rope_tables(pos):
    inv = ROPE_THETA ** (-jnp.arange(ROPE_HALF, dtype=F32) / ROPE_HALF)
    ang = inv[:, None] * pos.astype(F32)[None, :]
    return jnp.concatenate([jnp.cos(ang), jnp.sin(ang)], axis=0)


def _prep_weights(g_ff1, w_ff1_gate, w_ff1_up, w_ff1_down, g_mix, w_in, g_q, w_uq, g_kv, w_uk,
                  w_uv, b_gate_i, b_gate_f, g_attn_out, g_mlstm_out, w_out, g_ff2, w_ff2_gate,
                  w_ff2_up, w_ff2_down, g_ple, w_ple_gate, w_ple_proj, g_final):
    bf = lambda a: a.astype(BF16)
    row = lambda a: a.reshape(1, -1).astype(F32)
    off_kv, off_kr = Q_LORA, Q_LORA + KV_LORA
    off_m = off_kr + ROPE_DIM
    off_i = off_m + 4 * MLSTM_WIDTH
    small_pad = LANES - ROPE_DIM - 2 * M_HEADS
    w_t = bf(w_in.T)
    w_qs_t = jnp.concatenate([w_t[:off_kv], w_t[off_kr:off_m], w_t[off_i:off_i + 2 * M_HEADS],
                              jnp.zeros((small_pad, D_MODEL), BF16)], axis=0)
    w_qvo_t = jnp.concatenate([w_t[off_m:off_m + MLSTM_WIDTH], w_t[off_m + 2 * MLSTM_WIDTH:off_i]],
                              axis=0)
    gate_bias = jnp.concatenate([jnp.zeros((ROPE_DIM,), F32), b_gate_i, b_gate_f,
                                 jnp.zeros((small_pad,), F32)])
    src = jnp.arange(LANES)[:, None]
    dst = jnp.arange(MLA_PAD_WIDTH)[None, :]
    w_abs = jnp.pad(w_uk.reshape(KV_LORA, MLA_HEADS, NOPE_DIM).transpose(1, 2, 0),
                    ((0, 0), (0, HEAD_PAD - NOPE_DIM), (0, 0))).reshape(MLA_PAD_WIDTH, KV_LORA)
    rope_sel = (dst.T % HEAD_PAD) == (src.T + NOPE_DIM)
    rope_sel = rope_sel & (src.T < ROPE_DIM)
    w_abs = jnp.concatenate([w_abs, rope_sel.astype(F32)], axis=1)
    return dict(
        g_ff1=row(g_ff1), w_ff1_gate=bf(w_ff1_gate), w_ff1_up=bf(w_ff1_up), w_ff1_down=bf(w_ff1_down),
        g_mix=row(g_mix), w_qs_t=w_qs_t, w_kv_t=w_t[off_kv:off_kr],
        w_mk_t=w_t[off_m + MLSTM_WIDTH:off_m + 2 * MLSTM_WIDTH], w_qvo_t=w_qvo_t,
        g_q=row(g_q), w_uq=bf(_pad_heads(w_uq, NOPE_DIM + ROPE_DIM)),
        g_kv=row(g_kv), w_uk=bf(_pad_heads(w_uk, NOPE_DIM)), w_uv=bf(w_uv),
        w_uv_t=bf(jnp.pad(w_uv.T.reshape(MLA_HEADS, V_DIM, KV_LORA),
                          ((0, 0), (0, V_ROWS - V_DIM), (0, 0))).reshape(MLA_HEADS * V_ROWS, KV_LORA)),
        gate_bias=row(gate_bias), w_abs=bf(w_abs).reshape(MLA_HEADS, HEAD_PAD, KV_LORA + LANES),
        g_attn=row(g_attn_out), w_out_a=bf(w_out[:MLA_WIDTH]),
        w_out_m=bf(w_out[MLA_WIDTH:]), g_mlstm=row(g_mlstm_out),
        g_mlstm_lanes=jnp.broadcast_to(g_mlstm_out.reshape(MLSTM_WIDTH, 1).astype(F32),
                                       (MLSTM_WIDTH, LANES)),
        g_ff2=row(g_ff2), w_ff2_gate=bf(w_ff2_gate), w_ff2_up=bf(w_ff2_up), w_ff2_down=bf(w_ff2_down),
        g_ple=row(g_ple), w_ple_gate=bf(w_ple_gate), w_ple_proj=bf(w_ple_proj), g_final=row(g_final))


PROMPT_TILE = 512


def kernel(x_prompt, x_sample, p_prompt, p_sample, cache_ckv, cache_krope, state_C, state_n, state_m, page_table, g_ff1, w_ff1_gate, w_ff1_up, w_ff1_down, g_mix, w_in, g_q, w_uq, g_kv, w_uk, w_uv, b_gate_i, b_gate_f, g_attn_out, g_mlstm_out, w_out, g_ff2, w_ff2_gate, w_ff2_up, w_ff2_down, g_ple, w_ple_gate, w_ple_proj, g_final):
    assert w_in.shape[0] == 1, "single-layer trunk"
    w = _prep_weights(g_ff1[0], w_ff1_gate[0], w_ff1_up[0], w_ff1_down[0], g_mix[0], w_in[0],
                      g_q[0], w_uq[0], g_kv[0], w_uk[0], w_uv[0], b_gate_i[0], b_gate_f[0],
                      g_attn_out[0], g_mlstm_out[0], w_out[0], g_ff2[0], w_ff2_gate[0],
                      w_ff2_up[0], w_ff2_down[0], g_ple[0], w_ple_gate[0], w_ple_proj[0], g_final)
    nb_p, seq, _ = x_prompt.shape
    nb_s = x_sample.shape[0]
    t_p = nb_p * seq

    tab_p = _rope_tables(jnp.arange(seq))
    h_p = _ffn_call(x_prompt.reshape(t_p, D_MODEL), w["g_ff1"], w["w_ff1_gate"], w["w_ff1_up"],
                    w["w_ff1_down"], PROMPT_TILE)
    q_p, k_p, vt_p, ckv_p, _, krt_p, mk_p, mqt_p, mvt_p, mot_p, gt_p = _proj_call(
        h_p, w, tab_p, PROMPT_TILE, cell_feature_major=True)
    seq3 = lambda a: a.reshape(nb_p, seq, a.shape[-1])
    a_p = _flash_call(seq3(q_p), seq3(k_p), vt_p)
    hm_p, cxt_p, m_p = _mlstm_call(seq3(mk_p), mqt_p, mvt_p, mot_p, gt_p, w["g_mlstm_lanes"])
    cxt_p = cxt_p.reshape(nb_p, M_HEADS, CELL_V_ROWS, M_DIM)
    y_p = _merge_call(h_p, a_p.reshape(t_p, MLA_WIDTH), hm_p.reshape(t_p, MLSTM_WIDTH),
                      p_prompt.reshape(t_p, PLE_DIM), w, PROMPT_TILE)

    tab_s = _rope_tables(jnp.full((nb_s,), PAST_LEN, jnp.int32))
    h_s = _ffn_call(x_sample.reshape(nb_s, D_MODEL), w["g_ff1"], w["w_ff1_gate"], w["w_ff1_up"],
                    w["w_ff1_down"], nb_s)
    q_s, _, _, ckv_s, kr_s, krt_s, mq_s, mk_s, mv_s, mo_s, gates_s = _proj_call(
        h_s, w, tab_s, nb_s, cell_feature_major=False)
    n_phys = cache_ckv.shape[1]
    a_s = _sample_attn_call(
        page_table, _absorb_call(q_s, w["w_abs"]), ckv_s.reshape(nb_s, 1, KV_LORA),
        kr_s.reshape(nb_s, 1, ROPE_DIM), cache_ckv.reshape(n_phys, PAGE_SIZE, KV_LORA),
        jnp.swapaxes(cache_krope.reshape(n_phys, PAGE_SIZE, ROPE_DIM), 1, 2), w["w_uv"])
    hm_s, c_s, n_s, m_s = _sample_mlstm_call(
        mq_s, mk_s, mv_s, mo_s, gates_s, w["g_mlstm"], state_C[0].astype(F32),
        state_n[0].astype(F32).reshape(nb_s, MLSTM_WIDTH), state_m[0].astype(F32))
    y_s = _merge_call(h_s, a_s.reshape(nb_s, MLA_WIDTH), hm_s, p_sample.reshape(nb_s, PLE_DIM),
                      w, nb_s)

    return (y_p.reshape(nb_p, seq, D_MODEL), y_s.reshape(nb_s, 1, D_MODEL),
            ckv_p.reshape(1, nb_p, seq, KV_LORA), jnp.swapaxes(krt_p, 1, 2)[None],
            jnp.swapaxes(cxt_p[:, :, :M_DIM, :], 2, 3)[None], cxt_p[None, :, :, M_DIM, :],
            m_p[:, 0, 0].reshape(1, nb_p, M_HEADS),
            ckv_s.reshape(1, nb_s, 1, KV_LORA), jnp.swapaxes(krt_s, 1, 2).reshape(1, nb_s, 1, ROPE_DIM),
            c_s[None], n_s.reshape(1, nb_s, M_HEADS, M_DIM), m_s[None])
```

```python
import functools
import math

import jax
import jax.numpy as jnp
from jax import lax
from jax.experimental import pallas as pl
from jax.experimental.pallas import tpu as pltpu

F32 = jnp.float32
BF16 = jnp.bfloat16

D_MODEL = 1024
SEQ = 8192
DEC_BATCH = 128
PAST_LEN = 8192
PAGE_SIZE = 128
N_PAGES = PAST_LEN // PAGE_SIZE
MLA_HEADS = 8
Q_LORA = 384
KV_LORA = 256
NOPE_DIM = 64
ROPE_DIM = 32
ROPE_HALF = ROPE_DIM // 2
V_DIM = 64
ROPE_THETA = 10000.0
M_HEADS = 4
M_DIM = 128
CHUNK = 128
MLSTM_WIDTH = M_HEADS * M_DIM
D_FF = 2816
PLE_DIM = 256
EPS = 1e-6

LANES = 128
SUBLANES = 8

HEAD_PAD = LANES
MLA_PAD_WIDTH = MLA_HEADS * HEAD_PAD
MLA_WIDTH = MLA_HEADS * V_DIM
V_ROWS = 80
GATE_I_LANE = ROPE_DIM
GATE_F_LANE = ROPE_DIM + M_HEADS
QK_SCALE = (NOPE_DIM + ROPE_DIM) ** -0.5 * math.log2(math.e)

VMEM_LIMIT = 56 * 1024 * 1024


def _dot(a, b):
    return jnp.dot(a, b, preferred_element_type=F32)


def _dot_nt(a, b):
    return lax.dot_general(a, b, (((1,), (1,)), ((), ())), preferred_element_type=F32)


def _rms(x, g):
    ms = jnp.sum(x * x, axis=-1, keepdims=True) * (1.0 / x.shape[-1])
    return x * lax.rsqrt(ms + EPS) * g


def _resident(shape):
    return pl.BlockSpec(shape, lambda *_: (0,) * len(shape), pipeline_mode=pl.Buffered(1))


def _params(n_axes):
    return pltpu.CompilerParams(dimension_semantics=("arbitrary",) * n_axes,
                                vmem_limit_bytes=VMEM_LIMIT)


MXU_TILE = 256
FF_CHUNK_BOUNDS = (0, 6 * MXU_TILE, D_FF)
assert D_FF % MXU_TILE == 0


def _swiglu_half(xn, wg_ref, wu_ref, wd_ref):
    out = None
    for lo, hi in zip(FF_CHUNK_BOUNDS[:-1], FF_CHUNK_BOUNDS[1:]):
        cols = slice(lo, hi)
        gate = _dot(xn, wg_ref[:, cols])
        up = _dot(xn, wu_ref[:, cols])
        act = (jax.nn.silu(gate) * up).astype(BF16)
        part = _dot(act, wd_ref[cols, :])
        out = part if out is None else out + part
    return 0.5 * out


def _ffn_body(x_ref, g_ref, wg_ref, wu_ref, wd_ref, o_ref):
    x = x_ref[...]
    xn = _rms(x, g_ref[...]).astype(BF16)
    o_ref[...] = x + _swiglu_half(xn, wg_ref, wu_ref, wd_ref)


def _ffn_call(x, g, wg, wu, wd, tm):
    t = x.shape[0]
    row = lambda i: (i, 0)
    return pl.pallas_call(
        _ffn_body,
        grid=(t // tm,),
        in_specs=[pl.BlockSpec((tm, D_MODEL), row), _resident((1, D_MODEL)),
                  _resident((D_MODEL, D_FF)), _resident((D_MODEL, D_FF)), _resident((D_FF, D_MODEL))],
        out_specs=pl.BlockSpec((tm, D_MODEL), row),
        out_shape=jax.ShapeDtypeStruct((t, D_MODEL), F32),
        compiler_params=_params(1),
        name="ffn1",
    )(x, g, wg, wu, wd)


def _rope128(x, cos_tab, sin_tab, x2_start):
    lane = lax.broadcasted_iota(jnp.int32, x.shape, 1)
    partner = jnp.where(lane < x2_start, pltpu.roll(x, LANES - ROPE_HALF, 1),
                        pltpu.roll(x, ROPE_HALF, 1))
    return x * cos_tab + partner * sin_tab


CELL_V_ROWS = M_DIM + 16


def _proj_body(cell_feature_major, h_ref, gmix_ref, wqs_ref, wkv_ref, gq_ref, wuq_ref,
               gkv_ref, wuk_ref, wuvt_ref, bias_ref, tab_ref, *refs):
    wmk_ref, wqvot_ref = refs[:2]
    if cell_feature_major:
        q_out, k_out, vt_out, ckv_out, kr_out, krt_out, mk_out, mqt_out, mvt_out, mot_out, gt_out = refs[2:]
    else:
        q_out, k_out, vt_out, ckv_out, kr_out, krt_out, mq_out, mk_out, mv_out, mo_out, gate_out = refs[2:]
    u = _rms(h_ref[...], gmix_ref[...]).astype(BF16)
    ckv = _rms(_dot_nt(u, wkv_ref[...]), gkv_ref[...])
    ckv_out[...] = ckv
    ckv_b = ckv.astype(BF16)
    zqs = _dot_nt(u, wqs_ref[...])
    zs = zqs[:, Q_LORA:Q_LORA + LANES]
    n_tok = zs.shape[0]
    cs = jnp.concatenate([tab_ref[...], jnp.zeros((LANES - ROPE_DIM, n_tok), F32)], axis=0).T
    lane = lax.broadcasted_iota(jnp.int32, zs.shape, 1)
    shifted = lambda by: pltpu.roll(cs, by, 1)
    cos_k = jnp.where(lane < ROPE_HALF, cs, jnp.where(lane < ROPE_DIM, shifted(ROPE_HALF), 0.0))
    sin_k = jnp.where(lane < ROPE_HALF, -shifted(LANES - ROPE_HALF), jnp.where(lane < ROPE_DIM, cs, 0.0))
    kr = _rope128(zs, cos_k, sin_k, ROPE_HALF)
    kr_out[...] = kr[:, :ROPE_DIM]
    krt_out[0] = kr.T[:ROPE_DIM, :]
    kn = _dot(ckv_b, wuk_ref[...])
    kr_head = pltpu.roll(kr, NOPE_DIM, 1)
    for hd in range(MLA_HEADS):
        lanes = slice(hd * HEAD_PAD, (hd + 1) * HEAD_PAD)
        k_out[:, lanes] = (kn[:, lanes] + kr_head).astype(BF16)
    x1 = (lane >= NOPE_DIM) & (lane < NOPE_DIM + ROPE_HALF)
    x2 = (lane >= NOPE_DIM + ROPE_HALF) & (lane < NOPE_DIM + ROPE_DIM)
    cos_q = jnp.where(lane < NOPE_DIM, 1.0, jnp.where(x1, shifted(NOPE_DIM), jnp.where(
        x2, shifted(NOPE_DIM + ROPE_HALF), 0.0)))
    sin_q = jnp.where(x1, -shifted(NOPE_DIM - ROPE_HALF), jnp.where(x2, shifted(NOPE_DIM), 0.0))
    vt = _dot_nt(wuvt_ref[...], ckv_b)
    vrow = lax.broadcasted_iota(jnp.int32, vt.shape, 0)
    vt_out[0, 0] = jnp.where(vrow % V_ROWS == V_DIM, 1.0, vt).astype(BF16)
    qn = _rms(zqs[:, :Q_LORA], gq_ref[...]).astype(BF16)
    q = _dot(qn, wuq_ref[...])
    cos_qs = cos_q * QK_SCALE
    sin_qs = sin_q * QK_SCALE
    for hd in range(MLA_HEADS):
        lanes = slice(hd * HEAD_PAD, (hd + 1) * HEAD_PAD)
        q_out[:, lanes] = _rope128(q[:, lanes], cos_qs, sin_qs, NOPE_DIM + ROPE_HALF).astype(BF16)
    zb = zs + bias_ref[...]
    log_sig = jnp.minimum(zb, 0.0) - jnp.log1p(jnp.exp(-jnp.abs(zb)))
    is_f = (lane >= GATE_F_LANE) & (lane < GATE_F_LANE + M_HEADS)
    gates = jnp.where(is_f, log_sig, zb)
    mk_out[...] = (_dot_nt(u, wmk_ref[...]) * (M_DIM ** -0.5)).astype(BF16)
    if cell_feature_major:
        zt = _dot_nt(wqvot_ref[...], u)
        mqt_out[0, 0] = zt[0:MLSTM_WIDTH].astype(BF16)
        unit_rows = (lax.broadcasted_iota(jnp.int32, (CELL_V_ROWS - M_DIM, n_tok), 0) == 0)
        for hd in range(M_HEADS):
            rows = slice(MLSTM_WIDTH + hd * M_DIM, MLSTM_WIDTH + (hd + 1) * M_DIM)
            mvt_out[0, 0, hd * CELL_V_ROWS:hd * CELL_V_ROWS + M_DIM, :] = zt[rows].astype(BF16)
            mvt_out[0, 0, hd * CELL_V_ROWS + M_DIM:(hd + 1) * CELL_V_ROWS, :] = (
                unit_rows.astype(F32).astype(BF16))
        mot_out[0, 0] = zt[2 * MLSTM_WIDTH:3 * MLSTM_WIDTH]
        gt_out[0, 0] = gates.T[GATE_I_LANE:GATE_I_LANE + 2 * M_HEADS, :]
    else:
        zm = _dot_nt(u, wqvot_ref[...])
        mq_out[...] = zm[:, 0:MLSTM_WIDTH].astype(BF16)
        mv_out[...] = zm[:, MLSTM_WIDTH:2 * MLSTM_WIDTH].astype(BF16)
        mo_out[...] = zm[:, 2 * MLSTM_WIDTH:3 * MLSTM_WIDTH]
        gate_out[...] = gates


def _proj_call(h, w, rope_tab, tm, cell_feature_major):
    t = h.shape[0]
    n_tab = rope_tab.shape[1] // tm
    n_seq = t // tm // n_tab
    row = lambda i: (i, 0)
    tab = lambda i: (0, i % n_tab)
    tok = lambda width, dtype: (jax.ShapeDtypeStruct((t, width), dtype),
                                pl.BlockSpec((tm, width), row))
    slab = lambda rows, dtype: (jax.ShapeDtypeStruct((n_seq, n_tab, rows, tm), dtype),
                                pl.BlockSpec((1, 1, rows, tm), lambda i: (i // n_tab, i % n_tab, 0, 0)))
    outs = [tok(MLA_PAD_WIDTH, BF16), tok(MLA_PAD_WIDTH, BF16), slab(MLA_HEADS * V_ROWS, BF16),
            tok(KV_LORA, F32), tok(ROPE_DIM, F32),
            (jax.ShapeDtypeStruct((n_seq, ROPE_DIM, n_tab * tm), F32),
             pl.BlockSpec((1, ROPE_DIM, tm), lambda i: (i // n_tab, 0, i % n_tab)))]
    if cell_feature_major:
        outs += [tok(MLSTM_WIDTH, BF16), slab(MLSTM_WIDTH, BF16), slab(M_HEADS * CELL_V_ROWS, BF16),
                 slab(MLSTM_WIDTH, F32), slab(2 * M_HEADS, F32)]
    else:
        outs += [tok(MLSTM_WIDTH, BF16)] * 3 + [tok(MLSTM_WIDTH, F32), tok(LANES, F32)]
    return pl.pallas_call(
        functools.partial(_proj_body, cell_feature_major),
        grid=(t // tm,),
        in_specs=[pl.BlockSpec((tm, D_MODEL), row), _resident((1, D_MODEL)),
                  _resident((Q_LORA + LANES, D_MODEL)), _resident((KV_LORA, D_MODEL)),
                  _resident((1, Q_LORA)), _resident((Q_LORA, MLA_PAD_WIDTH)),
                  _resident((1, KV_LORA)), _resident((KV_LORA, MLA_PAD_WIDTH)),
                  _resident((MLA_HEADS * V_ROWS, KV_LORA)),
                  _resident((1, LANES)),
                  pl.BlockSpec((ROPE_DIM, tm), tab),
                  _resident((MLSTM_WIDTH, D_MODEL)), _resident((3 * MLSTM_WIDTH, D_MODEL))],
        out_specs=[o[1] for o in outs],
        out_shape=[o[0] for o in outs],
        compiler_params=_params(1),
        name="proj",
    )(h, w["g_mix"], w["w_qs_t"], w["w_kv_t"], w["g_q"], w["w_uq"], w["g_kv"],
      w["w_uk"], w["w_uv_t"], w["gate_bias"], rope_tab, w["w_mk_t"], w["w_qvo_t"])


ATT_BLOCK = 512
ATT_HEADS = 8
ATT_LANES = ATT_HEADS * HEAD_PAD
ATT_V_ROWS = ATT_HEADS * V_ROWS
ATT_LOOKAHEAD = 3


def _flash_body(q_ref, k_ref, vt_ref, o_ref):
    qi = pl.program_id(2)
    blk = ATT_BLOCK
    head_lanes = [slice(hd * HEAD_PAD, (hd + 1) * HEAD_PAD) for hd in range(ATT_HEADS)]
    qs = [q_ref[0, :, lanes] for lanes in head_lanes]

    def step(j, carry, diagonal):
        start = pl.multiple_of(j * blk, blk)
        scores = [None] * ATT_HEADS
        if diagonal:
            key = lax.broadcasted_iota(jnp.int32, (blk, blk), 0)
            qry = lax.broadcasted_iota(jnp.int32, (blk, blk), 1)
            visible = key <= qry
        out = []
        for hd in range(ATT_HEADS):
            m, acc = carry[hd]
            for nxt in range(hd if hd else 0, min(hd + ATT_LOOKAHEAD, ATT_HEADS - 1) + 1):
                if scores[nxt] is None:
                    scores[nxt] = _dot_nt(k_ref[0, pl.ds(start, blk), head_lanes[nxt]], qs[nxt])
            s = jnp.where(visible, scores[hd], -jnp.inf) if diagonal else scores[hd]
            m_new = jnp.maximum(m, jnp.max(s, axis=0, keepdims=True))
            p = jnp.exp2(s - m_new).astype(BF16)
            vt = vt_ref[0, j, hd * V_ROWS:(hd + 1) * V_ROWS, :]
            acc = jnp.exp2(m - m_new) * acc + _dot(vt, p)
            out.append((m_new, acc))
        return tuple(out)

    init = tuple((jnp.full((1, blk), -jnp.inf, F32), jnp.zeros((V_ROWS, blk), F32))
                 for _ in range(ATT_HEADS))
    carry = lax.fori_loop(0, qi, lambda j, c: step(j, c, False), init)
    final = step(qi, carry, True)
    for pair in range(ATT_HEADS // 2):
        o_t = jnp.concatenate([final[hd][1][:V_DIM] / final[hd][1][V_DIM:V_DIM + 1]
                               for hd in (2 * pair, 2 * pair + 1)], axis=0)
        o_ref[0, :, pair * LANES:(pair + 1) * LANES] = o_t.T.astype(BF16)


def _flash_call(q, k, vt):
    b, s, _ = q.shape
    assert vt.shape == (b, s // ATT_BLOCK, MLA_HEADS * V_ROWS, ATT_BLOCK)
    qmap = lambda bi, gi, qi: (bi, qi, gi)
    return pl.pallas_call(
        _flash_body,
        grid=(b, MLA_HEADS // ATT_HEADS, s // ATT_BLOCK),
        in_specs=[pl.BlockSpec((1, ATT_BLOCK, ATT_LANES), qmap),
                  pl.BlockSpec((1, s, ATT_LANES), lambda bi, gi, qi: (bi, 0, gi),
                               pipeline_mode=pl.Buffered(1)),
                  pl.BlockSpec((1, s // ATT_BLOCK, ATT_V_ROWS, ATT_BLOCK),
                               lambda bi, gi, qi: (bi, 0, gi, 0), pipeline_mode=pl.Buffered(1))],
        out_specs=pl.BlockSpec((1, ATT_BLOCK, ATT_HEADS * V_DIM), qmap),
        out_shape=jax.ShapeDtypeStruct((b, s, MLA_WIDTH), BF16),
        compiler_params=_params(3),
        name="prompt_attn",
    )(q, k, vt)


def _mlstm_body(k_ref, qt_ref, vt_ref, ot_ref, gt_ref, gm_ref, hm_out, cxt_out, m_out, cxt_s, m_s):
    ci = pl.program_id(0)
    n_seq = k_ref.shape[0]
    L = CHUNK

    @pl.when(ci == 0)
    def _():
        cxt_s[...] = jnp.zeros_like(cxt_s)
        m_s[...] = jnp.zeros_like(m_s)

    s_idx = lax.broadcasted_iota(jnp.int32, (L, L), 0)
    t_idx = lax.broadcasted_iota(jnp.int32, (L, L), 1)
    causal = s_idx <= t_idx

    gate_rows = [gt_ref[sq, 0] for sq in range(n_seq)]
    prefix = [jnp.dot(rows, causal.astype(F32), precision=lax.Precision.HIGHEST,
                      preferred_element_type=F32) for rows in gate_rows]

    streams = []
    for sq in range(n_seq):
        for hd in range(M_HEADS):
            feat = slice(hd * M_DIM, (hd + 1) * M_DIM)
            idx = sq * M_HEADS + hd
            k = k_ref[sq, :, feat]
            qt = qt_ref[sq, 0, feat, :]
            cxt_prev = cxt_s[idx]
            streams.append(dict(
                sq=sq, hd=hd, feat=feat, idx=idx, k=k, cxt_prev=cxt_prev,
                vt=vt_ref[sq, 0, hd * CELL_V_ROWS:(hd + 1) * CELL_V_ROWS, :],
                m_prev=m_s[idx][0:1, :],
                qk=_dot(k, qt),
                qe=_dot(cxt_prev.astype(BF16), qt)))

    gates = []
    for sq in range(n_seq):
        rows = gate_rows[sq]
        b_rows = pltpu.roll(prefix[sq], M_HEADS, 0)
        head_row = lax.broadcasted_iota(jnp.int32, rows.shape, 0) < M_HEADS
        a_rows = jnp.where(head_row, rows - b_rows, 0.0)
        top = jnp.broadcast_to(jnp.max(a_rows, axis=1, keepdims=True), rows.shape)
        b_last = pltpu.roll(jnp.broadcast_to(jnp.sum(rows, axis=1, keepdims=True), rows.shape),
                            M_HEADS, 0)
        a_cols = jnp.concatenate([a_rows, jnp.zeros((L - SUBLANES, L), F32)], axis=0).T
        gates.append(dict(b_rows=b_rows, a_cols=a_cols, top=top, b_last=b_last,
                          src=jnp.exp(a_rows - top)))

    for st in streams:
        g, hd = gates[st["sq"]], st["hd"]
        weighted_vt = (st["vt"].astype(F32) * g["src"][hd:hd + 1, :]).astype(BF16)
        st["kv"] = _dot(weighted_vt, st["k"])

    for st in streams:
        g, hd = gates[st["sq"]], st["hd"]
        a_col = g["a_cols"][:, hd:hd + 1]
        run_max = jnp.max(jnp.where(causal, a_col, -jnp.inf), axis=0, keepdims=True)
        big_m = jnp.maximum(st["m_prev"], run_max)
        st["sqk"] = st["qk"] * jnp.where(causal, jnp.exp(a_col - big_m), 0.0)
        st["w_inter"] = jnp.exp(st["m_prev"] - big_m)
        st["floor"] = jnp.exp(-(g["b_rows"][hd:hd + 1, :] + big_m))
        st["top"] = g["top"][hd:hd + 1, :]
        st["b_last"] = g["b_last"][hd:hd + 1, :]

    for st in streams:
        ue = _dot(st["vt"], st["sqk"].astype(BF16))
        num = ue[:M_DIM] + st["qe"][:M_DIM] * st["w_inter"]
        den = ue[M_DIM:M_DIM + 1] + st["qe"][M_DIM:M_DIM + 1] * st["w_inter"]
        inv = 1.0 / jnp.maximum(jnp.abs(den), st["floor"])
        t = jax.nn.sigmoid(ot_ref[st["sq"], 0, st["feat"], :]) * num
        ms = jnp.sum(t * t, axis=0, keepdims=True) * (1.0 / M_DIM)
        scale = inv * lax.rsqrt(ms * inv * inv + EPS)
        hm_out[st["sq"], :, st["feat"]] = (t * scale * gm_ref[st["feat"], :]).T.astype(BF16)
        m_last = jnp.maximum(st["m_prev"], st["top"])
        keep = jnp.exp(st["m_prev"] - m_last)
        gain = jnp.exp(st["top"] - m_last)
        cxt_s[st["idx"]] = keep * st["cxt_prev"] + gain * st["kv"]
        m_s[st["idx"]] = jnp.broadcast_to(st["b_last"] + m_last, (SUBLANES, LANES))

    @pl.when(ci == pl.num_programs(0) - 1)
    def _():
        cxt_out[...] = cxt_s[...]
        m_out[...] = m_s[...]


def _mlstm_call(mk, mqt, mvt, mot, gt, gm_lanes):
    b, s, _ = mk.shape
    tile = mqt.shape[-1]
    per_tile = tile // CHUNK
    n_streams = b * M_HEADS
    slab = lambda rows: pl.BlockSpec((b, 1, rows, CHUNK),
                                     lambda ci: (0, ci // per_tile, 0, ci % per_tile))
    tok = pl.BlockSpec((b, CHUNK, MLSTM_WIDTH), lambda ci: (0, ci, 0))
    whole3 = lambda ci: (0, 0, 0)
    return pl.pallas_call(
        _mlstm_body,
        grid=(s // CHUNK,),
        in_specs=[tok, slab(MLSTM_WIDTH), slab(M_HEADS * CELL_V_ROWS), slab(MLSTM_WIDTH),
                  slab(2 * M_HEADS), _resident((MLSTM_WIDTH, LANES))],
        out_specs=[tok,
                   pl.BlockSpec((n_streams, CELL_V_ROWS, M_DIM), whole3),
                   pl.BlockSpec((n_streams, SUBLANES, LANES), whole3)],
        out_shape=[jax.ShapeDtypeStruct((b, s, MLSTM_WIDTH), BF16),
                   jax.ShapeDtypeStruct((n_streams, CELL_V_ROWS, M_DIM), F32),
                   jax.ShapeDtypeStruct((n_streams, SUBLANES, LANES), F32)],
        scratch_shapes=[pltpu.VMEM((n_streams, CELL_V_ROWS, M_DIM), F32),
                        pltpu.VMEM((n_streams, SUBLANES, LANES), F32)],
        compiler_params=_params(1),
        name="prompt_mlstm",
    )(mk, mqt, mvt, mot, gt, gm_lanes)


def _merge_body(h_ref, a_ref, hm_ref, p_ref, ga_ref, woa_ref, wom_ref, gff_ref, wg_ref, wu_ref,
                wd_ref, gple_ref, wpg_ref, wpp_ref, gfin_ref, y_ref):
    a = a_ref[...].astype(F32)
    an = _rms(a, ga_ref[...]).astype(BF16)
    h = h_ref[...] + _dot(an, woa_ref[...]) + _dot(hm_ref[...], wom_ref[...])
    h = h + _swiglu_half(_rms(h, gff_ref[...]).astype(BF16), wg_ref, wu_ref, wd_ref)
    gate = jax.nn.sigmoid(_dot(_rms(h, gple_ref[...]).astype(BF16), wpg_ref[...]))
    h = h + gate * _dot(p_ref[...].astype(BF16), wpp_ref[...])
    y_ref[...] = _rms(h, gfin_ref[...])


def _merge_call(h, a, hm, p, w, tm):
    t = h.shape[0]
    row = lambda i: (i, 0)
    return pl.pallas_call(
        _merge_body,
        grid=(t // tm,),
        in_specs=[pl.BlockSpec((tm, D_MODEL), row), pl.BlockSpec((tm, MLA_WIDTH), row),
                  pl.BlockSpec((tm, MLSTM_WIDTH), row), pl.BlockSpec((tm, PLE_DIM), row),
                  _resident((1, MLA_WIDTH)), _resident((MLA_WIDTH, D_MODEL)),
                  _resident((MLSTM_WIDTH, D_MODEL)), _resident((1, D_MODEL)),
                  _resident((D_MODEL, D_FF)), _resident((D_MODEL, D_FF)), _resident((D_FF, D_MODEL)),
                  _resident((1, D_MODEL)), _resident((D_MODEL, D_MODEL)),
                  _resident((PLE_DIM, D_MODEL)), _resident((1, D_MODEL))],
        out_specs=pl.BlockSpec((tm, D_MODEL), row),
        out_shape=jax.ShapeDtypeStruct((t, D_MODEL), F32),
        compiler_params=_params(1),
        name="merge",
    )(h, a, hm, p, w["g_attn"], w["w_out_a"], w["w_out_m"], w["g_ff2"], w["w_ff2_gate"],
      w["w_ff2_up"], w["w_ff2_down"], w["g_ple"], w["w_ple_gate"], w["w_ple_proj"], w["g_final"])


N_SLOTS = 2
SAMPLE_CHUNKS = 4
SAMPLE_CHUNK_PAGES = N_PAGES // SAMPLE_CHUNKS
SAMPLE_CHUNK_KEYS = SAMPLE_CHUNK_PAGES * PAGE_SIZE


def _absorb_body(q_ref, wabs_ref, qx_ref):
    for hd in range(MLA_HEADS):
        lanes = slice(hd * HEAD_PAD, (hd + 1) * HEAD_PAD)
        qx_ref[hd, :, 0, :] = _dot(q_ref[:, lanes], wabs_ref[hd])


def _absorb_call(q, w_abs):
    nb = q.shape[0]
    return pl.pallas_call(
        _absorb_body,
        out_shape=jax.ShapeDtypeStruct((MLA_HEADS, nb, 1, KV_LORA + LANES), F32),
        name="sample_absorb",
    )(q, w_abs)


def _sample_attn_body(pt_ref, qx_ref, ckv_ref, kr_ref, cache_c, cache_r, wuv_ref,
                      o_ref, cbuf, rbuf, sem_c, sem_r):
    b = pl.program_id(0)
    nb = pl.num_programs(0)
    slot = b % N_SLOTS

    def copies(bi, sl):
        out = []
        for j in range(N_PAGES):
            page = pt_ref[bi, j]
            out.append(pltpu.make_async_copy(cache_c.at[page], cbuf.at[sl, j], sem_c.at[sl]))
            out.append(pltpu.make_async_copy(
                cache_r.at[page], rbuf.at[sl, :, pl.ds(j * PAGE_SIZE, PAGE_SIZE)], sem_r.at[sl]))
        return out

    @pl.when(b == 0)
    def _():
        for bi in range(min(N_SLOTS, DEC_BATCH)):
            for cp in copies(bi, bi):
                cp.start()

    q_ext = qx_ref[:, 0, 0, :]
    q_abs = q_ext[:, :KV_LORA].astype(BF16)
    q_rope = q_ext[:, KV_LORA:KV_LORA + ROPE_DIM].astype(BF16)

    c_new = ckv_ref[0].astype(BF16).astype(F32)
    r_new = kr_ref[0].astype(BF16).astype(F32)
    s_new = (jnp.sum(q_abs.astype(F32) * c_new, axis=1, keepdims=True)
             + jnp.sum(q_rope.astype(F32) * r_new, axis=1, keepdims=True))

    for cp in copies(b, slot):
        cp.wait()

    def chunk_keys(i):
        pages = slice(i * SAMPLE_CHUNK_PAGES, (i + 1) * SAMPLE_CHUNK_PAGES)
        keys = slice(i * SAMPLE_CHUNK_KEYS, (i + 1) * SAMPLE_CHUNK_KEYS)
        kc = cbuf[slot, pages].reshape(SAMPLE_CHUNK_KEYS, KV_LORA).astype(BF16)
        return kc, _dot_nt(q_abs, kc) + _dot(q_rope, rbuf[slot, :, keys].astype(BF16))

    m = s_new
    l = jnp.ones_like(s_new)
    acc = jnp.broadcast_to(c_new, (MLA_HEADS, KV_LORA))
    kc, s = chunk_keys(0)
    for i in range(SAMPLE_CHUNKS):
        nxt = chunk_keys(i + 1) if i + 1 < SAMPLE_CHUNKS else None
        m_new = jnp.maximum(m, jnp.max(s, axis=1, keepdims=True))
        alpha = jnp.exp2(m - m_new)
        p = jnp.exp2(s - m_new)
        l = alpha * l + jnp.sum(p, axis=1, keepdims=True)
        acc = alpha * acc + _dot(p.astype(BF16), kc)
        m = m_new
        if nxt is not None:
            kc, s = nxt
    o_lat = (acc / l).astype(BF16)
    res = _dot(o_lat, wuv_ref[...])
    own_v = (lax.broadcasted_iota(jnp.int32, res.shape, 1) // V_DIM
             == lax.broadcasted_iota(jnp.int32, res.shape, 0))
    o_ref[0] = jnp.sum(jnp.where(own_v, res, 0.0), axis=0, keepdims=True).astype(BF16)

    @pl.when(b + N_SLOTS < nb)
    def _():
        for cp in copies(b + N_SLOTS, slot):
            cp.start()


def _sample_attn_call(page_table, q_ext, ckv, kr, cache_c, cache_r, w_uv):
    nb = q_ext.shape[1]
    tok = lambda bi, pt: (bi, 0, 0)
    whole = lambda shape: pl.BlockSpec(shape, lambda bi, pt: (0,) * len(shape),
                                       pipeline_mode=pl.Buffered(1))
    grid_spec = pltpu.PrefetchScalarGridSpec(
        num_scalar_prefetch=1,
        grid=(nb,),
        in_specs=[pl.BlockSpec((MLA_HEADS, 1, 1, KV_LORA + LANES), lambda bi, pt: (0, bi, 0, 0)),
                  pl.BlockSpec((1, 1, KV_LORA), tok),
                  pl.BlockSpec((1, 1, ROPE_DIM), tok),
                  pl.BlockSpec(memory_space=pl.ANY), pl.BlockSpec(memory_space=pl.ANY),
                  whole((KV_LORA, MLA_WIDTH))],
        out_specs=pl.BlockSpec((1, 1, MLA_WIDTH), tok),
        scratch_shapes=[pltpu.VMEM((N_SLOTS, N_PAGES, PAGE_SIZE, KV_LORA), F32),
                        pltpu.VMEM((N_SLOTS, ROPE_DIM, PAST_LEN), F32),
                        pltpu.SemaphoreType.DMA((N_SLOTS,)), pltpu.SemaphoreType.DMA((N_SLOTS,))])
    return pl.pallas_call(
        _sample_attn_body,
        grid_spec=grid_spec,
        out_shape=jax.ShapeDtypeStruct((nb, 1, MLA_WIDTH), BF16),
        compiler_params=_params(1),
        name="sample_attn",
    )(page_table, q_ext, ckv, kr, cache_c, cache_r, w_uv)


SAMPLE_ROWS = 32


def _sample_mlstm_body(q_ref, k_ref, v_ref, o_ref, g_ref, gm_ref, c_ref, n_ref, m_ref,
                       hm_out, c_out, n_out, m_out, qc_s):
    hd = pl.program_id(1)
    g = g_ref[...]
    lane = lax.broadcasted_iota(jnp.int32, g.shape, 1)
    ig = jnp.sum(jnp.where(lane == GATE_I_LANE + hd, g, 0.0), axis=1, keepdims=True)
    lf = jnp.sum(jnp.where(lane == GATE_F_LANE + hd, g, 0.0), axis=1, keepdims=True)
    m_all = m_ref[...]
    head_lane = lax.broadcasted_iota(jnp.int32, m_all.shape, 1)
    m0 = jnp.sum(jnp.where(head_lane == hd, m_all, 0.0), axis=1, keepdims=True)
    q = q_ref[...].astype(F32)
    k = k_ref[...].astype(F32)
    v = v_ref[...].astype(F32)
    n0 = n_ref[...]
    m_new = jnp.maximum(lf + m0, ig)
    keep = jnp.exp(lf + m0 - m_new)
    w_src = jnp.exp(ig - m_new)
    sqk = jnp.sum(q * k, axis=1, keepdims=True) * w_src
    wk = w_src * k
    rows = q.shape[0]
    pad = jnp.zeros((M_DIM - rows, M_DIM), F32)
    q_t = jnp.concatenate([q, pad], axis=0).T
    wk_t = jnp.concatenate([wk, pad], axis=0).T
    keep_lanes = jnp.broadcast_to(keep, (rows, M_DIM))
    for b in range(rows):
        c_b = c_ref[b, 0]
        qc_s[b:b + 1, :] = jnp.sum(q_t[:, b:b + 1] * c_b, axis=0, keepdims=True)
        c_out[b, 0] = keep_lanes[b:b + 1, :] * c_b + wk_t[:, b:b + 1] * v[b:b + 1, :]
    qc = qc_s[...]
    num = sqk * v + qc * keep
    den = sqk + jnp.sum(q * n0, axis=1, keepdims=True) * keep
    den = jnp.maximum(jnp.abs(den), jnp.exp(-m_new))
    hh = jax.nn.sigmoid(o_ref[...]) * (num / den)
    hm_out[...] = _rms(hh, gm_ref[...]).astype(BF16)
    n_out[...] = keep * n0 + wk

    @pl.when(hd == 0)
    def _():
        m_out[...] = jnp.broadcast_to(m_new, m_all.shape)

    @pl.when(hd > 0)
    def _():
        m_out[...] = jnp.where(head_lane == hd, m_new, m_out[...])


def _sample_mlstm_call(mq, mk, mv, mo, gates, gm, c0, n0, m0):
    nb = mq.shape[0]
    rows = SAMPLE_ROWS
    tok = lambda bi, hi: (bi, hi)
    return pl.pallas_call(
        _sample_mlstm_body,
        grid=(nb // rows, M_HEADS),
        in_specs=[pl.BlockSpec((rows, M_DIM), tok)] * 4
                 + [pl.BlockSpec((rows, LANES), lambda bi, hi: (bi, 0)),
                    pl.BlockSpec((1, M_DIM), lambda bi, hi: (0, hi)),
                    pl.BlockSpec((rows, 1, M_DIM, M_DIM), lambda bi, hi: (bi, hi, 0, 0)),
                    pl.BlockSpec((rows, M_DIM), tok),
                    pl.BlockSpec((rows, M_HEADS), lambda bi, hi: (bi, 0))],
        out_specs=[pl.BlockSpec((rows, M_DIM), tok),
                   pl.BlockSpec((rows, 1, M_DIM, M_DIM), lambda bi, hi: (bi, hi, 0, 0)),
                   pl.BlockSpec((rows, M_DIM), tok),
                   pl.BlockSpec((rows, M_HEADS), lambda bi, hi: (bi, 0))],
        out_shape=[jax.ShapeDtypeStruct((nb, MLSTM_WIDTH), BF16),
                   jax.ShapeDtypeStruct((nb, M_HEADS, M_DIM, M_DIM), F32),
                   jax.ShapeDtypeStruct((nb, MLSTM_WIDTH), F32),
                   jax.ShapeDtypeStruct((nb, M_HEADS), F32)],
        scratch_shapes=[pltpu.VMEM((rows, M_DIM), F32)],
        compiler_params=_params(2),
        name="sample_mlstm",
    )(mq, mk, mv, mo, gates, gm, c0, n0, m0)


def _pad_heads(w, head_dim):
    rows = w.shape[0]
    w = w.reshape(rows, MLA_HEADS, head_dim)
    w = jnp.pad(w, ((0, 0), (0, 0), (0, HEAD_PAD - head_dim)))
    return w.reshape(rows, MLA_PAD_WIDTH)


def _rope_tables(pos):
    inv = ROPE_THETA ** (-jnp.arange(ROPE_HALF, dtype=F32) / ROPE_HALF)
    ang = inv[:, None] * pos.astype(F32)[None, :]
    return jnp.concatenate([jnp.cos(ang), jnp.sin(ang)], axis=0)


def _prep_weights(g_ff1, w_ff1_gate, w_ff1_up, w_ff1_down, g_mix, w_in, g_q, w_uq, g_kv, w_uk,
                  w_uv, b_gate_i, b_gate_f, g_attn_out, g_mlstm_out, w_out, g_ff2, w_ff2_gate,
                  w_ff2_up, w_ff2_down, g_ple, w_ple_gate, w_ple_proj, g_final):
    bf = lambda a: a.astype(BF16)
    row = lambda a: a.reshape(1, -1).astype(F32)
    off_kv, off_kr = Q_LORA, Q_LORA + KV_LORA
    off_m = off_kr + ROPE_DIM
    off_i = off_m + 4 * MLSTM_WIDTH
    small_pad = LANES - ROPE_DIM - 2 * M_HEADS
    w_t = bf(w_in.T)
    w_qs_t = jnp.concatenate([w_t[:off_kv], w_t[off_kr:off_m], w_t[off_i:off_i + 2 * M_HEADS],
                              jnp.zeros((small_pad, D_MODEL), BF16)], axis=0)
    w_qvo_t = jnp.concatenate([w_t[off_m:off_m + MLSTM_WIDTH], w_t[off_m + 2 * MLSTM_WIDTH:off_i]],
                              axis=0)
    gate_bias = jnp.concatenate([jnp.zeros((ROPE_DIM,), F32), b_gate_i, b_gate_f,
                                 jnp.zeros((small_pad,), F32)])
    src = jnp.arange(LANES)[:, None]
    dst = jnp.arange(MLA_PAD_WIDTH)[None, :]
    w_abs = jnp.pad(w_uk.reshape(KV_LORA, MLA_HEADS, NOPE_DIM).transpose(1, 2, 0),
                    ((0, 0), (0, HEAD_PAD - NOPE_DIM), (0, 0))).reshape(MLA_PAD_WIDTH, KV_LORA)
    rope_sel = (dst.T % HEAD_PAD) == (src.T + NOPE_DIM)
    rope_sel = rope_sel & (src.T < ROPE_DIM)
    w_abs = jnp.concatenate([w_abs, rope_sel.astype(F32)], axis=1)
    return dict(
        g_ff1=row(g_ff1), w_ff1_gate=bf(w_ff1_gate), w_ff1_up=bf(w_ff1_up), w_ff1_down=bf(w_ff1_down),
        g_mix=row(g_mix), w_qs_t=w_qs_t, w_kv_t=w_t[off_kv:off_kr],
        w_mk_t=w_t[off_m + MLSTM_WIDTH:off_m + 2 * MLSTM_WIDTH], w_qvo_t=w_qvo_t,
        g_q=row(g_q), w_uq=bf(_pad_heads(w_uq, NOPE_DIM + ROPE_DIM)),
        g_kv=row(g_kv), w_uk=bf(_pad_heads(w_uk, NOPE_DIM)), w_uv=bf(w_uv),
        w_uv_t=bf(jnp.pad(w_uv.T.reshape(MLA_HEADS, V_DIM, KV_LORA),
                          ((0, 0), (0, V_ROWS - V_DIM), (0, 0))).reshape(MLA_HEADS * V_ROWS, KV_LORA)),
        gate_bias=row(gate_bias), w_abs=bf(w_abs).reshape(MLA_HEADS, HEAD_PAD, KV_LORA + LANES),
        g_attn=row(g_attn_out), w_out_a=bf(w_out[:MLA_WIDTH]),
        w_out_m=bf(w_out[MLA_WIDTH:]), g_mlstm=row(g_mlstm_out),
        g_mlstm_lanes=jnp.broadcast_to(g_mlstm_out.reshape(MLSTM_WIDTH, 1).astype(F32),
                                       (MLSTM_WIDTH, LANES)),
        g_ff2=row(g_ff2), w_ff2_gate=bf(w_ff2_gate), w_ff2_up=bf(w_ff2_up), w_ff2_down=bf(w_ff2_down),
        g_ple=row(g_ple), w_ple_gate=bf(w_ple_gate), w_ple_proj=bf(w_ple_proj), g_final=row(g_final))


PROMPT_TILE = 512


def kernel(x_prompt, x_sample, p_prompt, p_sample, cache_ckv, cache_krope, state_C, state_n, state_m, page_table, g_ff1, w_ff1_gate, w_ff1_up, w_ff1_down, g_mix, w_in, g_q, w_uq, g_kv, w_uk, w_uv, b_gate_i, b_gate_f, g_attn_out, g_mlstm_out, w_out, g_ff2, w_ff2_gate, w_ff2_up, w_ff2_down, g_ple, w_ple_gate, w_ple_proj, g_final):
    assert w_in.shape[0] == 1, "single-layer trunk"
    w = _prep_weights(g_ff1[0], w_ff1_gate[0], w_ff1_up[0], w_ff1_down[0], g_mix[0], w_in[0],
                      g_q[0], w_uq[0], g_kv[0], w_uk[0], w_uv[0], b_gate_i[0], b_gate_f[0],
                      g_attn_out[0], g_mlstm_out[0], w_out[0], g_ff2[0], w_ff2_gate[0],
                      w_ff2_up[0], w_ff2_down[0], g_ple[0], w_ple_gate[0], w_ple_proj[0], g_final)
    nb_p, seq, _ = x_prompt.shape
    nb_s = x_sample.shape[0]
    t_p = nb_p * seq

    tab_p = _rope_tables(jnp.arange(seq))
    h_p = _ffn_call(x_prompt.reshape(t_p, D_MODEL), w["g_ff1"], w["w_ff1_gate"], w["w_ff1_up"],
                    w["w_ff1_down"], PROMPT_TILE)
    q_p, k_p, vt_p, ckv_p, _, krt_p, mk_p, mqt_p, mvt_p, mot_p, gt_p = _proj_call(
        h_p, w, tab_p, PROMPT_TILE, cell_feature_major=True)
    seq3 = lambda a: a.reshape(nb_p, seq, a.shape[-1])
    a_p = _flash_call(seq3(q_p), seq3(k_p), vt_p)
    hm_p, cxt_p, m_p = _mlstm_call(seq3(mk_p), mqt_p, mvt_p, mot_p, gt_p, w["g_mlstm_lanes"])
    cxt_p = cxt_p.reshape(nb_p, M_HEADS, CELL_V_ROWS, M_DIM)
    y_p = _merge_call(h_p, a_p.reshape(t_p, MLA_WIDTH), hm_p.reshape(t_p, MLSTM_WIDTH),
                      p_prompt.reshape(t_p, PLE_DIM), w, PROMPT_TILE)

    tab_s = _rope_tables(jnp.full((nb_s,), PAST_LEN, jnp.int32))
    h_s = _ffn_call(x_sample.reshape(nb_s, D_MODEL), w["g_ff1"], w["w_ff1_gate"], w["w_ff1_up"],
                    w["w_ff1_down"], nb_s)
    q_s, _, _, ckv_s, kr_s, krt_s, mq_s, mk_s, mv_s, mo_s, gates_s = _proj_call(
        h_s, w, tab_s, nb_s, cell_feature_major=False)
    n_phys = cache_ckv.shape[1]
    a_s = _sample_attn_call(
        page_table, _absorb_call(q_s, w["w_abs"]), ckv_s.reshape(nb_s, 1, KV_LORA),
        kr_s.reshape(nb_s, 1, ROPE_DIM), cache_ckv.reshape(n_phys, PAGE_SIZE, KV_LORA),
        jnp.swapaxes(cache_krope.reshape(n_phys, PAGE_SIZE, ROPE_DIM), 1, 2), w["w_uv"])
    hm_s, c_s, n_s, m_s = _sample_mlstm_call(
        mq_s, mk_s, mv_s, mo_s, gates_s, w["g_mlstm"], state_C[0].astype(F32),
        state_n[0].astype(F32).reshape(nb_s, MLSTM_WIDTH), state_m[0].astype(F32))
    y_s = _merge_call(h_s, a_s.reshape(nb_s, MLA_WIDTH), hm_s, p_sample.reshape(nb_s, PLE_DIM),
                      w, nb_s)

    return (y_p.reshape(nb_p, seq, D_MODEL), y_s.reshape(nb_s, 1, D_MODEL),
            ckv_p.reshape(1, nb_p, seq, KV_LORA), jnp.swapaxes(krt_p, 1, 2)[None],
            jnp.swapaxes(cxt_p[:, :, :M_DIM, :], 2, 3)[None], cxt_p[None, :, :, M_DIM, :],
            m_p[:, 0, 0].reshape(1, nb_p, M_HEADS),
            ckv_s.reshape(1, nb_s, 1, KV_LORA), jnp.swapaxes(krt_s, 1, 2).reshape(1, nb_s, 1, ROPE_DIM),
            c_s[None], n_s.reshape(1, nb_s, M_HEADS, M_DIM), m_s[None])
```

```python
import functools
import math

import jax
import jax.numpy as jnp
from jax import lax
from jax.experimental import pallas as pl
from jax.experimental.pallas import tpu as pltpu

F32 = jnp.float32
BF16 = jnp.bfloat16

D_MODEL = 1024
SEQ = 8192
DEC_BATCH = 128
PAST_LEN = 8192
PAGE_SIZE = 128
N_PAGES = PAST_LEN // PAGE_SIZE
MLA_HEADS = 8
Q_LORA = 384
KV_LORA = 256
NOPE_DIM = 64
ROPE_DIM = 32
ROPE_HALF = ROPE_DIM // 2
V_DIM = 64
ROPE_THETA = 10000.0
M_HEADS = 4
M_DIM = 128
CHUNK = 128
MLSTM_WIDTH = M_HEADS * M_DIM
D_FF = 2816
PLE_DIM = 256
EPS = 1e-6

LANES = 128
SUBLANES = 8

HEAD_PAD = LANES
MLA_PAD_WIDTH = MLA_HEADS * HEAD_PAD
MLA_WIDTH = MLA_HEADS * V_DIM
V_ROWS = 80
GATE_I_LANE = ROPE_DIM
GATE_F_LANE = ROPE_DIM + M_HEADS
QK_SCALE = (NOPE_DIM + ROPE_DIM) ** -0.5 * math.log2(math.e)

VMEM_LIMIT = 56 * 1024 * 1024


def _dot(a, b):
    return jnp.dot(a, b, preferred_element_type=F32)


def _dot_nt(a, b):
    return lax.dot_general(a, b, (((1,), (1,)), ((), ())), preferred_element_type=F32)


def _rms(x, g):
    ms = jnp.sum(x * x, axis=-1, keepdims=True) * (1.0 / x.shape[-1])
    return x * lax.rsqrt(ms + EPS) * g


def _resident(shape):
    return pl.BlockSpec(shape, lambda *_: (0,) * len(shape), pipeline_mode=pl.Buffered(1))


def _params(n_axes):
    return pltpu.CompilerParams(dimension_semantics=("arbitrary",) * n_axes,
                                vmem_limit_bytes=VMEM_LIMIT)


MXU_TILE = 256
FF_CHUNK_BOUNDS = (0, 6 * MXU_TILE, D_FF)
assert D_FF % MXU_TILE == 0


def _swiglu_half(xn, wg_ref, wu_ref, wd_ref):
    out = None
    for lo, hi in zip(FF_CHUNK_BOUNDS[:-1], FF_CHUNK_BOUNDS[1:]):
        cols = slice(lo, hi)
        gate = _dot(xn, wg_ref[:, cols])
        up = _dot(xn, wu_ref[:, cols])
        act = (jax.nn.silu(gate) * up).astype(BF16)
        part = _dot(act, wd_ref[cols, :])
        out = part if out is None else out + part
    return 0.5 * out


def _ffn_body(x_ref, xs_ref, g_ref, wg_ref, wu_ref, wd_ref, o_ref, os_ref):
    def half_step(x):
        return x + _swiglu_half(_rms(x, g_ref[...]).astype(BF16), wg_ref, wu_ref, wd_ref)

    o_ref[...] = half_step(x_ref[...])

    @pl.when(pl.program_id(0) == pl.num_programs(0) - 1)
    def _():
        os_ref[...] = half_step(xs_ref[...])


def _ffn_call(x, x_small, g, wg, wu, wd, tm):
    t = x.shape[0]
    row = lambda i: (i, 0)
    small = pl.BlockSpec(x_small.shape, lambda i: (0, 0))
    return pl.pallas_call(
        _ffn_body,
        grid=(t // tm,),
        in_specs=[pl.BlockSpec((tm, D_MODEL), row), small, _resident((1, D_MODEL)),
                  _resident((D_MODEL, D_FF)), _resident((D_MODEL, D_FF)), _resident((D_FF, D_MODEL))],
        out_specs=[pl.BlockSpec((tm, D_MODEL), row), small],
        out_shape=[jax.ShapeDtypeStruct((t, D_MODEL), F32),
                   jax.ShapeDtypeStruct(x_small.shape, F32)],
        compiler_params=_params(1),
        name="ffn1",
    )(x, x_small, g, wg, wu, wd)


def _rope128(x, cos_tab, sin_tab, x2_start):
    lane = lax.broadcasted_iota(jnp.int32, x.shape, 1)
    partner = jnp.where(lane < x2_start, pltpu.roll(x, LANES - ROPE_HALF, 1),
                        pltpu.roll(x, ROPE_HALF, 1))
    return x * cos_tab + partner * sin_tab


CELL_V_ROWS = M_DIM + 16


def _proj_body(cell_feature_major, h_ref, gmix_ref, wqs_ref, wkv_ref, gq_ref, wuq_ref,
               gkv_ref, wuk_ref, wuvt_ref, bias_ref, tab_ref, *refs):
    wmk_ref, wqvot_ref = refs[:2]
    if cell_feature_major:
        q_out, k_out, vt_out, ckv_out, kr_out, krt_out, mk_out, mqt_out, mvt_out, mot_out, gt_out = refs[2:]
    else:
        q_out, k_out, vt_out, ckv_out, kr_out, krt_out, mq_out, mk_out, mv_out, mo_out, gate_out = refs[2:]
    u = _rms(h_ref[...], gmix_ref[...]).astype(BF16)
    ckv = _rms(_dot_nt(u, wkv_ref[...]), gkv_ref[...])
    ckv_out[...] = ckv
    ckv_b = ckv.astype(BF16)
    zqs = _dot_nt(u, wqs_ref[...])
    zs = zqs[:, Q_LORA:Q_LORA + LANES]
    n_tok = zs.shape[0]
    cs = jnp.concatenate([tab_ref[...], jnp.zeros((LANES - ROPE_DIM, n_tok), F32)], axis=0).T
    lane = lax.broadcasted_iota(jnp.int32, zs.shape, 1)
    shifted = lambda by: pltpu.roll(cs, by, 1)
    cos_k = jnp.where(lane < ROPE_HALF, cs, jnp.where(lane < ROPE_DIM, shifted(ROPE_HALF), 0.0))
    sin_k = jnp.where(lane < ROPE_HALF, -shifted(LANES - ROPE_HALF), jnp.where(lane < ROPE_DIM, cs, 0.0))
    kr = _rope128(zs, cos_k, sin_k, ROPE_HALF)
    kr_out[...] = kr[:, :ROPE_DIM]
    krt_out[0] = kr.T[:ROPE_DIM, :]
    kn = _dot(ckv_b, wuk_ref[...])
    kr_head = pltpu.roll(kr, NOPE_DIM, 1)
    for hd in range(MLA_HEADS):
        lanes = slice(hd * HEAD_PAD, (hd + 1) * HEAD_PAD)
        k_out[:, lanes] = (kn[:, lanes] + kr_head).astype(BF16)
    x1 = (lane >= NOPE_DIM) & (lane < NOPE_DIM + ROPE_HALF)
    x2 = (lane >= NOPE_DIM + ROPE_HALF) & (lane < NOPE_DIM + ROPE_DIM)
    cos_q = jnp.where(lane < NOPE_DIM, 1.0, jnp.where(x1, shifted(NOPE_DIM), jnp.where(
        x2, shifted(NOPE_DIM + ROPE_HALF), 0.0)))
    sin_q = jnp.where(x1, -shifted(NOPE_DIM - ROPE_HALF), jnp.where(x2, shifted(NOPE_DIM), 0.0))
    vt = _dot_nt(wuvt_ref[...], ckv_b)
    vrow = lax.broadcasted_iota(jnp.int32, vt.shape, 0)
    vt_out[0, 0] = jnp.where(vrow % V_ROWS == V_DIM, 1.0, vt).astype(BF16)
    qn = _rms(zqs[:, :Q_LORA], gq_ref[...]).astype(BF16)
    q = _dot(qn, wuq_ref[...])
    cos_qs = cos_q * QK_SCALE
    sin_qs = sin_q * QK_SCALE
    for hd in range(MLA_HEADS):
        lanes = slice(hd * HEAD_PAD, (hd + 1) * HEAD_PAD)
        q_out[:, lanes] = _rope128(q[:, lanes], cos_qs, sin_qs, NOPE_DIM + ROPE_HALF).astype(BF16)
    zb = zs + bias_ref[...]
    log_sig = jnp.minimum(zb, 0.0) - jnp.log1p(jnp.exp(-jnp.abs(zb)))
    is_f = (lane >= GATE_F_LANE) & (lane < GATE_F_LANE + M_HEADS)
    gates = jnp.where(is_f, log_sig, zb)
    mk_out[...] = (_dot_nt(u, wmk_ref[...]) * (M_DIM ** -0.5)).astype(BF16)
    if cell_feature_major:
        zt = _dot_nt(wqvot_ref[...], u)
        mqt_out[0, 0] = zt[0:MLSTM_WIDTH].astype(BF16)
        unit_rows = (lax.broadcasted_iota(jnp.int32, (CELL_V_ROWS - M_DIM, n_tok), 0) == 0)
        for hd in range(M_HEADS):
            rows = slice(MLSTM_WIDTH + hd * M_DIM, MLSTM_WIDTH + (hd + 1) * M_DIM)
            mvt_out[0, 0, hd * CELL_V_ROWS:hd * CELL_V_ROWS + M_DIM, :] = zt[rows].astype(BF16)
            mvt_out[0, 0, hd * CELL_V_ROWS + M_DIM:(hd + 1) * CELL_V_ROWS, :] = (
                unit_rows.astype(F32).astype(BF16))
        mot_out[0, 0] = zt[2 * MLSTM_WIDTH:3 * MLSTM_WIDTH]
        gt_out[0, 0] = gates.T[GATE_I_LANE:GATE_I_LANE + 2 * M_HEADS, :]
    else:
        zm = _dot_nt(u, wqvot_ref[...])
        mq_out[...] = zm[:, 0:MLSTM_WIDTH].astype(BF16)
        mv_out[...] = zm[:, MLSTM_WIDTH:2 * MLSTM_WIDTH].astype(BF16)
        mo_out[...] = zm[:, 2 * MLSTM_WIDTH:3 * MLSTM_WIDTH]
        gate_out[...] = gates


def _proj_call(h, w, rope_tab, tm, cell_feature_major):
    t = h.shape[0]
    n_tab = rope_tab.shape[1] // tm
    n_seq = t // tm // n_tab
    row = lambda i: (i, 0)
    tab = lambda i: (0, i % n_tab)
    tok = lambda width, dtype: (jax.ShapeDtypeStruct((t, width), dtype),
                                pl.BlockSpec((tm, width), row))
    slab = lambda rows, dtype: (jax.ShapeDtypeStruct((n_seq, n_tab, rows, tm), dtype),
                                pl.BlockSpec((1, 1, rows, tm), lambda i: (i // n_tab, i % n_tab, 0, 0)))
    outs = [tok(MLA_PAD_WIDTH, BF16), tok(MLA_PAD_WIDTH, BF16), slab(MLA_HEADS * V_ROWS, BF16),
            tok(KV_LORA, F32), tok(ROPE_DIM, F32),
            (jax.ShapeDtypeStruct((n_seq, ROPE_DIM, n_tab * tm), F32),
             pl.BlockSpec((1, ROPE_DIM, tm), lambda i: (i // n_tab, 0, i % n_tab)))]
    if cell_feature_major:
        outs += [tok(MLSTM_WIDTH, BF16), slab(MLSTM_WIDTH, BF16), slab(M_HEADS * CELL_V_ROWS, BF16),
                 slab(MLSTM_WIDTH, F32), slab(2 * M_HEADS, F32)]
    else:
        outs += [tok(MLSTM_WIDTH, BF16)] * 3 + [tok(MLSTM_WIDTH, F32), tok(LANES, F32)]
    return pl.pallas_call(
        functools.partial(_proj_body, cell_feature_major),
        grid=(t // tm,),
        in_specs=[pl.BlockSpec((tm, D_MODEL), row), _resident((1, D_MODEL)),
                  _resident((Q_LORA + LANES, D_MODEL)), _resident((KV_LORA, D_MODEL)),
                  _resident((1, Q_LORA)), _resident((Q_LORA, MLA_PAD_WIDTH)),
                  _resident((1, KV_LORA)), _resident((KV_LORA, MLA_PAD_WIDTH)),
                  _resident((MLA_HEADS * V_ROWS, KV_LORA)),
                  _resident((1, LANES)),
                  pl.BlockSpec((ROPE_DIM, tm), tab),
                  _resident((MLSTM_WIDTH, D_MODEL)), _resident((3 * MLSTM_WIDTH, D_MODEL))],
        out_specs=[o[1] for o in outs],
        out_shape=[o[0] for o in outs],
        compiler_params=_params(1),
        name="proj",
    )(h, w["g_mix"], w["w_qs_t"], w["w_kv_t"], w["g_q"], w["w_uq"], w["g_kv"],
      w["w_uk"], w["w_uv_t"], w["gate_bias"], rope_tab, w["w_mk_t"], w["w_qvo_t"])


ATT_BLOCK = 512
ATT_HEADS = 8
ATT_LANES = ATT_HEADS * HEAD_PAD
ATT_V_ROWS = ATT_HEADS * V_ROWS
ATT_LOOKAHEAD = 3


def _flash_body(q_ref, k_ref, vt_ref, o_ref):
    qi = pl.program_id(2)
    blk = ATT_BLOCK
    head_lanes = [slice(hd * HEAD_PAD, (hd + 1) * HEAD_PAD) for hd in range(ATT_HEADS)]
    qs = [q_ref[0, :, lanes] for lanes in head_lanes]

    def step(j, carry, diagonal):
        start = pl.multiple_of(j * blk, blk)
        scores = [None] * ATT_HEADS
        if diagonal:
            key = lax.broadcasted_iota(jnp.int32, (blk, blk), 0)
            qry = lax.broadcasted_iota(jnp.int32, (blk, blk), 1)
            visible = key <= qry
        out = []
        for hd in range(ATT_HEADS):
            m, acc = carry[hd]
            for nxt in range(hd if hd else 0, min(hd + ATT_LOOKAHEAD, ATT_HEADS - 1) + 1):
                if scores[nxt] is None:
                    scores[nxt] = _dot_nt(k_ref[0, pl.ds(start, blk), head_lanes[nxt]], qs[nxt])
            s = jnp.where(visible, scores[hd], -jnp.inf) if diagonal else scores[hd]
            m_new = jnp.maximum(m, jnp.max(s, axis=0, keepdims=True))
            p = jnp.exp2(s - m_new).astype(BF16)
            vt = vt_ref[0, j, hd * V_ROWS:(hd + 1) * V_ROWS, :]
            acc = jnp.exp2(m - m_new) * acc + _dot(vt, p)
            out.append((m_new, acc))
        return tuple(out)

    init = tuple((jnp.full((1, blk), -jnp.inf, F32), jnp.zeros((V_ROWS, blk), F32))
                 for _ in range(ATT_HEADS))
    carry = lax.fori_loop(0, qi, lambda j, c: step(j, c, False), init)
    final = step(qi, carry, True)
    for pair in range(ATT_HEADS // 2):
        o_t = jnp.concatenate([final[hd][1][:V_DIM] / final[hd][1][V_DIM:V_DIM + 1]
                               for hd in (2 * pair, 2 * pair + 1)], axis=0)
        o_ref[0, :, pair * LANES:(pair + 1) * LANES] = o_t.T.astype(BF16)


def _flash_call(q, k, vt):
    b, s, _ = q.shape
    assert vt.shape == (b, s // ATT_BLOCK, MLA_HEADS * V_ROWS, ATT_BLOCK)
    qmap = lambda bi, gi, qi: (bi, qi, gi)
    return pl.pallas_call(
        _flash_body,
        grid=(b, MLA_HEADS // ATT_HEADS, s // ATT_BLOCK),
        in_specs=[pl.BlockSpec((1, ATT_BLOCK, ATT_LANES), qmap),
                  pl.BlockSpec((1, s, ATT_LANES), lambda bi, gi, qi: (bi, 0, gi),
                               pipeline_mode=pl.Buffered(1)),
                  pl.BlockSpec((1, s // ATT_BLOCK, ATT_V_ROWS, ATT_BLOCK),
                               lambda bi, gi, qi: (bi, 0, gi, 0), pipeline_mode=pl.Buffered(1))],
        out_specs=pl.BlockSpec((1, ATT_BLOCK, ATT_HEADS * V_DIM), qmap),
        out_shape=jax.ShapeDtypeStruct((b, s, MLA_WIDTH), BF16),
        compiler_params=_params(3),
        name="prompt_attn",
    )(q, k, vt)


def _mlstm_body(k_ref, qt_ref, vt_ref, ot_ref, gt_ref, gm_ref, hm_out, cxt_out, m_out, cxt_s, m_s):
    ci = pl.program_id(0)
    n_seq = k_ref.shape[0]
    L = CHUNK

    @pl.when(ci == 0)
    def _():
        cxt_s[...] = jnp.zeros_like(cxt_s)
        m_s[...] = jnp.zeros_like(m_s)

    s_idx = lax.broadcasted_iota(jnp.int32, (L, L), 0)
    t_idx = lax.broadcasted_iota(jnp.int32, (L, L), 1)
    causal = s_idx <= t_idx

    gate_rows = [gt_ref[sq, 0] for sq in range(n_seq)]
    prefix = [jnp.dot(rows, causal.astype(F32), precision=lax.Precision.HIGHEST,
                      preferred_element_type=F32) for rows in gate_rows]

    streams = []
    for sq in range(n_seq):
        for hd in range(M_HEADS):
            feat = slice(hd * M_DIM, (hd + 1) * M_DIM)
            idx = sq * M_HEADS + hd
            k = k_ref[sq, :, feat]
            qt = qt_ref[sq, 0, feat, :]
            cxt_prev = cxt_s[idx]
            streams.append(dict(
                sq=sq, hd=hd, feat=feat, idx=idx, k=k, cxt_prev=cxt_prev,
                vt=vt_ref[sq, 0, hd * CELL_V_ROWS:(hd + 1) * CELL_V_ROWS, :],
                m_prev=m_s[idx][0:1, :],
                qk=_dot(k, qt),
                qe=_dot(cxt_prev.astype(BF16), qt)))

    gates = []
    for sq in range(n_seq):
        rows = gate_rows[sq]
        b_rows = pltpu.roll(prefix[sq], M_HEADS, 0)
        head_row = lax.broadcasted_iota(jnp.int32, rows.shape, 0) < M_HEADS
        a_rows = jnp.where(head_row, rows - b_rows, 0.0)
        top = jnp.broadcast_to(jnp.max(a_rows, axis=1, keepdims=True), rows.shape)
        b_last = pltpu.roll(jnp.broadcast_to(jnp.sum(rows, axis=1, keepdims=True), rows.shape),
                            M_HEADS, 0)
        a_cols = jnp.concatenate([a_rows, jnp.zeros((L - SUBLANES, L), F32)], axis=0).T
        gates.append(dict(b_rows=b_rows, a_cols=a_cols, top=top, b_last=b_last,
                          src=jnp.exp(a_rows - top)))

    for st in streams:
        g, hd = gates[st["sq"]], st["hd"]
        weighted_vt = (st["vt"].astype(F32) * g["src"][hd:hd + 1, :]).astype(BF16)
        st["kv"] = _dot(weighted_vt, st["k"])

    for st in streams:
        g, hd = gates[st["sq"]], st["hd"]
        a_col = g["a_cols"][:, hd:hd + 1]
        run_max = jnp.max(jnp.where(causal, a_col, -jnp.inf), axis=0, keepdims=True)
        big_m = jnp.maximum(st["m_prev"], run_max)
        st["sqk"] = st["qk"] * jnp.where(causal, jnp.exp(a_col - big_m), 0.0)
        st["w_inter"] = jnp.exp(st["m_prev"] - big_m)
        st["floor"] = jnp.exp(-(g["b_rows"][hd:hd + 1, :] + big_m))
        st["top"] = g["top"][hd:hd + 1, :]
        st["b_last"] = g["b_last"][hd:hd + 1, :]

    for st in streams:
        ue = _dot(st["vt"], st["sqk"].astype(BF16))
        num = ue[:M_DIM] + st["qe"][:M_DIM] * st["w_inter"]
        den = ue[M_DIM:M_DIM + 1] + st["qe"][M_DIM:M_DIM + 1] * st["w_inter"]
        inv = 1.0 / jnp.maximum(jnp.abs(den), st["floor"])
        t = jax.nn.sigmoid(ot_ref[st["sq"], 0, st["feat"], :]) * num
        ms = jnp.sum(t * t, axis=0, keepdims=True) * (1.0 / M_DIM)
        scale = inv * lax.rsqrt(ms * inv * inv + EPS)
        hm_out[st["sq"], :, st["feat"]] = (t * scale * gm_ref[st["feat"], :]).T.astype(BF16)
        m_last = jnp.maximum(st["m_prev"], st["top"])
        keep = jnp.exp(st["m_prev"] - m_last)
        gain = jnp.exp(st["top"] - m_last)
        cxt_s[st["idx"]] = keep * st["cxt_prev"] + gain * st["kv"]
        m_s[st["idx"]] = jnp.broadcast_to(st["b_last"] + m_last, (SUBLANES, LANES))

    @pl.when(ci == pl.num_programs(0) - 1)
    def _():
        cxt_out[...] = cxt_s[...]
        m_out[...] = m_s[...]


def _mlstm_call(mk, mqt, mvt, mot, gt, gm_lanes):
    b, s, _ = mk.shape
    tile = mqt.shape[-1]
    per_tile = tile // CHUNK
    n_streams = b * M_HEADS
    slab = lambda rows: pl.BlockSpec((b, 1, rows, CHUNK),
                                     lambda ci: (0, ci // per_tile, 0, ci % per_tile))
    tok = pl.BlockSpec((b, CHUNK, MLSTM_WIDTH), lambda ci: (0, ci, 0))
    whole3 = lambda ci: (0, 0, 0)
    return pl.pallas_call(
        _mlstm_body,
        grid=(s // CHUNK,),
        in_specs=[tok, slab(MLSTM_WIDTH), slab(M_HEADS * CELL_V_ROWS), slab(MLSTM_WIDTH),
                  slab(2 * M_HEADS), _resident((MLSTM_WIDTH, LANES))],
        out_specs=[tok,
                   pl.BlockSpec((n_streams, CELL_V_ROWS, M_DIM), whole3),
                   pl.BlockSpec((n_streams, SUBLANES, LANES), whole3)],
        out_shape=[jax.ShapeDtypeStruct((b, s, MLSTM_WIDTH), BF16),
                   jax.ShapeDtypeStruct((n_streams, CELL_V_ROWS, M_DIM), F32),
                   jax.ShapeDtypeStruct((n_streams, SUBLANES, LANES), F32)],
        scratch_shapes=[pltpu.VMEM((n_streams, CELL_V_ROWS, M_DIM), F32),
                        pltpu.VMEM((n_streams, SUBLANES, LANES), F32)],
        compiler_params=_params(1),
        name="prompt_mlstm",
    )(mk, mqt, mvt, mot, gt, gm_lanes)


def _merge_body(h_ref, a_ref, hm_ref, p_ref, ga_ref, woa_ref, wom_ref, gff_ref, wg_ref, wu_ref,
                wd_ref, gple_ref, wpg_ref, wpp_ref, gfin_ref, y_ref):
    a = a_ref[...].astype(F32)
    an = _rms(a, ga_ref[...]).astype(BF16)
    h = h_ref[...] + _dot(an, woa_ref[...]) + _dot(hm_ref[...], wom_ref[...])
    h = h + _swiglu_half(_rms(h, gff_ref[...]).astype(BF16), wg_ref, wu_ref, wd_ref)
    gate = jax.nn.sigmoid(_dot(_rms(h, gple_ref[...]).astype(BF16), wpg_ref[...]))
    h = h + gate * _dot(p_ref[...].astype(BF16), wpp_ref[...])
    y_ref[...] = _rms(h, gfin_ref[...])


def _merge_call(h, a, hm, p, w, tm):
    t = h.shape[0]
    row = lambda i: (i, 0)
    return pl.pallas_call(
        _merge_body,
        grid=(t // tm,),
        in_specs=[pl.BlockSpec((tm, D_MODEL), row), pl.BlockSpec((tm, MLA_WIDTH), row),
                  pl.BlockSpec((tm, MLSTM_WIDTH), row), pl.BlockSpec((tm, PLE_DIM), row),
                  _resident((1, MLA_WIDTH)), _resident((MLA_WIDTH, D_MODEL)),
                  _resident((MLSTM_WIDTH, D_MODEL)), _resident((1, D_MODEL)),
                  _resident((D_MODEL, D_FF)), _resident((D_MODEL, D_FF)), _resident((D_FF, D_MODEL)),
                  _resident((1, D_MODEL)), _resident((D_MODEL, D_MODEL)),
                  _resident((PLE_DIM, D_MODEL)), _resident((1, D_MODEL))],
        out_specs=pl.BlockSpec((tm, D_MODEL), row),
        out_shape=jax.ShapeDtypeStruct((t, D_MODEL), F32),
        compiler_params=_params(1),
        name="merge",
    )(h, a, hm, p, w["g_attn"], w["w_out_a"], w["w_out_m"], w["g_ff2"], w["w_ff2_gate"],
      w["w_ff2_up"], w["w_ff2_down"], w["g_ple"], w["w_ple_gate"], w["w_ple_proj"], w["g_final"])


N_SLOTS = 2
SAMPLE_CHUNKS = 4
SAMPLE_CHUNK_PAGES = N_PAGES // SAMPLE_CHUNKS
SAMPLE_CHUNK_KEYS = SAMPLE_CHUNK_PAGES * PAGE_SIZE


def _absorb_body(q_ref, wabs_ref, qx_ref):
    for hd in range(MLA_HEADS):
        lanes = slice(hd * HEAD_PAD, (hd + 1) * HEAD_PAD)
        qx_ref[hd, :, 0, :] = _dot(q_ref[:, lanes], wabs_ref[hd])


def _absorb_call(q, w_abs):
    nb = q.shape[0]
    return pl.pallas_call(
        _absorb_body,
        out_shape=jax.ShapeDtypeStruct((MLA_HEADS, nb, 1, KV_LORA + LANES), F32),
        name="sample_absorb",
    )(q, w_abs)


def _sample_attn_body(pt_ref, qx_ref, ckv_ref, kr_ref, cache_c, cache_r, wuv_ref,
                      o_ref, cbuf, rbuf, sem_c, sem_r):
    b = pl.program_id(0)
    nb = pl.num_programs(0)
    slot = b % N_SLOTS

    def copies(bi, sl):
        out = []
        for j in range(N_PAGES):
            page = pt_ref[bi, j]
            out.append(pltpu.make_async_copy(cache_c.at[page], cbuf.at[sl, j], sem_c.at[sl]))
            out.append(pltpu.make_async_copy(
                cache_r.at[page], rbuf.at[sl, :, pl.ds(j * PAGE_SIZE, PAGE_SIZE)], sem_r.at[sl]))
        return out

    @pl.when(b == 0)
    def _():
        for bi in range(min(N_SLOTS, DEC_BATCH)):
            for cp in copies(bi, bi):
                cp.start()

    q_ext = qx_ref[:, 0, 0, :]
    q_abs = q_ext[:, :KV_LORA].astype(BF16)
    q_rope = q_ext[:, KV_LORA:KV_LORA + ROPE_DIM].astype(BF16)

    c_new = ckv_ref[0].astype(BF16).astype(F32)
    r_new = kr_ref[0].astype(BF16).astype(F32)
    s_new = (jnp.sum(q_abs.astype(F32) * c_new, axis=1, keepdims=True)
             + jnp.sum(q_rope.astype(F32) * r_new, axis=1, keepdims=True))

    for cp in copies(b, slot):
        cp.wait()

    def chunk_keys(i):
        pages = slice(i * SAMPLE_CHUNK_PAGES, (i + 1) * SAMPLE_CHUNK_PAGES)
        keys = slice(i * SAMPLE_CHUNK_KEYS, (i + 1) * SAMPLE_CHUNK_KEYS)
        kc = cbuf[slot, pages].reshape(SAMPLE_CHUNK_KEYS, KV_LORA).astype(BF16)
        return kc, _dot_nt(q_abs, kc) + _dot(q_rope, rbuf[slot, :, keys].astype(BF16))

    m = s_new
    l = jnp.ones_like(s_new)
    acc = jnp.broadcast_to(c_new, (MLA_HEADS, KV_LORA))
    kc, s = chunk_keys(0)
    for i in range(SAMPLE_CHUNKS):
        nxt = chunk_keys(i + 1) if i + 1 < SAMPLE_CHUNKS else None
        m_new = jnp.maximum(m, jnp.max(s, axis=1, keepdims=True))
        alpha = jnp.exp2(m - m_new)
        p = jnp.exp2(s - m_new)
        l = alpha * l + jnp.sum(p, axis=1, keepdims=True)
        acc = alpha * acc + _dot(p.astype(BF16), kc)
        m = m_new
        if nxt is not None:
            kc, s = nxt
    o_lat = (acc / l).astype(BF16)
    res = _dot(o_lat, wuv_ref[...])
    own_v = (lax.broadcasted_iota(jnp.int32, res.shape, 1) // V_DIM
             == lax.broadcasted_iota(jnp.int32, res.shape, 0))
    o_ref[0] = jnp.sum(jnp.where(own_v, res, 0.0), axis=0, keepdims=True).astype(BF16)

    @pl.when(b + N_SLOTS < nb)
    def _():
        for cp in copies(b + N_SLOTS, slot):
            cp.start()


def _sample_attn_call(page_table, q_ext, ckv, kr, cache_c, cache_r, w_uv):
    nb = q_ext.shape[1]
    tok = lambda bi, pt: (bi, 0, 0)
    whole = lambda shape: pl.BlockSpec(shape, lambda bi, pt: (0,) * len(shape),
                                       pipeline_mode=pl.Buffered(1))
    grid_spec = pltpu.PrefetchScalarGridSpec(
        num_scalar_prefetch=1,
        grid=(nb,),
        in_specs=[pl.BlockSpec((MLA_HEADS, 1, 1, KV_LORA + LANES), lambda bi, pt: (0, bi, 0, 0)),
                  pl.BlockSpec((1, 1, KV_LORA), tok),
                  pl.BlockSpec((1, 1, ROPE_DIM), tok),
                  pl.BlockSpec(memory_space=pl.ANY), pl.BlockSpec(memory_space=pl.ANY),
                  whole((KV_LORA, MLA_WIDTH))],
        out_specs=pl.BlockSpec((1, 1, MLA_WIDTH), tok),
        scratch_shapes=[pltpu.VMEM((N_SLOTS, N_PAGES, PAGE_SIZE, KV_LORA), F32),
                        pltpu.VMEM((N_SLOTS, ROPE_DIM, PAST_LEN), F32),
                        pltpu.SemaphoreType.DMA((N_SLOTS,)), pltpu.SemaphoreType.DMA((N_SLOTS,))])
    return pl.pallas_call(
        _sample_attn_body,
        grid_spec=grid_spec,
        out_shape=jax.ShapeDtypeStruct((nb, 1, MLA_WIDTH), BF16),
        compiler_params=_params(1),
        name="sample_attn",
    )(page_table, q_ext, ckv, kr, cache_c, cache_r, w_uv)


SAMPLE_ROWS = 32


def _sample_mlstm_body(q_ref, k_ref, v_ref, o_ref, g_ref, gm_ref, c_ref, n_ref, m_ref,
                       hm_out, c_out, n_out, m_out, qc_s):
    hd = pl.program_id(1)
    g = g_ref[...]
    lane = lax.broadcasted_iota(jnp.int32, g.shape, 1)
    ig = jnp.sum(jnp.where(lane == GATE_I_LANE + hd, g, 0.0), axis=1, keepdims=True)
    lf = jnp.sum(jnp.where(lane == GATE_F_LANE + hd, g, 0.0), axis=1, keepdims=True)
    m_all = m_ref[...]
    head_lane = lax.broadcasted_iota(jnp.int32, m_all.shape, 1)
    m0 = jnp.sum(jnp.where(head_lane == hd, m_all, 0.0), axis=1, keepdims=True)
    q = q_ref[...].astype(F32)
    k = k_ref[...].astype(F32)
    v = v_ref[...].astype(F32)
    n0 = n_ref[...]
    m_new = jnp.maximum(lf + m0, ig)
    keep = jnp.exp(lf + m0 - m_new)
    w_src = jnp.exp(ig - m_new)
    sqk = jnp.sum(q * k, axis=1, keepdims=True) * w_src
    wk = w_src * k
    rows = q.shape[0]
    pad = jnp.zeros((M_DIM - rows, M_DIM), F32)
    q_t = jnp.concatenate([q, pad], axis=0).T
    wk_t = jnp.concatenate([wk, pad], axis=0).T
    keep_lanes = jnp.broadcast_to(keep, (rows, M_DIM))
    for b in range(rows):
        c_b = c_ref[b, 0]
        qc_s[b:b + 1, :] = jnp.sum(q_t[:, b:b + 1] * c_b, axis=0, keepdims=True)
        c_out[b, 0] = keep_lanes[b:b + 1, :] * c_b + wk_t[:, b:b + 1] * v[b:b + 1, :]
    qc = qc_s[...]
    num = sqk * v + qc * keep
    den = sqk + jnp.sum(q * n0, axis=1, keepdims=True) * keep
    den = jnp.maximum(jnp.abs(den), jnp.exp(-m_new))
    hh = jax.nn.sigmoid(o_ref[...]) * (num / den)
    hm_out[...] = _rms(hh, gm_ref[...]).astype(BF16)
    n_out[...] = keep * n0 + wk

    @pl.when(hd == 0)
    def _():
        m_out[...] = jnp.broadcast_to(m_new, m_all.shape)

    @pl.when(hd > 0)
    def _():
        m_out[...] = jnp.where(head_lane == hd, m_new, m_out[...])


def _sample_mlstm_call(mq, mk, mv, mo, gates, gm, c0, n0, m0):
    nb = mq.shape[0]
    rows = SAMPLE_ROWS
    tok = lambda bi, hi: (bi, hi)
    return pl.pallas_call(
        _sample_mlstm_body,
        grid=(nb // rows, M_HEADS),
        in_specs=[pl.BlockSpec((rows, M_DIM), tok)] * 4
                 + [pl.BlockSpec((rows, LANES), lambda bi, hi: (bi, 0)),
                    pl.BlockSpec((1, M_DIM), lambda bi, hi: (0, hi)),
                    pl.BlockSpec((rows, 1, M_DIM, M_DIM), lambda bi, hi: (bi, hi, 0, 0)),
                    pl.BlockSpec((rows, M_DIM), tok),
                    pl.BlockSpec((rows, M_HEADS), lambda bi, hi: (bi, 0))],
        out_specs=[pl.BlockSpec((rows, M_DIM), tok),
                   pl.BlockSpec((rows, 1, M_DIM, M_DIM), lambda bi, hi: (bi, hi, 0, 0)),
                   pl.BlockSpec((rows, M_DIM), tok),
                   pl.BlockSpec((rows, M_HEADS), lambda bi, hi: (bi, 0))],
        out_shape=[jax.ShapeDtypeStruct((nb, MLSTM_WIDTH), BF16),
                   jax.ShapeDtypeStruct((nb, M_HEADS, M_DIM, M_DIM), F32),
                   jax.ShapeDtypeStruct((nb, MLSTM_WIDTH), F32),
                   jax.ShapeDtypeStruct((nb, M_HEADS), F32)],
        scratch_shapes=[pltpu.VMEM((rows, M_DIM), F32)],
        compiler_params=_params(2),
        name="sample_mlstm",
    )(mq, mk, mv, mo, gates, gm, c0, n0, m0)


def _pad_heads(w, head_dim):
    rows = w.shape[0]
    w = w.reshape(rows, MLA_HEADS, head_dim)
    w = jnp.pad(w, ((0, 0), (0, 0), (0, HEAD_PAD - head_dim)))
    return w.reshape(rows, MLA_PAD_WIDTH)


def _rope_tables(pos):
    inv = ROPE_THETA ** (-jnp.arange(ROPE_HALF, dtype=F32) / ROPE_HALF)
    ang = inv[:, None] * pos.astype(F32)[None, :]
    return jnp.concatenate([jnp.cos(ang), jnp.sin(ang)], axis=0)


def _prep_weights(g_ff1, w_ff1_gate, w_ff1_up, w_ff1_down, g_mix, w_in, g_q, w_uq, g_kv, w_uk,
                  w_uv, b_gate_i, b_gate_f, g_attn_out, g_mlstm_out, w_out, g_ff2, w_ff2_gate,
                  w_ff2_up, w_ff2_down, g_ple, w_ple_gate, w_ple_proj, g_final):
    bf = lambda a: a.astype(BF16)
    row = lambda a: a.reshape(1, -1).astype(F32)
    off_kv, off_kr = Q_LORA, Q_LORA + KV_LORA
    off_m = off_kr + ROPE_DIM
    off_i = off_m + 4 * MLSTM_WIDTH
    small_pad = LANES - ROPE_DIM - 2 * M_HEADS
    w_t = bf(w_in.T)
    w_qs_t = jnp.concatenate([w_t[:off_kv], w_t[off_kr:off_m], w_t[off_i:off_i + 2 * M_HEADS],
                              jnp.zeros((small_pad, D_MODEL), BF16)], axis=0)
    w_qvo_t = jnp.concatenate([w_t[off_m:off_m + MLSTM_WIDTH], w_t[off_m + 2 * MLSTM_WIDTH:off_i]],
                              axis=0)
    gate_bias = jnp.concatenate([jnp.zeros((ROPE_DIM,), F32), b_gate_i, b_gate_f,
                                 jnp.zeros((small_pad,), F32)])
    src = jnp.arange(LANES)[:, None]
    dst = jnp.arange(MLA_PAD_WIDTH)[None, :]
    w_abs = jnp.pad(w_uk.reshape(KV_LORA, MLA_HEADS, NOPE_DIM).transpose(1, 2, 0),
                    ((0, 0), (0, HEAD_PAD - NOPE_DIM), (0, 0))).reshape(MLA_PAD_WIDTH, KV_LORA)
    rope_sel = (dst.T % HEAD_PAD) == (src.T + NOPE_DIM)
    rope_sel = rope_sel & (src.T < ROPE_DIM)
    w_abs = jnp.concatenate([w_abs, rope_sel.astype(F32)], axis=1)
    return dict(
        g_ff1=row(g_ff1), w_ff1_gate=bf(w_ff1_gate), w_ff1_up=bf(w_ff1_up), w_ff1_down=bf(w_ff1_down),
        g_mix=row(g_mix), w_qs_t=w_qs_t, w_kv_t=w_t[off_kv:off_kr],
        w_mk_t=w_t[off_m + MLSTM_WIDTH:off_m + 2 * MLSTM_WIDTH], w_qvo_t=w_qvo_t,
        g_q=row(g_q), w_uq=bf(_pad_heads(w_uq, NOPE_DIM + ROPE_DIM)),
        g_kv=row(g_kv), w_uk=bf(_pad_heads(w_uk, NOPE_DIM)), w_uv=bf(w_uv),
        w_uv_t=bf(jnp.pad(w_uv.T.reshape(MLA_HEADS, V_DIM, KV_LORA),
                          ((0, 0), (0, V_ROWS - V_DIM), (0, 0))).reshape(MLA_HEADS * V_ROWS, KV_LORA)),
        gate_bias=row(gate_bias), w_abs=bf(w_abs).reshape(MLA_HEADS, HEAD_PAD, KV_LORA + LANES),
        g_attn=row(g_attn_out), w_out_a=bf(w_out[:MLA_WIDTH]),
        w_out_m=bf(w_out[MLA_WIDTH:]), g_mlstm=row(g_mlstm_out),
        g_mlstm_lanes=jnp.broadcast_to(g_mlstm_out.reshape(MLSTM_WIDTH, 1).astype(F32),
                                       (MLSTM_WIDTH, LANES)),
        g_ff2=row(g_ff2), w_ff2_gate=bf(w_ff2_gate), w_ff2_up=bf(w_ff2_up), w_ff2_down=bf(w_ff2_down),
        g_ple=row(g_ple), w_ple_gate=bf(w_ple_gate), w_ple_proj=bf(w_ple_proj), g_final=row(g_final))


PROMPT_TILE = 512


def kernel(x_prompt, x_sample, p_prompt, p_sample, cache_ckv, cache_krope, state_C, state_n, state_m, page_table, g_ff1, w_ff1_gate, w_ff1_up, w_ff1_down, g_mix, w_in, g_q, w_uq, g_kv, w_uk, w_uv, b_gate_i, b_gate_f, g_attn_out, g_mlstm_out, w_out, g_ff2, w_ff2_gate, w_ff2_up, w_ff2_down, g_ple, w_ple_gate, w_ple_proj, g_final):
    assert w_in.shape[0] == 1, "single-layer trunk"
    w = _prep_weights(g_ff1[0], w_ff1_gate[0], w_ff1_up[0], w_ff1_down[0], g_mix[0], w_in[0],
                      g_q[0], w_uq[0], g_kv[0], w_uk[0], w_uv[0], b_gate_i[0], b_gate_f[0],
                      g_attn_out[0], g_mlstm_out[0], w_out[0], g_ff2[0], w_ff2_gate[0],
                      w_ff2_up[0], w_ff2_down[0], g_ple[0], w_ple_gate[0], w_ple_proj[0], g_final)
    nb_p, seq, _ = x_prompt.shape
    nb_s = x_sample.shape[0]
    t_p = nb_p * seq

    tab_p = _rope_tables(jnp.arange(seq))
    h_p, h_s = _ffn_call(x_prompt.reshape(t_p, D_MODEL), x_sample.reshape(nb_s, D_MODEL),
                         w["g_ff1"], w["w_ff1_gate"], w["w_ff1_up"], w["w_ff1_down"], PROMPT_TILE)
    q_p, k_p, vt_p, ckv_p, _, krt_p, mk_p, mqt_p, mvt_p, mot_p, gt_p = _proj_call(
        h_p, w, tab_p, PROMPT_TILE, cell_feature_major=True)
    seq3 = lambda a: a.reshape(nb_p, seq, a.shape[-1])
    a_p = _flash_call(seq3(q_p), seq3(k_p), vt_p)
    hm_p, cxt_p, m_p = _mlstm_call(seq3(mk_p), mqt_p, mvt_p, mot_p, gt_p, w["g_mlstm_lanes"])
    cxt_p = cxt_p.reshape(nb_p, M_HEADS, CELL_V_ROWS, M_DIM)
    y_p = _merge_call(h_p, a_p.reshape(t_p, MLA_WIDTH), hm_p.reshape(t_p, MLSTM_WIDTH),
                      p_prompt.reshape(t_p, PLE_DIM), w, PROMPT_TILE)

    tab_s = _rope_tables(jnp.full((nb_s,), PAST_LEN, jnp.int32))
    q_s, _, _, ckv_s, kr_s, krt_s, mq_s, mk_s, mv_s, mo_s, gates_s = _proj_call(
        h_s, w, tab_s, nb_s, cell_feature_major=False)
    n_phys = cache_ckv.shape[1]
    a_s = _sample_attn_call(
        page_table, _absorb_call(q_s, w["w_abs"]), ckv_s.reshape(nb_s, 1, KV_LORA),
        kr_s.reshape(nb_s, 1, ROPE_DIM), cache_ckv.reshape(n_phys, PAGE_SIZE, KV_LORA),
        jnp.swapaxes(cache_krope.reshape(n_phys, PAGE_SIZE, ROPE_DIM), 1, 2), w["w_uv"])
    hm_s, c_s, n_s, m_s = _sample_mlstm_call(
        mq_s, mk_s, mv_s, mo_s, gates_s, w["g_mlstm"], state_C[0].astype(F32),
        state_n[0].astype(F32).reshape(nb_s, MLSTM_WIDTH), state_m[0].astype(F32))
    y_s = _merge_call(h_s, a_s.reshape(nb_s, MLA_WIDTH), hm_s, p_sample.reshape(nb_s, PLE_DIM),
                      w, nb_s)

    return (y_p.reshape(nb_p, seq, D_MODEL), y_s.reshape(nb_s, 1, D_MODEL),
            ckv_p.reshape(1, nb_p, seq, KV_LORA), jnp.swapaxes(krt_p, 1, 2)[None],
            jnp.swapaxes(cxt_p[:, :, :M_DIM, :], 2, 3)[None], cxt_p[None, :, :, M_DIM, :],
            m_p[:, 0, 0].reshape(1, nb_p, M_HEADS),
            ckv_s.reshape(1, nb_s, 1, KV_LORA), jnp.swapaxes(krt_s, 1, 2).reshape(1, nb_s, 1, ROPE_DIM),
            c_s[None], n_s.reshape(1, nb_s, M_HEADS, M_DIM), m_s[None])
```
